```python
import jax, jax.numpy as jnp
from jax import lax
import numpy as np

D_MODEL = 1024
BATCH = 4
SEQ = 4096
DEPTH = 4

HEAD_DIM = 64
N_MEM = 256
MEM_HEADS = 4
CROSS_WIDTH = MEM_HEADS * HEAD_DIM
MIX_WIDTH = 12 * HEAD_DIM
ATTN_WIDTH = MIX_WIDTH + CROSS_WIDTH
EPS = 1e-6
NEG = -1e30
MLA_HEADS = 12
MLA_Q_RANK = 384
MLA_KV_RANK = 256
MLA_NOPE = 64
MLA_ROPE = 32
MLA_V = 64
MLA_QK = MLA_NOPE + MLA_ROPE
ROPE_THETA = 10000.0
Q_BLOCK = 128
MLA_IN = MLA_Q_RANK + MLA_KV_RANK + MLA_ROPE + CROSS_WIDTH
SWA_Q_HEADS = 12
SWA_KV_HEADS = 4
SWA_GROUP = SWA_Q_HEADS // SWA_KV_HEADS
WINDOW = 128
SWA_IN = (SWA_Q_HEADS + 2 * SWA_KV_HEADS) * HEAD_DIM + CROSS_WIDTH
D_FF = 4 * D_MODEL
N_MLA_LAYERS = (DEPTH + 1) // 2
N_SWA_LAYERS = DEPTH // 2

kernel_name = "hybrid_mla_swa_sink_memx_sqrelu"


def rmsnorm(x, g):
    xf = x.astype(jnp.float32)
    y = xf * lax.rsqrt(jnp.mean(xf * xf, axis=-1, keepdims=True) + EPS)
    return (y * g.astype(jnp.float32)).astype(x.dtype)


def rope(x, positions):
    r = x.shape[-1]
    half = r // 2
    inv = ROPE_THETA ** (-(jnp.arange(half, dtype=jnp.float32) * 2.0) / r)
    ang = positions.astype(jnp.float32)[..., None] * inv
    cos = jnp.cos(ang)[:, :, None, :]
    sin = jnp.sin(ang)[:, :, None, :]
    xf = x.astype(jnp.float32)
    x1, x2 = xf[..., :half], xf[..., half:]
    out = jnp.concatenate([x1 * cos - x2 * sin, x1 * sin + x2 * cos], axis=-1)
    return out.astype(x.dtype)


def alibi_slopes(n_heads):
    return 2.0 ** (-8.0 * (jnp.arange(n_heads, dtype=jnp.float32) + 1.0) / n_heads)


def causal_dense_attention(q, k, v):
    b, s, h, dk = q.shape
    dv = v.shape[-1]
    nb = s // Q_BLOCK
    scale = dk ** -0.5
    qb = q.reshape(b, nb, Q_BLOCK, h, dk).transpose(1, 0, 2, 3, 4)
    k_idx = jnp.arange(s)

    def one_block(args):
        q_blk, n = args
        t_idx = n * Q_BLOCK + jnp.arange(Q_BLOCK)
        sc = jnp.einsum('bqhd,bkhd->bhqk', q_blk, k,
                        preferred_element_type=jnp.float32) * scale
        mask = k_idx[None, :] <= t_idx[:, None]
        sc = jnp.where(mask[None, None], sc, NEG)
        p = jax.nn.softmax(sc, axis=-1).astype(v.dtype)
        return jnp.einsum('bhqk,bkhd->bqhd', p, v)

    out = lax.map(one_block, (qb, jnp.arange(nb)))
    return out.transpose(1, 0, 2, 3, 4).reshape(b, s, h, dv)


def mla_mixer(hn, positions, w_in, q_norm_g, kv_norm_g, w_uq, w_ukv):
    b, s, _ = hn.shape
    proj = hn @ w_in
    c_q = proj[..., :MLA_Q_RANK]
    c_kv = proj[..., MLA_Q_RANK:MLA_Q_RANK + MLA_KV_RANK]
    k_r = proj[..., MLA_Q_RANK + MLA_KV_RANK:MLA_Q_RANK + MLA_KV_RANK + MLA_ROPE]
    q_cross = proj[..., MLA_Q_RANK + MLA_KV_RANK + MLA_ROPE:]
    q = (rmsnorm(c_q, q_norm_g) @ w_uq).reshape(b, s, MLA_HEADS, MLA_QK)
    q = jnp.concatenate([q[..., :MLA_NOPE], rope(q[..., MLA_NOPE:], positions)], axis=-1)
    kv = (rmsnorm(c_kv, kv_norm_g) @ w_ukv).reshape(b, s, MLA_HEADS, MLA_NOPE + MLA_V)
    k_nope, v = kv[..., :MLA_NOPE], kv[..., MLA_NOPE:]
    k_rope = jnp.broadcast_to(rope(k_r[:, :, None, :], positions),
                              (b, s, MLA_HEADS, MLA_ROPE))
    k = jnp.concatenate([k_nope, k_rope], axis=-1)
    o = causal_dense_attention(q, k, v)
    return o.reshape(b, s, MLA_HEADS * MLA_V), q_cross


def _with_prev_block(a):
    pad = [(0, 0), (1, 0)] + [(0, 0)] * (a.ndim - 2)
    prev = jnp.pad(a[:, :-1], pad)
    return jnp.concatenate([prev, a], axis=2)


def swa_mixer(hn, positions, w_in, sinks):
    b, s, _ = hn.shape
    nb = s // WINDOW
    proj = hn @ w_in
    nq = SWA_Q_HEADS * HEAD_DIM
    nk = SWA_KV_HEADS * HEAD_DIM
    q = proj[..., :nq].reshape(b, nb, WINDOW, SWA_KV_HEADS, SWA_GROUP, HEAD_DIM)
    k = proj[..., nq:nq + nk].reshape(b, nb, WINDOW, SWA_KV_HEADS, HEAD_DIM)
    v = proj[..., nq + nk:nq + 2 * nk].reshape(b, nb, WINDOW, SWA_KV_HEADS, HEAD_DIM)
    q_cross = proj[..., nq + 2 * nk:]
    kk = _with_prev_block(k)
    vv = _with_prev_block(v)
    pos_q = positions.reshape(b, nb, WINDOW)
    pos_k = _with_prev_block(pos_q)
    sc = jnp.einsum('bnqhgd,bnkhd->bnhgqk', q, kk,
                    preferred_element_type=jnp.float32) * (HEAD_DIM ** -0.5)
    dist = (pos_q[..., :, None] - pos_k[..., None, :]).astype(jnp.float32)
    slopes = alibi_slopes(SWA_Q_HEADS).reshape(SWA_KV_HEADS, SWA_GROUP)
    sc = sc - slopes[None, None, :, :, None, None] * dist[:, :, None, None]
    qi = jnp.arange(WINDOW)[:, None]
    kj = jnp.arange(2 * WINDOW)[None, :]
    rel = WINDOW + qi - kj
    band = (rel >= 0) & (rel < WINDOW)
    valid = band[None] & ((jnp.arange(nb)[:, None, None] > 0) | (kj >= WINDOW)[None])
    sc = jnp.where(valid[None, :, None, None], sc, NEG)
    sink = sinks.astype(jnp.float32).reshape(SWA_KV_HEADS, SWA_GROUP)[None, None, :, :, None, None]
    sink = jnp.broadcast_to(sink, sc.shape[:-1] + (1,))
    p = jax.nn.softmax(jnp.concatenate([sc, sink], axis=-1), axis=-1)[..., :-1]
    o = jnp.einsum('bnhgqk,bnkhd->bnqhgd', p.astype(vv.dtype), vv)
    return o.reshape(b, s, SWA_Q_HEADS * HEAD_DIM), q_cross


def memory_cross_attention(q_cross, mem_n, w_mem_kv):
    b, s, _ = q_cross.shape
    kv = (mem_n @ w_mem_kv).reshape(b, N_MEM, 2, MEM_HEADS, HEAD_DIM)
    k, v = kv[:, :, 0], kv[:, :, 1]
    q = q_cross.reshape(b, s, MEM_HEADS, HEAD_DIM)
    sc = jnp.einsum('bshd,bmhd->bhsm', q, k,
                    preferred_element_type=jnp.float32) * (HEAD_DIM ** -0.5)
    p = jax.nn.softmax(sc, axis=-1).astype(v.dtype)
    return jnp.einsum('bhsm,bmhd->bshd', p, v).reshape(b, s, CROSS_WIDTH)


def squared_relu_mlp(h, w_up, w_down):
    a = jax.nn.relu(h @ w_up)
    return (a * a) @ w_down


def setup_inputs(seed: int = 0) -> dict:
    key = jax.random.key(seed)
    ks = jax.random.split(key, 20)

    def w(k, shape, fan_in):
        return jax.random.normal(k, shape, jnp.float32) * (fan_in ** -0.5)

    def gain(k, shape):
        return 1.0 + 0.02 * jax.random.normal(k, shape, jnp.float32)

    x = jax.random.normal(ks[0], (BATCH, SEQ, D_MODEL), jnp.float32)
    mem = jax.random.normal(ks[1], (BATCH, N_MEM, D_MODEL), jnp.float32)
    offsets = jax.random.randint(ks[2], (BATCH, 1), 0, 1024, dtype=jnp.int32)
    positions = (offsets + jnp.arange(SEQ, dtype=jnp.int32)[None, :]).astype(jnp.int32)
    return {
        "x": x,
        "mem": mem,
        "positions": positions,
        "attn_norm_g": gain(ks[3], (DEPTH, D_MODEL)),
        "mlp_norm_g": gain(ks[4], (DEPTH, D_MODEL)),
        "mem_norm_g": gain(ks[5], (D_MODEL,)),
        "final_norm_g": gain(ks[6], (D_MODEL,)),
        "mla_w_in": w(ks[7], (N_MLA_LAYERS, D_MODEL, MLA_IN), D_MODEL),
        "mla_q_norm_g": gain(ks[8], (N_MLA_LAYERS, MLA_Q_RANK)),
        "mla_kv_norm_g": gain(ks[9], (N_MLA_LAYERS, MLA_KV_RANK)),
        "mla_w_uq": w(ks[10], (N_MLA_LAYERS, MLA_Q_RANK, MLA_HEADS * MLA_QK), MLA_Q_RANK),
        "mla_w_ukv": w(ks[11], (N_MLA_LAYERS, MLA_KV_RANK, MLA_HEADS * (MLA_NOPE + MLA_V)), MLA_KV_RANK),
        "swa_w_in": w(ks[12], (N_SWA_LAYERS, D_MODEL, SWA_IN), D_MODEL),
        "swa_sinks": 0.5 * jax.random.normal(ks[13], (N_SWA_LAYERS, SWA_Q_HEADS), jnp.float32),
        "w_mem_kv": w(ks[14], (DEPTH, D_MODEL, 2 * CROSS_WIDTH), D_MODEL),
        "w_o": w(ks[15], (DEPTH, ATTN_WIDTH, D_MODEL), ATTN_WIDTH),
        "mlp_w_up": w(ks[16], (DEPTH, D_MODEL, D_FF), D_MODEL),
        "mlp_w_down": w(ks[17], (DEPTH, D_FF, D_MODEL), D_FF),
    }


def reference(x, mem, positions, attn_norm_g, mlp_norm_g, mem_norm_g, final_norm_g,
              mla_w_in, mla_q_norm_g, mla_kv_norm_g, mla_w_uq, mla_w_ukv,
              swa_w_in, swa_sinks, w_mem_kv, w_o, mlp_w_up, mlp_w_down):
    mem_n = rmsnorm(mem, mem_norm_g)
    for i in range(DEPTH):
        j = i // 2
        hn = rmsnorm(x, attn_norm_g[i])
        if i % 2 == 0:
            mix, q_cross = mla_mixer(hn, positions, mla_w_in[j], mla_q_norm_g[j],
                                     mla_kv_norm_g[j], mla_w_uq[j], mla_w_ukv[j])
        else:
            mix, q_cross = swa_mixer(hn, positions, swa_w_in[j], swa_sinks[j])
        cross = memory_cross_attention(q_cross, mem_n, w_mem_kv[i])
        x = x + jnp.concatenate([mix, cross], axis=-1) @ w_o[i]
        x = x + squared_relu_mlp(rmsnorm(x, mlp_norm_g[i]), mlp_w_up[i], mlp_w_down[i])
    return rmsnorm(x, final_norm_g)
```

```python
import functools

import jax
import jax.numpy as jnp
from jax import lax
from jax.experimental import pallas as pl
from jax.experimental.pallas import tpu as pltpu

F32 = jnp.float32
BF16 = jnp.bfloat16

D_MODEL = 1024
HEAD_DIM = 64
N_MEM = 256
CROSS_WIDTH = 256
MIX_WIDTH = 768
EPS = 1e-6
NEG = -1e30
MLA_HEADS = 12
MLA_Q_RANK = 384
MLA_KV_RANK = 256
MLA_NOPE = 64
MLA_ROPE = 32
MLA_QK = MLA_NOPE + MLA_ROPE
ROPE_THETA = 10000.0
SWA_Q_HEADS = 12
SWA_KV_HEADS = 4
SWA_GROUP = 3
WINDOW = 128
D_FF = 4 * D_MODEL

LANES = 128
QK_PAD = LANES
ROPE_HALF = MLA_ROPE // 2

ROW_TILE = 512
ATTN_TQ = 512
ATTN_TK = 512
SWA_ROWS = 512
FF_CHUNK = 1024
VMEM_LIMIT = 48 * 1024 * 1024


def _rms(x, g):
    ms = jnp.mean(x * x, axis=-1, keepdims=True)
    return x * lax.rsqrt(ms + EPS) * g


def _dot(a, b):
    return jnp.dot(a, b, preferred_element_type=F32)


def _dot_nt(a, b):
    return lax.dot_general(a, b, (((1,), (1,)), ((), ())), preferred_element_type=F32)


def _low_half():
    return lax.broadcasted_iota(jnp.int32, (1, LANES), 1) < HEAD_DIM


def _const_spec(shape):
    nd = len(shape)
    return pl.BlockSpec(shape, lambda *_: (0,) * nd, pipeline_mode=pl.Buffered(1))


def _params(*sem):
    return pltpu.CompilerParams(dimension_semantics=sem, vmem_limit_bytes=VMEM_LIMIT)


def _rope_table_kernel(pos_ref, inv_ref, cos_ref, sin_ref):
    ang = pos_ref[...] * inv_ref[...]
    cos_ref[...] = jnp.cos(ang)
    sin_ref[...] = jnp.sin(ang)


def _rope_tables(pos_col, inv_lane):
    t = pos_col.shape[0]
    tm = 2048
    return pl.pallas_call(
        _rope_table_kernel,
        grid=(t // tm,),
        in_specs=[pl.BlockSpec((tm, 1), lambda i: (i, 0)),
                  pl.BlockSpec((1, LANES), lambda i: (0, 0))],
        out_specs=[pl.BlockSpec((tm, LANES), lambda i: (i, 0))] * 2,
        out_shape=[jax.ShapeDtypeStruct((t, LANES), F32)] * 2,
        compiler_params=_params("parallel"),
        name="rope_tables",
    )(pos_col, inv_lane)


def _memkv_kernel(mem_ref, g_ref, w_ref, o_ref):
    mn = _rms(mem_ref[...], g_ref[...]).astype(BF16)
    o_ref[...] = _dot(mn, w_ref[...]).astype(BF16)


def _memkv(mem2, g, w_all):
    rows = mem2.shape[0]
    ncol = w_all.shape[1]
    return pl.pallas_call(
        _memkv_kernel,
        grid=(rows // N_MEM,),
        in_specs=[pl.BlockSpec((N_MEM, D_MODEL), lambda b: (b, 0)),
                  _const_spec((1, D_MODEL)),
                  _const_spec((D_MODEL, ncol))],
        out_specs=pl.BlockSpec((N_MEM, ncol), lambda b: (b, 0)),
        out_shape=jax.ShapeDtypeStruct((rows, ncol), BF16),
        compiler_params=_params("parallel"),
        name="mem_kv",
    )(mem2, g, w_all)


def _mla_proj_kernel(x_ref, g_ref, win_ref, qg_ref, kvg_ref, wuq_ref, wukv_ref,
                     cos_ref, sin_ref, q_ref, k_ref, v_ref, qc_ref):
    hn = _rms(x_ref[...], g_ref[...]).astype(BF16)
    proj = _dot(hn, win_ref[...])
    c_q = proj[:, :MLA_Q_RANK]
    c_kv = proj[:, MLA_Q_RANK:MLA_Q_RANK + MLA_KV_RANK]
    k_r = proj[:, MLA_Q_RANK + MLA_KV_RANK:MLA_Q_RANK + MLA_KV_RANK + QK_PAD]
    qc_ref[...] = proj[:, MLA_Q_RANK + MLA_KV_RANK + QK_PAD:].astype(BF16)

    q = _dot(_rms(c_q, qg_ref[...]).astype(BF16), wuq_ref[...])
    kv = _dot(_rms(c_kv, kvg_ref[...]).astype(BF16), wukv_ref[...])

    lane = lax.broadcasted_iota(jnp.int32, (1, LANES), 1)
    second = lane >= MLA_NOPE + ROPE_HALF
    cos = cos_ref[...]
    sin = sin_ref[...]
    sin_up = jnp.where(second, sin, 0.0)
    sin_dn = jnp.where(second, 0.0, -sin)

    def rope(t):
        return (t * cos + pltpu.roll(t, ROPE_HALF, 1) * sin_up
                + pltpu.roll(t, LANES - ROPE_HALF, 1) * sin_dn)

    k_rope = rope(k_r)
    scale = MLA_QK ** -0.5
    for h in range(MLA_HEADS):
        sl = slice(h * QK_PAD, (h + 1) * QK_PAD)
        q_ref[:, sl] = (rope(q[:, sl]) * scale).astype(BF16)
        k_ref[:, sl] = (kv[:, sl] + k_rope).astype(BF16)
    v_ref[...] = kv[:, MLA_HEADS * QK_PAD:].astype(BF16)


def _mla_proj(x2, g, w_in, qg, kvg, w_uq, w_ukv, cos_t, sin_t):
    t = x2.shape[0]
    tm = ROW_TILE
    qw = MLA_HEADS * QK_PAD
    row = lambda w: pl.BlockSpec((tm, w), lambda i: (i, 0))
    return pl.pallas_call(
        _mla_proj_kernel,
        grid=(t // tm,),
        in_specs=[row(D_MODEL), _const_spec(g.shape), _const_spec(w_in.shape),
                  _const_spec(qg.shape), _const_spec(kvg.shape),
                  _const_spec(w_uq.shape), _const_spec(w_ukv.shape),
                  row(LANES), row(LANES)],
        out_specs=[row(qw), row(qw), row(MIX_WIDTH), row(CROSS_WIDTH)],
        out_shape=[jax.ShapeDtypeStruct((t, qw), BF16),
                   jax.ShapeDtypeStruct((t, qw), BF16),
                   jax.ShapeDtypeStruct((t, MIX_WIDTH), BF16),
                   jax.ShapeDtypeStruct((t, CROSS_WIDTH), BF16)],
        compiler_params=_params("parallel"),
        name="mla_proj",
    )(x2, g, w_in, qg, kvg, w_uq, w_ukv, cos_t, sin_t)


def _mla_attn_kernel(q_ref, k_ref, v_ref, o_ref, m_ref, l_ref, acc_ref, *, tq, tk):
    i = pl.program_id(2)
    lo = _low_half()
    m_ref[...] = jnp.full(m_ref.shape, NEG, F32)
    l_ref[...] = jnp.zeros(l_ref.shape, F32)
    acc_ref[...] = jnp.zeros(acc_ref.shape, F32)

    def step(j, masked):
        start = pl.multiple_of(j * tk, tk)
        vb = v_ref[pl.ds(start, tk), :]
        pv = []
        alpha = []
        for h in range(2):
            qh = q_ref[:, h * QK_PAD:(h + 1) * QK_PAD]
            kb = k_ref[pl.ds(start, tk), h * QK_PAD:(h + 1) * QK_PAD]
            s = _dot_nt(qh, kb)
            if masked:
                row = i * tq + lax.broadcasted_iota(jnp.int32, (tq, tk), 0)
                col = start + lax.broadcasted_iota(jnp.int32, (tq, tk), 1)
                s = jnp.where(col <= row, s, NEG)
            m_prev = m_ref[h]
            m_new = jnp.maximum(m_prev, jnp.max(s, axis=-1, keepdims=True))
            a = jnp.exp(m_prev - m_new)
            p = jnp.exp(s - m_new)
            l_ref[h] = a * l_ref[h] + jnp.sum(p, axis=-1, keepdims=True)
            m_ref[h] = m_new
            pv.append(_dot(p.astype(BF16), vb))
            alpha.append(a)
        acc_ref[...] = (jnp.where(lo, alpha[0], alpha[1]) * acc_ref[...]
                        + jnp.where(lo, pv[0], pv[1]))

    n_full = (i * tq) // tk

    def body(j, carry):
        step(j, False)
        return carry

    lax.fori_loop(0, n_full, body, 0)
    step(n_full, True)
    inv_l = jnp.where(lo, 1.0 / l_ref[0], 1.0 / l_ref[1])
    o_ref[...] = (acc_ref[...] * inv_l).astype(BF16)


def _mla_attn(q, k, v, batch, seq):
    tq, tk = ATTN_TQ, ATTN_TK
    nq = seq // tq
    pairs = MLA_HEADS // 2
    return pl.pallas_call(
        functools.partial(_mla_attn_kernel, tq=tq, tk=tk),
        grid=(batch, pairs, nq),
        in_specs=[pl.BlockSpec((tq, 2 * QK_PAD), lambda b, p, i: (b * nq + i, p)),
                  pl.BlockSpec((seq, 2 * QK_PAD), lambda b, p, i: (b, p)),
                  pl.BlockSpec((seq, LANES), lambda b, p, i: (b, p))],
        out_specs=pl.BlockSpec((tq, LANES), lambda b, p, i: (b * nq + i, p)),
        out_shape=jax.ShapeDtypeStruct((batch * seq, MIX_WIDTH), BF16),
        scratch_shapes=[pltpu.VMEM((2, tq, 1), F32), pltpu.VMEM((2, tq, 1), F32),
                        pltpu.VMEM((tq, LANES), F32)],
        compiler_params=_params("parallel", "parallel", "arbitrary"),
        name="mla_attn",
    )(q, k, v)


def _swa_proj_kernel(x_ref, g_ref, win_ref, q_ref, k_ref, v_ref, qc_ref):
    hn = _rms(x_ref[...], g_ref[...]).astype(BF16)
    proj = _dot(hn, win_ref[...])
    nq = SWA_Q_HEADS * HEAD_DIM
    nk = SWA_KV_HEADS * HEAD_DIM
    q_ref[...] = proj[:, :nq].astype(BF16)
    k_ref[...] = proj[:, nq:nq + nk].astype(BF16)
    v_ref[...] = proj[:, nq + nk:nq + 2 * nk].astype(BF16)
    qc_ref[...] = proj[:, nq + 2 * nk:].astype(BF16)


def _swa_proj(x2, g, w_in):
    t = x2.shape[0]
    tm = ROW_TILE
    nk = SWA_KV_HEADS * HEAD_DIM
    row = lambda w: pl.BlockSpec((tm, w), lambda i: (i, 0))
    return pl.pallas_call(
        _swa_proj_kernel,
        grid=(t // tm,),
        in_specs=[row(D_MODEL), _const_spec(g.shape), _const_spec(w_in.shape)],
        out_specs=[row(MIX_WIDTH), row(nk), row(nk), row(CROSS_WIDTH)],
        out_shape=[jax.ShapeDtypeStruct((t, MIX_WIDTH), BF16),
                   jax.ShapeDtypeStruct((t, nk), BF16),
                   jax.ShapeDtypeStruct((t, nk), BF16),
                   jax.ShapeDtypeStruct((t, CROSS_WIDTH), BF16)],
        compiler_params=_params("parallel"),
        name="swa_proj",
    )(x2, g, w_in)


def _swa_attn_kernel(slope_ref, sink_ref, q_ref, ko_ref, kp_ref, vo_ref, vp_ref,
                     pq_ref, pko_ref, pkp_ref, o_ref, *, nwin):
    i = pl.program_id(1)
    lo = _low_half()
    w_ = WINDOW
    kcat = jnp.concatenate([kp_ref[...], ko_ref[...]], axis=0)
    vcat = jnp.concatenate([vp_ref[...], vo_ref[...]], axis=0)
    qi = lax.broadcasted_iota(jnp.int32, (w_, 2 * w_), 0)
    kj = lax.broadcasted_iota(jnp.int32, (w_, 2 * w_), 1)
    rel = w_ + qi - kj
    band = (rel >= 0) & (rel < w_)
    first_band = band & ((kj >= w_) | (i > 0))

    for w in range(nwin):
        rows = slice(w * w_, (w + 1) * w_)
        pos_q = pq_ref[rows, :]
        pos_prev = pkp_ref[0] if w == 0 else pko_ref[w - 1]
        pos_k = jnp.concatenate([pos_prev, pko_ref[w]], axis=1)
        dist = (pos_q - pos_k).astype(F32)
        valid = first_band if w == 0 else band
        for pair in range(SWA_KV_HEADS // 2):
            ksl = slice(pair * LANES, (pair + 1) * LANES)
            kw = kcat[w * w_:(w + 2) * w_, ksl]
            vw = vcat[w * w_:(w + 2) * w_, ksl]
            outs = []
            for half in range(2):
                sel = lo if half == 0 else jnp.logical_not(lo)
                qs = []
                for g in range(SWA_GROUP):
                    tile = pair * SWA_GROUP + g
                    qt = q_ref[rows, tile * LANES:(tile + 1) * LANES]
                    qs.append(jnp.where(sel, qt, jnp.zeros_like(qt)))
                s3 = _dot_nt(jnp.concatenate(qs, axis=0), kw)
                ps = []
                dens = []
                for g in range(SWA_GROUP):
                    hq = (2 * pair + half) * SWA_GROUP + g
                    sink = sink_ref[hq]
                    s = s3[g * w_:(g + 1) * w_] - slope_ref[hq] * dist
                    s = jnp.where(valid, s, NEG)
                    m = jnp.maximum(jnp.max(s, axis=-1, keepdims=True), sink)
                    e = jnp.exp(s - m)
                    dens.append(jnp.sum(e, axis=-1, keepdims=True) + jnp.exp(sink - m))
                    ps.append(e.astype(BF16))
                o3 = _dot(jnp.concatenate(ps, axis=0), vw)
                outs.append([o3[g * w_:(g + 1) * w_] / dens[g] for g in range(SWA_GROUP)])
            for g in range(SWA_GROUP):
                tile = pair * SWA_GROUP + g
                o_ref[rows, tile * LANES:(tile + 1) * LANES] = jnp.where(
                    lo, outs[0][g], outs[1][g]).astype(BF16)


def _swa_attn(slopes, sinks, q, k, v, pos_col, pos_row, batch, seq):
    rows = SWA_ROWS
    nwin = rows // WINDOW
    nsteps = seq // rows
    nblk = seq // WINDOW
    nk = SWA_KV_HEADS * HEAD_DIM
    own = lambda b, i: (b * nsteps + i, 0)
    prev = lambda b, i: (b * nblk + jnp.maximum(i * nwin - 1, 0), 0)
    smem = pl.BlockSpec(memory_space=pltpu.SMEM)
    return pl.pallas_call(
        functools.partial(_swa_attn_kernel, nwin=nwin),
        grid=(batch, nsteps),
        in_specs=[smem, smem,
                  pl.BlockSpec((rows, MIX_WIDTH), own),
                  pl.BlockSpec((rows, nk), own),
                  pl.BlockSpec((WINDOW, nk), prev),
                  pl.BlockSpec((rows, nk), own),
                  pl.BlockSpec((WINDOW, nk), prev),
                  pl.BlockSpec((rows, 1), own),
                  pl.BlockSpec((nwin, 1, WINDOW), lambda b, i: (b * nsteps + i, 0, 0)),
                  pl.BlockSpec((1, 1, WINDOW),
                               lambda b, i: (b * nblk + jnp.maximum(i * nwin - 1, 0), 0, 0))],
        out_specs=pl.BlockSpec((rows, MIX_WIDTH), own),
        out_shape=jax.ShapeDtypeStruct((batch * seq, MIX_WIDTH), BF16),
        compiler_params=_params("parallel", "parallel"),
        name="swa_attn",
    )(slopes, sinks, q, k, k, v, v, pos_col, pos_row, pos_row)


def _out_mlp_kernel(*refs, final):
    if final:
        (x_ref, mix_ref, qc_ref, km_ref, vm_ref, wo_ref, g_ref, wup_ref, wdn_ref,
         gf_ref, o_ref, acc_ref) = refs
    else:
        (x_ref, mix_ref, qc_ref, km_ref, vm_ref, wo_ref, g_ref, wup_ref, wdn_ref,
         o_ref, acc_ref) = refs
    lo = _low_half()
    cross = []
    for pair in range(CROSS_WIDTH // LANES):
        sl = slice(pair * LANES, (pair + 1) * LANES)
        qp = qc_ref[:, sl]
        kp = km_ref[:, sl]
        vp = vm_ref[:, sl]
        outs = []
        for half in range(2):
            sel = lo if half == 0 else jnp.logical_not(lo)
            s = _dot_nt(jnp.where(sel, qp, jnp.zeros_like(qp)), kp)
            e = jnp.exp(s - jnp.max(s, axis=-1, keepdims=True))
            den = jnp.sum(e, axis=-1, keepdims=True)
            outs.append(_dot(e.astype(BF16), vp) / den)
        cross.append(jnp.where(lo, outs[0], outs[1]).astype(BF16))
    attn = jnp.concatenate([mix_ref[...]] + cross, axis=1)
    x1 = x_ref[...] + _dot(attn, wo_ref[...])
    hn = _rms(x1, g_ref[...]).astype(BF16)
    acc_ref[...] = x1
    for c in range(wup_ref.shape[0]):
        h = jnp.maximum(_dot(hn, wup_ref[c]), 0.0)
        acc_ref[...] += _dot((h * h).astype(BF16), wdn_ref[c])
    if final:
        o_ref[...] = _rms(acc_ref[...], gf_ref[...])
    else:
        o_ref[...] = acc_ref[...]


def _out_mlp(x2, mix, qc, memkv, layer, w_o, g, w_up, w_dn, g_final, seq):
    t = x2.shape[0]
    tm = ROW_TILE
    per_b = seq // tm
    final = g_final is not None
    row = lambda w: pl.BlockSpec((tm, w), lambda i: (i, 0))
    in_specs = [row(D_MODEL), row(MIX_WIDTH), row(CROSS_WIDTH),
                pl.BlockSpec((N_MEM, CROSS_WIDTH), lambda i: (i // per_b, 2 * layer)),
                pl.BlockSpec((N_MEM, CROSS_WIDTH), lambda i: (i // per_b, 2 * layer + 1)),
                _const_spec(w_o.shape), _const_spec(g.shape),
                _const_spec(w_up.shape), _const_spec(w_dn.shape)]
    args = [x2, mix, qc, memkv, memkv, w_o, g, w_up, w_dn]
    if final:
        in_specs.append(_const_spec(g_final.shape))
        args.append(g_final)
    return pl.pallas_call(
        functools.partial(_out_mlp_kernel, final=final),
        grid=(t // tm,),
        in_specs=in_specs,
        out_specs=row(D_MODEL),
        out_shape=jax.ShapeDtypeStruct((t, D_MODEL), F32),
        scratch_shapes=[pltpu.VMEM((tm, D_MODEL), F32)],
        compiler_params=_params("parallel"),
        name="out_mlp",
    )(*args)


def _prep_mla(w_in, w_uq, w_ukv):
    c_q = w_in[:, :MLA_Q_RANK]
    c_kv = w_in[:, MLA_Q_RANK:MLA_Q_RANK + MLA_KV_RANK]
    k_r = w_in[:, MLA_Q_RANK + MLA_KV_RANK:MLA_Q_RANK + MLA_KV_RANK + MLA_ROPE]
    q_c = w_in[:, MLA_Q_RANK + MLA_KV_RANK + MLA_ROPE:]
    k_r = jnp.pad(k_r, ((0, 0), (MLA_NOPE, QK_PAD - MLA_QK)))
    w_in_p = jnp.concatenate([c_q, c_kv, k_r, q_c], axis=1).astype(BF16)
    w_uq_p = jnp.pad(w_uq.reshape(MLA_Q_RANK, MLA_HEADS, MLA_QK),
                     ((0, 0), (0, 0), (0, QK_PAD - MLA_QK)))
    w_uq_p = w_uq_p.reshape(MLA_Q_RANK, MLA_HEADS * QK_PAD).astype(BF16)
    kv = w_ukv.reshape(MLA_KV_RANK, MLA_HEADS, 2 * HEAD_DIM)
    w_k = jnp.pad(kv[:, :, :MLA_NOPE], ((0, 0), (0, 0), (0, QK_PAD - MLA_NOPE)))
    w_k = w_k.reshape(MLA_KV_RANK, MLA_HEADS * QK_PAD)
    w_v = kv[:, :, MLA_NOPE:].reshape(MLA_KV_RANK, MLA_HEADS * HEAD_DIM)
    w_ukv_p = jnp.concatenate([w_k, w_v], axis=1).astype(BF16)
    return w_in_p, w_uq_p, w_ukv_p


def _pair_interleave(a, axis):
    shape = a.shape
    a = jnp.moveaxis(a, axis, 0)
    rest = a.shape[1:]
    a = a.reshape((SWA_KV_HEADS // 2, 2, SWA_GROUP, HEAD_DIM) + rest)
    a = a.transpose((0, 2, 1, 3) + tuple(range(4, 4 + len(rest))))
    a = a.reshape((MIX_WIDTH,) + rest)
    return jnp.moveaxis(a, 0, axis).reshape(shape)


def _prep_swa(w_in, w_o):
    nq = SWA_Q_HEADS * HEAD_DIM
    nk = SWA_KV_HEADS * HEAD_DIM
    w_q = _pair_interleave(w_in[:, :nq], 1)
    w_k = w_in[:, nq:nq + nk] * (HEAD_DIM ** -0.5)
    w_in_p = jnp.concatenate([w_q, w_k, w_in[:, nq + nk:]], axis=1).astype(BF16)
    w_o_p = jnp.concatenate([_pair_interleave(w_o[:MIX_WIDTH], 0), w_o[MIX_WIDTH:]], axis=0)
    return w_in_p, w_o_p.astype(BF16)


def kernel(x, mem, positions, attn_norm_g, mlp_norm_g, mem_norm_g, final_norm_g,
           mla_w_in, mla_q_norm_g, mla_kv_norm_g, mla_w_uq, mla_w_ukv,
           swa_w_in, swa_sinks, w_mem_kv, w_o, mlp_w_up, mlp_w_down):
    batch, seq, d = x.shape
    depth = attn_norm_g.shape[0]
    t = batch * seq
    x2 = x.reshape(t, d)

    w_mk = w_mem_kv[:, :, :CROSS_WIDTH] * (HEAD_DIM ** -0.5)
    w_mem_all = jnp.concatenate([w_mk, w_mem_kv[:, :, CROSS_WIDTH:]], axis=2)
    w_mem_all = w_mem_all.transpose(1, 0, 2).reshape(d, depth * 2 * CROSS_WIDTH).astype(BF16)
    memkv = _memkv(mem.reshape(batch * N_MEM, d), mem_norm_g.reshape(1, d), w_mem_all)

    inv = ROPE_THETA ** (-(jnp.arange(ROPE_HALF, dtype=F32) * 2.0) / MLA_ROPE)
    inv_lane = jnp.zeros((LANES,), F32).at[MLA_NOPE:MLA_QK].set(jnp.concatenate([inv, inv]))
    pos_col_f = positions.astype(F32).reshape(t, 1)
    cos_t, sin_t = _rope_tables(pos_col_f, inv_lane.reshape(1, LANES))

    pos_col_i = positions.reshape(t, 1)
    pos_row_i = positions.reshape(t // WINDOW, 1, WINDOW)
    slopes = 2.0 ** (-8.0 * (jnp.arange(SWA_Q_HEADS, dtype=F32) + 1.0) / SWA_Q_HEADS)

    nchunk = D_FF // FF_CHUNK
    for i in range(depth):
        j = i // 2
        g_attn = attn_norm_g[i].reshape(1, d)
        if i % 2 == 0:
            w_in_p, w_uq_p, w_ukv_p = _prep_mla(mla_w_in[j], mla_w_uq[j], mla_w_ukv[j])
            q, k, v, qc = _mla_proj(x2, g_attn, w_in_p,
                                    mla_q_norm_g[j].reshape(1, MLA_Q_RANK),
                                    mla_kv_norm_g[j].reshape(1, MLA_KV_RANK),
                                    w_uq_p, w_ukv_p, cos_t, sin_t)
            mix = _mla_attn(q, k, v, batch, seq)
            w_o_p = w_o[i].astype(BF16)
        else:
            w_in_p, w_o_p = _prep_swa(swa_w_in[j], w_o[i])
            q, k, v, qc = _swa_proj(x2, g_attn, w_in_p)
            mix = _swa_attn(slopes, swa_sinks[j].astype(F32), q, k, v,
                            pos_col_i, pos_row_i, batch, seq)
        w_up = mlp_w_up[i].reshape(d, nchunk, FF_CHUNK).transpose(1, 0, 2).astype(BF16)
        w_dn = mlp_w_down[i].reshape(nchunk, FF_CHUNK, d).astype(BF16)
        g_final = final_norm_g.reshape(1, d) if i == depth - 1 else None
        x2 = _out_mlp(x2, mix, qc, memkv, i, w_o_p, mlp_norm_g[i].reshape(1, d),
                      w_up, w_dn, g_final, seq)
    return x2.reshape(batch, seq, d)
```

```python
import functools
import math

import jax
import jax.numpy as jnp
from jax import lax
from jax.experimental import pallas as pl
from jax.experimental.pallas import tpu as pltpu

F32 = jnp.float32
BF16 = jnp.bfloat16

D_MODEL = 1024
HEAD_DIM = 64
N_MEM = 256
CROSS_WIDTH = 256
MIX_WIDTH = 768
EPS = 1e-6
NEG = -1e30
MLA_HEADS = 12
MLA_Q_RANK = 384
MLA_KV_RANK = 256
MLA_NOPE = 64
MLA_ROPE = 32
MLA_QK = MLA_NOPE + MLA_ROPE
ROPE_THETA = 10000.0
SWA_Q_HEADS = 12
SWA_KV_HEADS = 4
SWA_GROUP = 3
WINDOW = 128
D_FF = 4 * D_MODEL

LANES = 128
QK_PAD = LANES
ROPE_HALF = MLA_ROPE // 2

ROW_TILE = 512
ATTN_TQ = 512
ATTN_TK = ROW_TILE
SWA_ROWS = 512
FF_CHUNK = 1024
VMEM_LIMIT = 48 * 1024 * 1024


def _rms(x, g):
    ms = jnp.mean(x * x, axis=-1, keepdims=True)
    return x * lax.rsqrt(ms + EPS) * g


def _dot(a, b):
    return jnp.dot(a, b, preferred_element_type=F32)


def _dot_nt(a, b):
    return lax.dot_general(a, b, (((1,), (1,)), ((), ())), preferred_element_type=F32)


def _low_half():
    return lax.broadcasted_iota(jnp.int32, (1, LANES), 1) < HEAD_DIM


def _const_spec(shape):
    nd = len(shape)
    return pl.BlockSpec(shape, lambda *_: (0,) * nd, pipeline_mode=pl.Buffered(1))


def _params(*sem):
    return pltpu.CompilerParams(dimension_semantics=sem, vmem_limit_bytes=VMEM_LIMIT)


def _rope_table_kernel(pos_ref, inv_ref, cos_ref, sin_ref):
    ang = inv_ref[...] * pos_ref[...]
    cos_ref[...] = jnp.cos(ang)
    sin_ref[...] = jnp.sin(ang)


def _rope_tables(pos_row, inv_col):
    t = pos_row.shape[1]
    tn = 2048
    return pl.pallas_call(
        _rope_table_kernel,
        grid=(t // tn,),
        in_specs=[pl.BlockSpec((1, tn), lambda i: (0, i)),
                  pl.BlockSpec((ROPE_HALF, 1), lambda i: (0, 0))],
        out_specs=[pl.BlockSpec((ROPE_HALF, tn), lambda i: (0, i))] * 2,
        out_shape=[jax.ShapeDtypeStruct((ROPE_HALF, t), F32)] * 2,
        compiler_params=_params("parallel"),
        name="rope_tables",
    )(pos_row, inv_col)


def _memkv_kernel(mem_ref, g_ref, w_ref, o_ref):
    mn = _rms(mem_ref[...], g_ref[...]).astype(BF16)
    o_ref[...] = _dot(mn, w_ref[...]).astype(BF16)


def _memkv(mem2, g, w_all):
    rows = mem2.shape[0]
    ncol = w_all.shape[1]
    return pl.pallas_call(
        _memkv_kernel,
        grid=(rows // N_MEM,),
        in_specs=[pl.BlockSpec((N_MEM, D_MODEL), lambda b: (b, 0)),
                  _const_spec((1, D_MODEL)),
                  _const_spec((D_MODEL, ncol))],
        out_specs=pl.BlockSpec((N_MEM, ncol), lambda b: (b, 0)),
        out_shape=jax.ShapeDtypeStruct((rows, ncol), BF16),
        compiler_params=_params("parallel"),
        name="mem_kv",
    )(mem2, g, w_all)


def _mla_proj_kernel(x_ref, g_ref, wa_ref, wkr_ref, qg_ref, kvg_ref, wuq_ref, wuk_ref,
                     wuv_ref, cos_ref, sin_ref, q_ref, k_ref, v_ref, qc_ref):
    tm = x_ref.shape[0]
    hn = _rms(x_ref[...], g_ref[...]).astype(BF16)
    proj = _dot(hn, wa_ref[...])
    c_q = proj[:, :MLA_Q_RANK]
    c_kv = proj[:, MLA_Q_RANK:MLA_Q_RANK + MLA_KV_RANK]
    qc_ref[...] = proj[:, MLA_Q_RANK + MLA_KV_RANK:].astype(BF16)
    cqn = _rms(c_q, qg_ref[...]).astype(BF16)
    ckvn = _rms(c_kv, kvg_ref[...]).astype(BF16)

    cos = cos_ref[...]
    sin = sin_ref[...]

    def rope(x1, x2):
        return x1 * cos - x2 * sin, x1 * sin + x2 * cos

    kr_t = _dot_nt(wkr_ref[...], hn)
    r1, r2 = rope(kr_t[:ROPE_HALF], kr_t[ROPE_HALF:])
    k_rope = jnp.concatenate(
        [jnp.zeros((MLA_NOPE, tm), F32), r1, r2,
         jnp.zeros((QK_PAD - MLA_QK, tm), F32)], axis=0).T

    k_nope = _dot(ckvn, wuk_ref[...])
    for h in range(MLA_HEADS):
        sl = slice(h * QK_PAD, (h + 1) * QK_PAD)
        k_ref[:, sl] = (k_nope[:, sl] + k_rope).astype(BF16)

    v_ref[0] = _dot_nt(wuv_ref[...], ckvn).astype(BF16)

    q_t = _dot_nt(wuq_ref[...], cqn)
    scale = MLA_QK ** -0.5 * math.log2(math.e)
    pad = jnp.zeros((QK_PAD - MLA_QK, tm), BF16)
    for h in range(MLA_HEADS):
        base = h * QK_PAD
        x1 = q_t[base + MLA_NOPE:base + MLA_NOPE + ROPE_HALF]
        x2 = q_t[base + MLA_NOPE + ROPE_HALF:base + MLA_QK]
        r1, r2 = rope(x1, x2)
        q_ref[base:base + MLA_NOPE, :] = (q_t[base:base + MLA_NOPE] * scale).astype(BF16)
        q_ref[base + MLA_NOPE:base + MLA_NOPE + ROPE_HALF, :] = (r1 * scale).astype(BF16)
        q_ref[base + MLA_NOPE + ROPE_HALF:base + MLA_QK, :] = (r2 * scale).astype(BF16)
        q_ref[base + MLA_QK:base + QK_PAD, :] = pad


def _mla_proj(x2, g, w_a, w_kr, qg, kvg, w_uq, w_uk, w_uv, cos_t, sin_t):
    t = x2.shape[0]
    tm = ROW_TILE
    qw = MLA_HEADS * QK_PAD
    row = lambda w: pl.BlockSpec((tm, w), lambda i: (i, 0))
    col = lambda r: pl.BlockSpec((r, tm), lambda i: (0, i))
    consts = [g, w_a, w_kr, qg, kvg, w_uq, w_uk, w_uv]
    return pl.pallas_call(
        _mla_proj_kernel,
        grid=(t // tm,),
        in_specs=[row(D_MODEL)] + [_const_spec(c.shape) for c in consts]
                 + [col(ROPE_HALF), col(ROPE_HALF)],
        out_specs=[col(qw), row(qw),
                   pl.BlockSpec((1, MIX_WIDTH, tm), lambda i: (i, 0, 0)),
                   row(CROSS_WIDTH)],
        out_shape=[jax.ShapeDtypeStruct((qw, t), BF16),
                   jax.ShapeDtypeStruct((t, qw), BF16),
                   jax.ShapeDtypeStruct((t // tm, MIX_WIDTH, tm), BF16),
                   jax.ShapeDtypeStruct((t, CROSS_WIDTH), BF16)],
        compiler_params=_params("parallel"),
        name="mla_proj",
    )(x2, *consts, cos_t, sin_t)


def _mla_attn_kernel(q_ref, k_ref, v_ref, o_ref, s_ref, mb_ref, m_ref, l_ref, acc_ref,
                     *, tq, tk):
    i = pl.program_id(2)
    m_ref[...] = jnp.full(m_ref.shape, NEG, F32)
    l_ref[...] = jnp.zeros(l_ref.shape, F32)
    acc_ref[...] = jnp.zeros(acc_ref.shape, F32)

    def scores(j, slot):
        start = pl.multiple_of(j * tk, tk)
        for h in range(2):
            kb = k_ref[pl.ds(start, tk), h * QK_PAD:(h + 1) * QK_PAD]
            s = _dot(kb, q_ref[h * QK_PAD:(h + 1) * QK_PAD, :])
            s_ref[slot, h] = s
            mb_ref[slot, h] = jnp.max(s, axis=0, keepdims=True)

    def consume(j, slot, masked):
        for h in range(2):
            s = s_ref[slot, h]
            if masked:
                key = j * tk + lax.broadcasted_iota(jnp.int32, (tk, tq), 0)
                qry = i * tq + lax.broadcasted_iota(jnp.int32, (tk, tq), 1)
                s = jnp.where(key <= qry, s, NEG)
                m_blk = jnp.max(s, axis=0, keepdims=True)
            else:
                m_blk = mb_ref[slot, h]
            m_prev = m_ref[h]
            m_new = jnp.maximum(m_prev, m_blk)
            a = jnp.exp2(m_prev - m_new)
            p = jnp.exp2(s - m_new)
            l_ref[h] = a * l_ref[h] + jnp.sum(p, axis=0, keepdims=True)
            m_ref[h] = m_new
            vb = v_ref[j, h * HEAD_DIM:(h + 1) * HEAD_DIM, :]
            rows = slice(h * HEAD_DIM, (h + 1) * HEAD_DIM)
            acc_ref[rows, :] = a * acc_ref[rows, :] + _dot(vb, p.astype(BF16))

    n_full = (i * tq) // tk
    scores(0, 0)

    def body(jj, carry):
        j = 2 * jj
        scores(j + 1, 1)
        consume(j, 0, False)
        scores(j + 2, 0)
        consume(j + 1, 1, False)
        return carry

    lax.fori_loop(0, n_full // 2, body, 0)

    @pl.when(lax.rem(n_full, 2) == 0)
    def _():
        consume(n_full, 0, True)

    @pl.when(lax.rem(n_full, 2) == 1)
    def _():
        scores(n_full, 1)
        consume(n_full - 1, 0, False)
        consume(n_full, 1, True)

    for h in range(2):
        rows = slice(h * HEAD_DIM, (h + 1) * HEAD_DIM)
        o_ref[rows, :] = (acc_ref[rows, :] / l_ref[h]).astype(BF16)


def _mla_attn(q_t, k, v_t, batch, seq):
    tq, tk = ATTN_TQ, ATTN_TK
    nq = seq // tq
    nk = seq // tk
    pairs = MLA_HEADS // 2
    return pl.pallas_call(
        functools.partial(_mla_attn_kernel, tq=tq, tk=tk),
        grid=(batch, pairs, nq),
        in_specs=[pl.BlockSpec((2 * QK_PAD, tq), lambda b, p, i: (p, b * nq + i)),
                  pl.BlockSpec((seq, 2 * QK_PAD), lambda b, p, i: (b, p)),
                  pl.BlockSpec((nk, LANES, tk), lambda b, p, i: (b, p, 0))],
        out_specs=pl.BlockSpec((LANES, tq), lambda b, p, i: (p, b * nq + i)),
        out_shape=jax.ShapeDtypeStruct((MIX_WIDTH, batch * seq), BF16),
        scratch_shapes=[pltpu.VMEM((2, 2, tk, tq), F32), pltpu.VMEM((2, 2, 1, tq), F32),
                        pltpu.VMEM((2, 1, tq), F32), pltpu.VMEM((2, 1, tq), F32),
                        pltpu.VMEM((LANES, tq), F32)],
        compiler_params=_params("parallel", "parallel", "arbitrary"),
        name="mla_attn",
    )(q_t, k, v_t)


def _swa_proj_kernel(x_ref, g_ref, win_ref, q_ref, k_ref, v_ref, qc_ref):
    hn = _rms(x_ref[...], g_ref[...]).astype(BF16)
    proj = _dot(hn, win_ref[...])
    nq = SWA_Q_HEADS * HEAD_DIM
    nk = SWA_KV_HEADS * HEAD_DIM
    q_ref[...] = proj[:, :nq].astype(BF16)
    k_ref[...] = proj[:, nq:nq + nk].astype(BF16)
    v_ref[...] = proj[:, nq + nk:nq + 2 * nk].astype(BF16)
    qc_ref[...] = proj[:, nq + 2 * nk:].astype(BF16)


def _swa_proj(x2, g, w_in):
    t = x2.shape[0]
    tm = ROW_TILE
    nk = SWA_KV_HEADS * HEAD_DIM
    row = lambda w: pl.BlockSpec((tm, w), lambda i: (i, 0))
    return pl.pallas_call(
        _swa_proj_kernel,
        grid=(t // tm,),
        in_specs=[row(D_MODEL), _const_spec(g.shape), _const_spec(w_in.shape)],
        out_specs=[row(MIX_WIDTH), row(nk), row(nk), row(CROSS_WIDTH)],
        out_shape=[jax.ShapeDtypeStruct((t, MIX_WIDTH), BF16),
                   jax.ShapeDtypeStruct((t, nk), BF16),
                   jax.ShapeDtypeStruct((t, nk), BF16),
                   jax.ShapeDtypeStruct((t, CROSS_WIDTH), BF16)],
        compiler_params=_params("parallel"),
        name="swa_proj",
    )(x2, g, w_in)


def _swa_attn_kernel(slope_ref, sink_ref, q_ref, ko_ref, kp_ref, vo_ref, vp_ref,
                     pq_ref, pko_ref, pkp_ref, o_ref, *, nwin):
    i = pl.program_id(1)
    lo = _low_half()
    w_ = WINDOW
    kcat = jnp.concatenate([kp_ref[...], ko_ref[...]], axis=0)
    vcat = jnp.concatenate([vp_ref[...], vo_ref[...]], axis=0)
    qi = lax.broadcasted_iota(jnp.int32, (w_, 2 * w_), 0)
    kj = lax.broadcasted_iota(jnp.int32, (w_, 2 * w_), 1)
    rel = w_ + qi - kj
    band = (rel >= 0) & (rel < w_)
    first_band = band & ((kj >= w_) | (i > 0))

    for w in range(nwin):
        rows = slice(w * w_, (w + 1) * w_)
        pos_q = pq_ref[rows, :]
        pos_prev = pkp_ref[0] if w == 0 else pko_ref[w - 1]
        pos_k = jnp.concatenate([pos_prev, pko_ref[w]], axis=1)
        dist = (pos_q - pos_k).astype(F32)
        valid = first_band if w == 0 else band
        for pair in range(SWA_KV_HEADS // 2):
            ksl = slice(pair * LANES, (pair + 1) * LANES)
            kw = kcat[w * w_:(w + 2) * w_, ksl]
            vw = vcat[w * w_:(w + 2) * w_, ksl]
            outs = []
            for half in range(2):
                sel = lo if half == 0 else jnp.logical_not(lo)
                qs = []
                for g in range(SWA_GROUP):
                    tile = pair * SWA_GROUP + g
                    qt = q_ref[rows, tile * LANES:(tile + 1) * LANES]
                    qs.append(jnp.where(sel, qt, jnp.zeros_like(qt)))
                s3 = _dot_nt(jnp.concatenate(qs, axis=0), kw)
                ps = []
                dens = []
                for g in range(SWA_GROUP):
                    hq = (2 * pair + half) * SWA_GROUP + g
                    sink = sink_ref[hq]
                    s = s3[g * w_:(g + 1) * w_] - slope_ref[hq] * dist
                    s = jnp.where(valid, s, NEG)
                    m = jnp.maximum(jnp.max(s, axis=-1, keepdims=True), sink)
                    e = jnp.exp(s - m)
                    dens.append(jnp.sum(e, axis=-1, keepdims=True) + jnp.exp(sink - m))
                    ps.append(e.astype(BF16))
                o3 = _dot(jnp.concatenate(ps, axis=0), vw)
                outs.append([o3[g * w_:(g + 1) * w_] / dens[g] for g in range(SWA_GROUP)])
            for g in range(SWA_GROUP):
                tile = pair * SWA_GROUP + g
                o_ref[rows, tile * LANES:(tile + 1) * LANES] = jnp.where(
                    lo, outs[0][g], outs[1][g]).astype(BF16)


def _swa_attn(slopes, sinks, q, k, v, pos_col, pos_row, batch, seq):
    rows = SWA_ROWS
    nwin = rows // WINDOW
    nsteps = seq // rows
    nblk = seq // WINDOW
    nk = SWA_KV_HEADS * HEAD_DIM
    own = lambda b, i: (b * nsteps + i, 0)
    prev = lambda b, i: (b * nblk + jnp.maximum(i * nwin - 1, 0), 0)
    smem = pl.BlockSpec(memory_space=pltpu.SMEM)
    return pl.pallas_call(
        functools.partial(_swa_attn_kernel, nwin=nwin),
        grid=(batch, nsteps),
        in_specs=[smem, smem,
                  pl.BlockSpec((rows, MIX_WIDTH), own),
                  pl.BlockSpec((rows, nk), own),
                  pl.BlockSpec((WINDOW, nk), prev),
                  pl.BlockSpec((rows, nk), own),
                  pl.BlockSpec((WINDOW, nk), prev),
                  pl.BlockSpec((rows, 1), own),
                  pl.BlockSpec((nwin, 1, WINDOW), lambda b, i: (b * nsteps + i, 0, 0)),
                  pl.BlockSpec((1, 1, WINDOW),
                               lambda b, i: (b * nblk + jnp.maximum(i * nwin - 1, 0), 0, 0))],
        out_specs=pl.BlockSpec((rows, MIX_WIDTH), own),
        out_shape=jax.ShapeDtypeStruct((batch * seq, MIX_WIDTH), BF16),
        compiler_params=_params("parallel", "parallel"),
        name="swa_attn",
    )(slopes, sinks, q, k, k, v, v, pos_col, pos_row, pos_row)


def _out_mlp_kernel(*refs, final, mix_transposed):
    if final:
        (x_ref, mix_ref, qc_ref, km_ref, vm_ref, wo_ref, g_ref, wup_ref, wdn_ref,
         gf_ref, o_ref, acc_ref) = refs
    else:
        (x_ref, mix_ref, qc_ref, km_ref, vm_ref, wo_ref, g_ref, wup_ref, wdn_ref,
         o_ref, acc_ref) = refs
    lo = _low_half()
    cross = []
    for pair in range(CROSS_WIDTH // LANES):
        sl = slice(pair * LANES, (pair + 1) * LANES)
        qp = qc_ref[:, sl]
        kp = km_ref[:, sl]
        vp = vm_ref[:, sl]
        outs = []
        for half in range(2):
            sel = lo if half == 0 else jnp.logical_not(lo)
            s = _dot_nt(jnp.where(sel, qp, jnp.zeros_like(qp)), kp)
            e = jnp.exp(s - jnp.max(s, axis=-1, keepdims=True))
            den = jnp.sum(e, axis=-1, keepdims=True)
            outs.append(_dot(e.astype(BF16), vp) / den)
        cross.append(jnp.where(lo, outs[0], outs[1]).astype(BF16))
    if mix_transposed:
        mix = mix_ref[...].astype(F32).T.astype(BF16)
    else:
        mix = mix_ref[...]
    attn = jnp.concatenate([mix] + cross, axis=1)
    x1 = x_ref[...] + _dot(attn, wo_ref[...])
    hn = _rms(x1, g_ref[...]).astype(BF16)
    acc_ref[...] = x1
    for c in range(wup_ref.shape[0]):
        h = jnp.maximum(_dot(hn, wup_ref[c]), 0.0)
        acc_ref[...] += _dot((h * h).astype(BF16), wdn_ref[c])
    if final:
        o_ref[...] = _rms(acc_ref[...], gf_ref[...])
    else:
        o_ref[...] = acc_ref[...]


def _out_mlp(x2, mix, qc, memkv, layer, w_o, g, w_up, w_dn, g_final, seq, mix_transposed):
    t = x2.shape[0]
    tm = ROW_TILE
    per_b = seq // tm
    final = g_final is not None
    row = lambda w: pl.BlockSpec((tm, w), lambda i: (i, 0))
    mix_spec = (pl.BlockSpec((MIX_WIDTH, tm), lambda i: (0, i)) if mix_transposed
                else row(MIX_WIDTH))
    in_specs = [row(D_MODEL), mix_spec, row(CROSS_WIDTH),
                pl.BlockSpec((N_MEM, CROSS_WIDTH), lambda i: (i // per_b, 2 * layer)),
                pl.BlockSpec((N_MEM, CROSS_WIDTH), lambda i: (i // per_b, 2 * layer + 1)),
                _const_spec(w_o.shape), _const_spec(g.shape),
                _const_spec(w_up.shape), _const_spec(w_dn.shape)]
    args = [x2, mix, qc, memkv, memkv, w_o, g, w_up, w_dn]
    if final:
        in_specs.append(_const_spec(g_final.shape))
        args.append(g_final)
    return pl.pallas_call(
        functools.partial(_out_mlp_kernel, final=final, mix_transposed=mix_transposed),
        grid=(t // tm,),
        in_specs=in_specs,
        out_specs=row(D_MODEL),
        out_shape=jax.ShapeDtypeStruct((t, D_MODEL), F32),
        scratch_shapes=[pltpu.VMEM((tm, D_MODEL), F32)],
        compiler_params=_params("parallel"),
        name="out_mlp",
    )(*args)


def _prep_mla(w_in, w_uq, w_ukv):
    n1 = MLA_Q_RANK + MLA_KV_RANK
    w_a = jnp.concatenate([w_in[:, :n1], w_in[:, n1 + MLA_ROPE:]], axis=1).astype(BF16)
    w_kr_t = w_in[:, n1:n1 + MLA_ROPE].T.astype(BF16)
    w_uq_t = jnp.pad(w_uq.reshape(MLA_Q_RANK, MLA_HEADS, MLA_QK),
                     ((0, 0), (0, 0), (0, QK_PAD - MLA_QK)))
    w_uq_t = w_uq_t.reshape(MLA_Q_RANK, MLA_HEADS * QK_PAD).T.astype(BF16)
    kv = w_ukv.reshape(MLA_KV_RANK, MLA_HEADS, 2 * HEAD_DIM)
    w_uk = jnp.pad(kv[:, :, :MLA_NOPE], ((0, 0), (0, 0), (0, QK_PAD - MLA_NOPE)))
    w_uk = w_uk.reshape(MLA_KV_RANK, MLA_HEADS * QK_PAD).astype(BF16)
    w_uv_t = kv[:, :, MLA_NOPE:].reshape(MLA_KV_RANK, MIX_WIDTH).T.astype(BF16)
    return w_a, w_kr_t, w_uq_t, w_uk, w_uv_t


def _pair_interleave(a, axis):
    shape = a.shape
    a = jnp.moveaxis(a, axis, 0)
    rest = a.shape[1:]
    a = a.reshape((SWA_KV_HEADS // 2, 2, SWA_GROUP, HEAD_DIM) + rest)
    a = a.transpose((0, 2, 1, 3) + tuple(range(4, 4 + len(rest))))
    a = a.reshape((MIX_WIDTH,) + rest)
    return jnp.moveaxis(a, 0, axis).reshape(shape)


def _prep_swa(w_in, w_o):
    nq = SWA_Q_HEADS * HEAD_DIM
    nk = SWA_KV_HEADS * HEAD_DIM
    w_q = _pair_interleave(w_in[:, :nq], 1)
    w_k = w_in[:, nq:nq + nk] * (HEAD_DIM ** -0.5)
    w_in_p = jnp.concatenate([w_q, w_k, w_in[:, nq + nk:]], axis=1).astype(BF16)
    w_o_p = jnp.concatenate([_pair_interleave(w_o[:MIX_WIDTH], 0), w_o[MIX_WIDTH:]], axis=0)
    return w_in_p, w_o_p.astype(BF16)


def kernel(x, mem, positions, attn_norm_g, mlp_norm_g, mem_norm_g, final_norm_g,
           mla_w_in, mla_q_norm_g, mla_kv_norm_g, mla_w_uq, mla_w_ukv,
           swa_w_in, swa_sinks, w_mem_kv, w_o, mlp_w_up, mlp_w_down):
    batch, seq, d = x.shape
    depth = attn_norm_g.shape[0]
    t = batch * seq
    x2 = x.reshape(t, d)

    w_mk = w_mem_kv[:, :, :CROSS_WIDTH] * (HEAD_DIM ** -0.5)
    w_mem_all = jnp.concatenate([w_mk, w_mem_kv[:, :, CROSS_WIDTH:]], axis=2)
    w_mem_all = w_mem_all.transpose(1, 0, 2).reshape(d, depth * 2 * CROSS_WIDTH).astype(BF16)
    memkv = _memkv(mem.reshape(batch * N_MEM, d), mem_norm_g.reshape(1, d), w_mem_all)

    inv = ROPE_THETA ** (-(jnp.arange(ROPE_HALF, dtype=F32) * 2.0) / MLA_ROPE)
    cos_t, sin_t = _rope_tables(positions.astype(F32).reshape(1, t), inv.reshape(ROPE_HALF, 1))

    pos_col_i = positions.reshape(t, 1)
    pos_row_i = positions.reshape(t // WINDOW, 1, WINDOW)
    slopes = 2.0 ** (-8.0 * (jnp.arange(SWA_Q_HEADS, dtype=F32) + 1.0) / SWA_Q_HEADS)

    nchunk = D_FF // FF_CHUNK
    for i in range(depth):
        j = i // 2
        g_attn = attn_norm_g[i].reshape(1, d)
        is_mla = i % 2 == 0
        if is_mla:
            w_a, w_kr_t, w_uq_t, w_uk, w_uv_t = _prep_mla(mla_w_in[j], mla_w_uq[j], mla_w_ukv[j])
            q_t, k, v_t, qc = _mla_proj(x2, g_attn, w_a, w_kr_t,
                                        mla_q_norm_g[j].reshape(1, MLA_Q_RANK),
                                        mla_kv_norm_g[j].reshape(1, MLA_KV_RANK),
                                        w_uq_t, w_uk, w_uv_t, cos_t, sin_t)
            mix = _mla_attn(q_t, k, v_t, batch, seq)
            w_o_p = w_o[i].astype(BF16)
        else:
            w_in_p, w_o_p = _prep_swa(swa_w_in[j], w_o[i])
            q, k, v, qc = _swa_proj(x2, g_attn, w_in_p)
            mix = _swa_attn(slopes, swa_sinks[j].astype(F32), q, k, v,
                            pos_col_i, pos_row_i, batch, seq)
        w_up = mlp_w_up[i].reshape(d, nchunk, FF_CHUNK).transpose(1, 0, 2).astype(BF16)
        w_dn = mlp_w_down[i].reshape(nchunk, FF_CHUNK, d).astype(BF16)
        g_final = final_norm_g.reshape(1, d) if i == depth - 1 else None
        x2 = _out_mlp(x2, mix, qc, memkv, i, w_o_p, mlp_norm_g[i].reshape(1, d),
                      w_up, w_dn, g_final, seq, is_mla)
    return x2.reshape(batch, seq, d)
```

```python
import functools
import math

import jax
import jax.numpy as jnp
from jax import lax
from jax.experimental import pallas as pl
from jax.experimental.pallas import tpu as pltpu

F32 = jnp.float32
BF16 = jnp.bfloat16

D_MODEL = 1024
HEAD_DIM = 64
N_MEM = 256
CROSS_WIDTH = 256
MIX_WIDTH = 768
EPS = 1e-6
NEG = -1e30
MLA_HEADS = 12
MLA_Q_RANK = 384
MLA_KV_RANK = 256
MLA_NOPE = 64
MLA_ROPE = 32
MLA_QK = MLA_NOPE + MLA_ROPE
ROPE_THETA = 10000.0
SWA_Q_HEADS = 12
SWA_KV_HEADS = 4
SWA_GROUP = 3
WINDOW = 128
D_FF = 4 * D_MODEL

LANES = 128
QK_PAD = LANES
ROPE_HALF = MLA_ROPE // 2
BF16_ROWS = 16
V_ROWS = HEAD_DIM + BF16_ROWS

ROW_TILE = 512
ATTN_TILE = ROW_TILE
SWA_ROWS = 512
FF_CHUNK = 1024
VMEM_LIMIT = 48 * 1024 * 1024


def _rms(x, g):
    ms = jnp.mean(x * x, axis=-1, keepdims=True)
    return x * lax.rsqrt(ms + EPS) * g


def _dot(a, b):
    return jnp.dot(a, b, preferred_element_type=F32)


def _dot_nt(a, b):
    return lax.dot_general(a, b, (((1,), (1,)), ((), ())), preferred_element_type=F32)


def _low_half():
    return lax.broadcasted_iota(jnp.int32, (1, LANES), 1) < HEAD_DIM


def _const_spec(shape):
    nd = len(shape)
    return pl.BlockSpec(shape, lambda *_: (0,) * nd, pipeline_mode=pl.Buffered(1))


def _layer_spec(stacked, layer):
    nd = stacked.ndim
    return pl.BlockSpec((None,) + stacked.shape[1:], lambda *_: (layer,) + (0,) * (nd - 1),
                        pipeline_mode=pl.Buffered(1))


def _params(*sem):
    return pltpu.CompilerParams(dimension_semantics=sem, vmem_limit_bytes=VMEM_LIMIT)


def _rope_table_kernel(pos_ref, inv_ref, cos_ref, sin_ref):
    ang = inv_ref[...] * pos_ref[...]
    cos_ref[...] = jnp.cos(ang)
    sin_ref[...] = jnp.sin(ang)


def _rope_tables(pos_row, inv_col):
    t = pos_row.shape[1]
    tn = 2048
    return pl.pallas_call(
        _rope_table_kernel,
        grid=(t // tn,),
        in_specs=[pl.BlockSpec((1, tn), lambda i: (0, i)),
                  pl.BlockSpec((ROPE_HALF, 1), lambda i: (0, 0))],
        out_specs=[pl.BlockSpec((ROPE_HALF, tn), lambda i: (0, i))] * 2,
        out_shape=[jax.ShapeDtypeStruct((ROPE_HALF, t), F32)] * 2,
        compiler_params=_params("parallel"),
        name="rope_tables",
    )(pos_row, inv_col)


def _memkv_kernel(mem_ref, g_ref, w_ref, o_ref):
    mn = _rms(mem_ref[...], g_ref[...]).astype(BF16)
    o_ref[...] = _dot(mn, w_ref[...]).astype(BF16)


def _memkv(mem2, g, w_all):
    rows = mem2.shape[0]
    depth, _, ncol = w_all.shape
    return pl.pallas_call(
        _memkv_kernel,
        grid=(rows // N_MEM, depth),
        in_specs=[pl.BlockSpec((N_MEM, D_MODEL), lambda b, l: (b, 0)),
                  _const_spec((1, D_MODEL)),
                  pl.BlockSpec((None, D_MODEL, ncol), lambda b, l: (l, 0, 0))],
        out_specs=pl.BlockSpec((N_MEM, ncol), lambda b, l: (b, l)),
        out_shape=jax.ShapeDtypeStruct((rows, depth * ncol), BF16),
        compiler_params=_params("parallel", "parallel"),
        name="mem_kv",
    )(mem2, g, w_all)


def _mla_proj_kernel(x_ref, g_ref, wa_ref, wkr_ref, qg_ref, kvg_ref, wuq_ref, wuk_ref,
                     wuv_ref, cos_ref, sin_ref, q_ref, k_ref, v_ref, qc_ref):
    tm = x_ref.shape[0]
    hn = _rms(x_ref[...], g_ref[...]).astype(BF16)
    proj = _dot(hn, wa_ref[...])
    c_q = proj[:, :MLA_Q_RANK]
    c_kv = proj[:, MLA_Q_RANK:MLA_Q_RANK + MLA_KV_RANK]
    qc_ref[...] = proj[:, MLA_Q_RANK + MLA_KV_RANK:].astype(BF16)
    cqn = _rms(c_q, qg_ref[...]).astype(BF16)
    ckvn = _rms(c_kv, kvg_ref[...]).astype(BF16)

    cos = cos_ref[...]
    sin = sin_ref[...]

    def rope(x1, x2):
        return x1 * cos - x2 * sin, x1 * sin + x2 * cos

    kr_t = _dot_nt(wkr_ref[...], hn)
    r1, r2 = rope(kr_t[:ROPE_HALF], kr_t[ROPE_HALF:])
    k_rope = jnp.concatenate(
        [jnp.zeros((MLA_NOPE, tm), F32), r1, r2,
         jnp.zeros((QK_PAD - MLA_QK, tm), F32)], axis=0).T

    k_nope = _dot(ckvn, wuk_ref[...])
    for h in range(MLA_HEADS):
        sl = slice(h * QK_PAD, (h + 1) * QK_PAD)
        k_ref[:, sl] = (k_nope[:, sl] + k_rope).astype(BF16)

    v_t = _dot_nt(wuv_ref[...], ckvn).astype(BF16)
    ones_rows = (lax.broadcasted_iota(jnp.int32, (BF16_ROWS, tm), 0) == 0).astype(BF16)
    for h in range(MLA_HEADS):
        v_ref[0, h * V_ROWS:h * V_ROWS + HEAD_DIM, :] = v_t[h * HEAD_DIM:(h + 1) * HEAD_DIM]
        v_ref[0, h * V_ROWS + HEAD_DIM:(h + 1) * V_ROWS, :] = ones_rows

    q_t = _dot_nt(wuq_ref[...], cqn)
    scale = MLA_QK ** -0.5 * math.log2(math.e)
    pad = jnp.zeros((QK_PAD - MLA_QK, tm), BF16)
    for h in range(MLA_HEADS):
        base = h * QK_PAD
        x1 = q_t[base + MLA_NOPE:base + MLA_NOPE + ROPE_HALF]
        x2 = q_t[base + MLA_NOPE + ROPE_HALF:base + MLA_QK]
        r1, r2 = rope(x1, x2)
        q_ref[0, base:base + MLA_NOPE, :] = (q_t[base:base + MLA_NOPE] * scale).astype(BF16)
        q_ref[0, base + MLA_NOPE:base + MLA_NOPE + ROPE_HALF, :] = (r1 * scale).astype(BF16)
        q_ref[0, base + MLA_NOPE + ROPE_HALF:base + MLA_QK, :] = (r2 * scale).astype(BF16)
        q_ref[0, base + MLA_QK:base + QK_PAD, :] = pad


def _mla_proj(x2, layer, g, w_a, w_kr, qg, kvg, w_uq, w_uk, w_uv, cos_t, sin_t):
    t = x2.shape[0]
    tm = ROW_TILE
    qw = MLA_HEADS * QK_PAD
    row = lambda w: pl.BlockSpec((tm, w), lambda i: (i, 0))
    col = lambda r: pl.BlockSpec((r, tm), lambda i: (0, i))
    tile = lambda r: pl.BlockSpec((1, r, tm), lambda i: (i, 0, 0))
    consts = [g, w_a, w_kr, qg, kvg, w_uq, w_uk, w_uv]
    return pl.pallas_call(
        _mla_proj_kernel,
        grid=(t // tm,),
        in_specs=[row(D_MODEL)] + [_layer_spec(c, layer) for c in consts]
                 + [col(ROPE_HALF), col(ROPE_HALF)],
        out_specs=[tile(qw), row(qw), tile(MLA_HEADS * V_ROWS), row(CROSS_WIDTH)],
        out_shape=[jax.ShapeDtypeStruct((t // tm, qw, tm), BF16),
                   jax.ShapeDtypeStruct((t, qw), BF16),
                   jax.ShapeDtypeStruct((t // tm, MLA_HEADS * V_ROWS, tm), BF16),
                   jax.ShapeDtypeStruct((t, CROSS_WIDTH), BF16)],
        compiler_params=_params("parallel"),
        name="mla_proj",
    )(x2, *consts, cos_t, sin_t)


def _mla_attn_kernel(q_ref, k_ref, v_ref, o_ref, s_ref, mb_ref, m_ref, acc_ref, bias_ref,
                     *, nt):
    t = q_ref.shape[2]
    acc_ref[...] = jnp.zeros(acc_ref.shape, F32)
    key = lax.broadcasted_iota(jnp.int32, (t, t), 0)
    qry = lax.broadcasted_iota(jnp.int32, (t, t), 1)
    bias_ref[...] = jnp.where(key <= qry, 0.0, NEG)

    def scores(i, j, slot, h):
        start = pl.multiple_of(j * t, t)
        kb = k_ref[pl.ds(start, t), h * QK_PAD:(h + 1) * QK_PAD]
        s = _dot(kb, q_ref[i, h * QK_PAD:(h + 1) * QK_PAD, :])
        s_ref[slot, h] = s
        mb_ref[slot, h] = jnp.max(s, axis=0, keepdims=True)

    def consume(i, j, slot, h, diagonal):
        s = s_ref[slot, h]
        if diagonal:
            s = s + bias_ref[...]
            m_blk = jnp.max(s, axis=0, keepdims=True)
        else:
            m_blk = mb_ref[slot, h]
        m_prev = jnp.where(j == 0, NEG, m_ref[h])
        m_new = jnp.maximum(m_prev, m_blk)
        a = jnp.exp2(m_prev - m_new)
        p = jnp.exp2(s - m_new).astype(BF16)
        m_ref[h] = m_new
        rows = slice(h * V_ROWS, (h + 1) * V_ROWS)
        acc_ref[rows, :] = a * acc_ref[rows, :] + _dot(v_ref[j, rows, :], p)
        if diagonal:
            base = h * V_ROWS
            den = acc_ref[base + HEAD_DIM:base + HEAD_DIM + 1, :]
            o_ref[i, h * HEAD_DIM:(h + 1) * HEAD_DIM, :] = (
                acc_ref[base:base + HEAD_DIM, :] / den).astype(BF16)

    def successor(i, j):
        wrap = j == i
        return jnp.minimum(jnp.where(wrap, i + 1, i), nt - 1), jnp.where(wrap, 0, j + 1)

    def block(i, j, slot):
        i_next, j_next = successor(i, j)

        def run(diagonal):
            for h in range(2):
                scores(i_next, j_next, 1 - slot, h)
                consume(i, j, slot, h, diagonal)

        pl.when(j == i)(functools.partial(run, True))
        pl.when(j != i)(functools.partial(run, False))
        return i_next, j_next

    for h in range(2):
        scores(0, 0, 0, h)

    def body(_, ij):
        return block(*block(*ij, 0), 1)

    n_blocks = nt * (nt + 1) // 2
    lax.fori_loop(0, n_blocks // 2, body, (jnp.int32(0), jnp.int32(0)))


def _mla_attn(q_t, k, v_t, batch, seq):
    t = ATTN_TILE
    nt = seq // t
    assert (nt * (nt + 1) // 2) % 2 == 0, "the pipeline retires two blocks per trip"
    pairs = MLA_HEADS // 2
    return pl.pallas_call(
        functools.partial(_mla_attn_kernel, nt=nt),
        grid=(batch, pairs),
        in_specs=[pl.BlockSpec((nt, 2 * QK_PAD, t), lambda b, p: (b, p, 0)),
                  pl.BlockSpec((seq, 2 * QK_PAD), lambda b, p: (b, p)),
                  pl.BlockSpec((nt, 2 * V_ROWS, t), lambda b, p: (b, p, 0))],
        out_specs=pl.BlockSpec((nt, LANES, t), lambda b, p: (b, p, 0)),
        out_shape=jax.ShapeDtypeStruct((batch * nt, MIX_WIDTH, t), BF16),
        scratch_shapes=[pltpu.VMEM((2, 2, t, t), F32), pltpu.VMEM((2, 2, 1, t), F32),
                        pltpu.VMEM((2, 1, t), F32), pltpu.VMEM((2 * V_ROWS, t), F32),
                        pltpu.VMEM((t, t), F32)],
        compiler_params=_params("parallel", "parallel"),
        name="mla_attn",
    )(q_t, k, v_t)


def _swa_proj_kernel(x_ref, g_ref, win_ref, q_ref, k_ref, v_ref, qc_ref):
    hn = _rms(x_ref[...], g_ref[...]).astype(BF16)
    proj = _dot(hn, win_ref[...])
    nq = SWA_Q_HEADS * HEAD_DIM
    nk = SWA_KV_HEADS * HEAD_DIM
    q_ref[...] = proj[:, :nq].astype(BF16)
    k_ref[...] = proj[:, nq:nq + nk].astype(BF16)
    v_ref[...] = proj[:, nq + nk:nq + 2 * nk].astype(BF16)
    qc_ref[...] = proj[:, nq + 2 * nk:].astype(BF16)


def _swa_proj(x2, layer, g, w_in):
    t = x2.shape[0]
    tm = ROW_TILE
    nk = SWA_KV_HEADS * HEAD_DIM
    row = lambda w: pl.BlockSpec((tm, w), lambda i: (i, 0))
    return pl.pallas_call(
        _swa_proj_kernel,
        grid=(t // tm,),
        in_specs=[row(D_MODEL), _layer_spec(g, layer), _layer_spec(w_in, layer)],
        out_specs=[row(MIX_WIDTH), row(nk), row(nk), row(CROSS_WIDTH)],
        out_shape=[jax.ShapeDtypeStruct((t, MIX_WIDTH), BF16),
                   jax.ShapeDtypeStruct((t, nk), BF16),
                   jax.ShapeDtypeStruct((t, nk), BF16),
                   jax.ShapeDtypeStruct((t, CROSS_WIDTH), BF16)],
        compiler_params=_params("parallel"),
        name="swa_proj",
    )(x2, g, w_in)


def _swa_attn_kernel(slope_ref, sink_ref, q_ref, ko_ref, kp_ref, vo_ref, vp_ref,
                     pq_ref, pko_ref, pkp_ref, o_ref, *, nwin):
    i = pl.program_id(1)
    lo = _low_half()
    w_ = WINDOW
    kcat = jnp.concatenate([kp_ref[...], ko_ref[...]], axis=0)
    vcat = jnp.concatenate([vp_ref[...], vo_ref[...]], axis=0)
    qi = lax.broadcasted_iota(jnp.int32, (w_, 2 * w_), 0)
    kj = lax.broadcasted_iota(jnp.int32, (w_, 2 * w_), 1)
    rel = w_ + qi - kj
    band = (rel >= 0) & (rel < w_)
    first_band = band & ((kj >= w_) | (i > 0))

    for w in range(nwin):
        rows = slice(w * w_, (w + 1) * w_)
        pos_q = pq_ref[rows, :]
        pos_prev = pkp_ref[0] if w == 0 else pko_ref[w - 1]
        pos_k = jnp.concatenate([pos_prev, pko_ref[w]], axis=1)
        dist = (pos_q - pos_k).astype(F32)
        valid = first_band if w == 0 else band
        for pair in range(SWA_KV_HEADS // 2):
            ksl = slice(pair * LANES, (pair + 1) * LANES)
            kw = kcat[w * w_:(w + 2) * w_, ksl]
            vw = vcat[w * w_:(w + 2) * w_, ksl]
            outs = []
            for half in range(2):
                sel = lo if half == 0 else jnp.logical_not(lo)
                qs = []
                for g in range(SWA_GROUP):
                    tile = pair * SWA_GROUP + g
                    qt = q_ref[rows, tile * LANES:(tile + 1) * LANES]
                    qs.append(jnp.where(sel, qt, jnp.zeros_like(qt)))
                s3 = _dot_nt(jnp.concatenate(qs, axis=0), kw)
                ps = []
                dens = []
                for g in range(SWA_GROUP):
                    hq = (2 * pair + half) * SWA_GROUP + g
                    sink = sink_ref[hq]
                    s = s3[g * w_:(g + 1) * w_] - slope_ref[hq] * dist
                    s = jnp.where(valid, s, NEG)
                    m = jnp.maximum(jnp.max(s, axis=-1, keepdims=True), sink)
                    e = jnp.exp(s - m)
                    dens.append(jnp.sum(e, axis=-1, keepdims=True) + jnp.exp(sink - m))
                    ps.append(e.astype(BF16))
                o3 = _dot(jnp.concatenate(ps, axis=0), vw)
                outs.append([o3[g * w_:(g + 1) * w_] / dens[g] for g in range(SWA_GROUP)])
            for g in range(SWA_GROUP):
                tile = pair * SWA_GROUP + g
                o_ref[rows, tile * LANES:(tile + 1) * LANES] = jnp.where(
                    lo, outs[0][g], outs[1][g]).astype(BF16)


def _swa_attn(slopes, sinks, q, k, v, pos_col, pos_row, batch, seq):
    rows = SWA_ROWS
    nwin = rows // WINDOW
    nsteps = seq // rows
    nblk = seq // WINDOW
    nk = SWA_KV_HEADS * HEAD_DIM
    own = lambda b, i: (b * nsteps + i, 0)
    prev = lambda b, i: (b * nblk + jnp.maximum(i * nwin - 1, 0), 0)
    smem = pl.BlockSpec(memory_space=pltpu.SMEM)
    return pl.pallas_call(
        functools.partial(_swa_attn_kernel, nwin=nwin),
        grid=(batch, nsteps),
        in_specs=[smem, smem,
                  pl.BlockSpec((rows, MIX_WIDTH), own),
                  pl.BlockSpec((rows, nk), own),
                  pl.BlockSpec((WINDOW, nk), prev),
                  pl.BlockSpec((rows, nk), own),
                  pl.BlockSpec((WINDOW, nk), prev),
                  pl.BlockSpec((rows, 1), own),
                  pl.BlockSpec((nwin, 1, WINDOW), lambda b, i: (b * nsteps + i, 0, 0)),
                  pl.BlockSpec((1, 1, WINDOW),
                               lambda b, i: (b * nblk + jnp.maximum(i * nwin - 1, 0), 0, 0))],
        out_specs=pl.BlockSpec((rows, MIX_WIDTH), own),
        out_shape=jax.ShapeDtypeStruct((batch * seq, MIX_WIDTH), BF16),
        compiler_params=_params("parallel", "parallel"),
        name="swa_attn",
    )(slopes, sinks, q, k, k, v, v, pos_col, pos_row, pos_row)


def _out_mlp_kernel(*refs, final, mix_transposed):
    if final:
        (x_ref, mix_ref, qc_ref, km_ref, vm_ref, wo_ref, g_ref, wup_ref, wdn_ref,
         gf_ref, o_ref, acc_ref) = refs
    else:
        (x_ref, mix_ref, qc_ref, km_ref, vm_ref, wo_ref, g_ref, wup_ref, wdn_ref,
         o_ref, acc_ref) = refs
    lo = _low_half()
    cross = []
    for pair in range(CROSS_WIDTH // LANES):
        sl = slice(pair * LANES, (pair + 1) * LANES)
        qp = qc_ref[:, sl]
        kp = km_ref[:, sl]
        vp = vm_ref[:, sl]
        outs = []
        for half in range(2):
            sel = lo if half == 0 else jnp.logical_not(lo)
            s = _dot_nt(jnp.where(sel, qp, jnp.zeros_like(qp)), kp)
            e = jnp.exp(s - jnp.max(s, axis=-1, keepdims=True))
            den = jnp.sum(e, axis=-1, keepdims=True)
            outs.append(_dot(e.astype(BF16), vp) / den)
        cross.append(jnp.where(lo, outs[0], outs[1]).astype(BF16))
    if mix_transposed:
        mix = mix_ref[0].astype(F32).T.astype(BF16)
    else:
        mix = mix_ref[...]
    attn = jnp.concatenate([mix] + cross, axis=1)
    x1 = x_ref[...] + _dot(attn, wo_ref[...])
    hn = _rms(x1, g_ref[...]).astype(BF16)
    acc_ref[...] = x1
    for c in range(D_FF // FF_CHUNK):
        cols = slice(c * FF_CHUNK, (c + 1) * FF_CHUNK)
        h = jnp.maximum(_dot(hn, wup_ref[:, cols]), 0.0)
        acc_ref[...] += _dot((h * h).astype(BF16), wdn_ref[cols, :])
    if final:
        o_ref[...] = _rms(acc_ref[...], gf_ref[...])
    else:
        o_ref[...] = acc_ref[...]


def _out_mlp(x2, mix, qc, memkv, layer, w_o, wo_layer, g, w_up, w_dn, g_final, seq,
             mix_transposed):
    t = x2.shape[0]
    tm = ROW_TILE
    per_b = seq // tm
    final = g_final is not None
    row = lambda w: pl.BlockSpec((tm, w), lambda i: (i, 0))
    mix_spec = (pl.BlockSpec((1, MIX_WIDTH, tm), lambda i: (i, 0, 0)) if mix_transposed
                else row(MIX_WIDTH))
    in_specs = [row(D_MODEL), mix_spec, row(CROSS_WIDTH),
                pl.BlockSpec((N_MEM, CROSS_WIDTH), lambda i: (i // per_b, 2 * layer)),
                pl.BlockSpec((N_MEM, CROSS_WIDTH), lambda i: (i // per_b, 2 * layer + 1)),
                _layer_spec(w_o, wo_layer), _layer_spec(g, layer),
                _layer_spec(w_up, layer), _layer_spec(w_dn, layer)]
    args = [x2, mix, qc, memkv, memkv, w_o, g, w_up, w_dn]
    if final:
        in_specs.append(_const_spec(g_final.shape))
        args.append(g_final)
    return pl.pallas_call(
        functools.partial(_out_mlp_kernel, final=final, mix_transposed=mix_transposed),
        grid=(t // tm,),
        in_specs=in_specs,
        out_specs=row(D_MODEL),
        out_shape=jax.ShapeDtypeStruct((t, D_MODEL), F32),
        scratch_shapes=[pltpu.VMEM((tm, D_MODEL), F32)],
        compiler_params=_params("parallel"),
        name="out_mlp",
    )(*args)


def _prep_mla(w_in, w_uq, w_ukv):
    n = w_in.shape[0]
    n1 = MLA_Q_RANK + MLA_KV_RANK
    w_in = w_in.astype(BF16)
    w_a = jnp.concatenate([w_in[:, :, :n1], w_in[:, :, n1 + MLA_ROPE:]], axis=2)
    w_kr_t = w_in[:, :, n1:n1 + MLA_ROPE].transpose(0, 2, 1)
    w_uq_t = jnp.pad(w_uq.astype(BF16).reshape(n, MLA_Q_RANK, MLA_HEADS, MLA_QK),
                     ((0, 0), (0, 0), (0, 0), (0, QK_PAD - MLA_QK)))
    w_uq_t = w_uq_t.reshape(n, MLA_Q_RANK, MLA_HEADS * QK_PAD).transpose(0, 2, 1)
    kv = w_ukv.astype(BF16).reshape(n, MLA_KV_RANK, MLA_HEADS, 2 * HEAD_DIM)
    w_uk = jnp.pad(kv[..., :MLA_NOPE], ((0, 0), (0, 0), (0, 0), (0, QK_PAD - MLA_NOPE)))
    w_uk = w_uk.reshape(n, MLA_KV_RANK, MLA_HEADS * QK_PAD)
    w_uv_t = kv[..., MLA_NOPE:].reshape(n, MLA_KV_RANK, MIX_WIDTH).transpose(0, 2, 1)
    return w_a, w_kr_t, w_uq_t, w_uk, w_uv_t


def _pair_interleave(a, axis):
    shape = a.shape
    a = jnp.moveaxis(a, axis, 0)
    rest = a.shape[1:]
    a = a.reshape((SWA_KV_HEADS // 2, 2, SWA_GROUP, HEAD_DIM) + rest)
    a = a.transpose((0, 2, 1, 3) + tuple(range(4, 4 + len(rest))))
    a = a.reshape((MIX_WIDTH,) + rest)
    return jnp.moveaxis(a, 0, axis).reshape(shape)


def _prep_swa(w_in, w_o):
    nq = SWA_Q_HEADS * HEAD_DIM
    nk = SWA_KV_HEADS * HEAD_DIM
    w_in = w_in.astype(BF16)
    w_q = _pair_interleave(w_in[:, :, :nq], 2)
    w_k = w_in[:, :, nq:nq + nk] * (HEAD_DIM ** -0.5)
    w_in_p = jnp.concatenate([w_q, w_k, w_in[:, :, nq + nk:]], axis=2)
    w_o = w_o.astype(BF16)
    w_o_p = jnp.concatenate([_pair_interleave(w_o[:, :MIX_WIDTH], 1), w_o[:, MIX_WIDTH:]], axis=1)
    return w_in_p, w_o_p


def kernel(x, mem, positions, attn_norm_g, mlp_norm_g, mem_norm_g, final_norm_g,
           mla_w_in, mla_q_norm_g, mla_kv_norm_g, mla_w_uq, mla_w_ukv,
           swa_w_in, swa_sinks, w_mem_kv, w_o, mlp_w_up, mlp_w_down):
    batch, seq, d = x.shape
    depth = attn_norm_g.shape[0]
    t = batch * seq
    x2 = x.reshape(t, d)

    w_mem = w_mem_kv.astype(BF16)
    w_mem = jnp.concatenate([w_mem[:, :, :CROSS_WIDTH] * (HEAD_DIM ** -0.5),
                             w_mem[:, :, CROSS_WIDTH:]], axis=2)
    memkv = _memkv(mem.reshape(batch * N_MEM, d), mem_norm_g.reshape(1, d), w_mem)

    inv = ROPE_THETA ** (-(jnp.arange(ROPE_HALF, dtype=F32) * 2.0) / MLA_ROPE)
    cos_t, sin_t = _rope_tables(positions.astype(F32).reshape(1, t), inv.reshape(ROPE_HALF, 1))

    pos_col_i = positions.reshape(t, 1)
    pos_row_i = positions.reshape(t // WINDOW, 1, WINDOW)
    slopes = 2.0 ** (-8.0 * (jnp.arange(SWA_Q_HEADS, dtype=F32) + 1.0) / SWA_Q_HEADS)

    g_attn = attn_norm_g.reshape(depth, 1, d)
    g_mlp = mlp_norm_g.reshape(depth, 1, d)
    mla_w = _prep_mla(mla_w_in, mla_w_uq, mla_w_ukv)
    mla_qg = mla_q_norm_g.reshape(-1, 1, MLA_Q_RANK)
    mla_kvg = mla_kv_norm_g.reshape(-1, 1, MLA_KV_RANK)
    swa_w_in_p, w_o_swa = _prep_swa(swa_w_in, w_o[1::2])
    w_o_mla = w_o[0::2].astype(BF16)
    w_up = mlp_w_up.astype(BF16)
    w_dn = mlp_w_down.astype(BF16)
    sinks = swa_sinks.astype(F32)

    for i in range(depth):
        j = i // 2
        is_mla = i % 2 == 0
        if is_mla:
            w_a, w_kr_t, w_uq_t, w_uk, w_uv_t = mla_w
            q_t, k, v_t, qc = _mla_proj(x2, j, g_attn[0::2], w_a, w_kr_t, mla_qg, mla_kvg,
                                        w_uq_t, w_uk, w_uv_t, cos_t, sin_t)
            mix = _mla_attn(q_t, k, v_t, batch, seq)
        else:
            q, k, v, qc = _swa_proj(x2, j, g_attn[1::2], swa_w_in_p)
            mix = _swa_attn(slopes, sinks[j], q, k, v, pos_col_i, pos_row_i, batch, seq)
        g_final = final_norm_g.reshape(1, d) if i == depth - 1 else None
        x2 = _out_mlp(x2, mix, qc, memkv, i, w_o_mla if is_mla else w_o_swa, j, g_mlp,
                      w_up, w_dn, g_final, seq, is_mla)
    return x2.reshape(batch, seq, d)
```

```python
import functools
import math

import jax
import jax.numpy as jnp
from jax import lax
from jax.experimental import pallas as pl
from jax.experimental.pallas import tpu as pltpu

F32 = jnp.float32
BF16 = jnp.bfloat16

D_MODEL = 1024
HEAD_DIM = 64
N_MEM = 256
CROSS_WIDTH = 256
MIX_WIDTH = 768
EPS = 1e-6
NEG = -1e30
LOG2E = math.log2(math.e)
MASK_DIST = 1e32
MLA_HEADS = 12
MLA_Q_RANK = 384
MLA_KV_RANK = 256
MLA_NOPE = 64
MLA_ROPE = 32
MLA_QK = MLA_NOPE + MLA_ROPE
ROPE_THETA = 10000.0
SWA_Q_HEADS = 12
SWA_KV_HEADS = 4
SWA_GROUP = 3
WINDOW = 128
D_FF = 4 * D_MODEL

LANES = 128
QK_PAD = LANES
ROPE_HALF = MLA_ROPE // 2
BF16_ROWS = 16
V_ROWS = HEAD_DIM + BF16_ROWS

ROW_TILE = 512
ATTN_TILE = ROW_TILE
FULL_BLOCKS_PER_TRIP = 4
DIAG_BLOCKS_PER_TRIP = 2
SWA_ROWS = 512
FF_CHUNK = 1024
VMEM_LIMIT = 48 * 1024 * 1024


def _rms(x, g):
    ms = jnp.mean(x * x, axis=-1, keepdims=True)
    return x * lax.rsqrt(ms + EPS) * g


def _dot(a, b):
    return jnp.dot(a, b, preferred_element_type=F32)


def _dot_nt(a, b):
    return lax.dot_general(a, b, (((1,), (1,)), ((), ())), preferred_element_type=F32)


def _low_half():
    return lax.broadcasted_iota(jnp.int32, (1, LANES), 1) < HEAD_DIM


def _const_spec(shape):
    nd = len(shape)
    return pl.BlockSpec(shape, lambda *_: (0,) * nd, pipeline_mode=pl.Buffered(1))


def _layer_spec(stacked, layer):
    nd = stacked.ndim
    return pl.BlockSpec((None,) + stacked.shape[1:], lambda *_: (layer,) + (0,) * (nd - 1),
                        pipeline_mode=pl.Buffered(1))


def _params(*sem):
    return pltpu.CompilerParams(dimension_semantics=sem, vmem_limit_bytes=VMEM_LIMIT)


def _rope_table_kernel(pos_ref, inv_ref, cos_ref, sin_ref):
    ang = inv_ref[...] * pos_ref[...]
    cos_ref[...] = jnp.cos(ang)
    sin_ref[...] = jnp.sin(ang)


def _rope_tables(pos_row, inv_col):
    t = pos_row.shape[1]
    tn = 2048
    return pl.pallas_call(
        _rope_table_kernel,
        grid=(t // tn,),
        in_specs=[pl.BlockSpec((1, tn), lambda i: (0, i)),
                  pl.BlockSpec((ROPE_HALF, 1), lambda i: (0, 0))],
        out_specs=[pl.BlockSpec((ROPE_HALF, tn), lambda i: (0, i))] * 2,
        out_shape=[jax.ShapeDtypeStruct((ROPE_HALF, t), F32)] * 2,
        compiler_params=_params("parallel"),
        name="rope_tables",
    )(pos_row, inv_col)


def _memkv_kernel(mem_ref, g_ref, w_ref, o_ref):
    mn = _rms(mem_ref[...], g_ref[...]).astype(BF16)
    o_ref[...] = _dot(mn, w_ref[...]).astype(BF16)


def _memkv(mem2, g, w_all):
    rows = mem2.shape[0]
    depth, _, ncol = w_all.shape
    return pl.pallas_call(
        _memkv_kernel,
        grid=(rows // N_MEM, depth),
        in_specs=[pl.BlockSpec((N_MEM, D_MODEL), lambda b, l: (b, 0)),
                  _const_spec((1, D_MODEL)),
                  pl.BlockSpec((None, D_MODEL, ncol), lambda b, l: (l, 0, 0))],
        out_specs=pl.BlockSpec((N_MEM, ncol), lambda b, l: (b, l)),
        out_shape=jax.ShapeDtypeStruct((rows, depth * ncol), BF16),
        compiler_params=_params("parallel", "parallel"),
        name="mem_kv",
    )(mem2, g, w_all)


def _mla_proj_kernel(x_ref, g_ref, wa_ref, wkr_ref, qg_ref, kvg_ref, wuq_ref, wuk_ref,
                     wuv_ref, cos_ref, sin_ref, q_ref, k_ref, v_ref, qc_ref):
    tm = x_ref.shape[0]
    hn = _rms(x_ref[...], g_ref[...]).astype(BF16)
    proj = _dot(hn, wa_ref[...])
    c_q = proj[:, :MLA_Q_RANK]
    c_kv = proj[:, MLA_Q_RANK:MLA_Q_RANK + MLA_KV_RANK]
    qc_ref[...] = proj[:, MLA_Q_RANK + MLA_KV_RANK:].astype(BF16)
    cqn = _rms(c_q, qg_ref[...]).astype(BF16)
    ckvn = _rms(c_kv, kvg_ref[...]).astype(BF16)

    cos = cos_ref[...]
    sin = sin_ref[...]

    def rope(x1, x2):
        return x1 * cos - x2 * sin, x1 * sin + x2 * cos

    kr_t = _dot_nt(wkr_ref[...], hn)
    r1, r2 = rope(kr_t[:ROPE_HALF], kr_t[ROPE_HALF:])
    k_rope = jnp.concatenate(
        [jnp.zeros((MLA_NOPE, tm), F32), r1, r2,
         jnp.zeros((QK_PAD - MLA_QK, tm), F32)], axis=0).T

    k_nope = _dot(ckvn, wuk_ref[...])
    for h in range(MLA_HEADS):
        sl = slice(h * QK_PAD, (h + 1) * QK_PAD)
        k_ref[:, sl] = (k_nope[:, sl] + k_rope).astype(BF16)

    v_t = _dot_nt(wuv_ref[...], ckvn).astype(BF16)
    ones_rows = (lax.broadcasted_iota(jnp.int32, (BF16_ROWS, tm), 0) == 0).astype(BF16)
    for h in range(MLA_HEADS):
        v_ref[0, h * V_ROWS:h * V_ROWS + HEAD_DIM, :] = v_t[h * HEAD_DIM:(h + 1) * HEAD_DIM]
        v_ref[0, h * V_ROWS + HEAD_DIM:(h + 1) * V_ROWS, :] = ones_rows

    q_t = _dot_nt(wuq_ref[...], cqn)
    scale = MLA_QK ** -0.5 * LOG2E
    pad = jnp.zeros((QK_PAD - MLA_QK, tm), BF16)
    for h in range(MLA_HEADS):
        base = h * QK_PAD
        x1 = q_t[base + MLA_NOPE:base + MLA_NOPE + ROPE_HALF]
        x2 = q_t[base + MLA_NOPE + ROPE_HALF:base + MLA_QK]
        r1, r2 = rope(x1, x2)
        q_ref[0, base:base + MLA_NOPE, :] = (q_t[base:base + MLA_NOPE] * scale).astype(BF16)
        q_ref[0, base + MLA_NOPE:base + MLA_NOPE + ROPE_HALF, :] = (r1 * scale).astype(BF16)
        q_ref[0, base + MLA_NOPE + ROPE_HALF:base + MLA_QK, :] = (r2 * scale).astype(BF16)
        q_ref[0, base + MLA_QK:base + QK_PAD, :] = pad


def _mla_proj(x2, layer, g, w_a, w_kr, qg, kvg, w_uq, w_uk, w_uv, cos_t, sin_t):
    t = x2.shape[0]
    tm = ROW_TILE
    qw = MLA_HEADS * QK_PAD
    row = lambda w: pl.BlockSpec((tm, w), lambda i: (i, 0))
    col = lambda r: pl.BlockSpec((r, tm), lambda i: (0, i))
    tile = lambda r: pl.BlockSpec((1, r, tm), lambda i: (i, 0, 0))
    consts = [g, w_a, w_kr, qg, kvg, w_uq, w_uk, w_uv]
    return pl.pallas_call(
        _mla_proj_kernel,
        grid=(t // tm,),
        in_specs=[row(D_MODEL)] + [_layer_spec(c, layer) for c in consts]
                 + [col(ROPE_HALF), col(ROPE_HALF)],
        out_specs=[tile(qw), row(qw), tile(MLA_HEADS * V_ROWS), row(CROSS_WIDTH)],
        out_shape=[jax.ShapeDtypeStruct((t // tm, qw, tm), BF16),
                   jax.ShapeDtypeStruct((t, qw), BF16),
                   jax.ShapeDtypeStruct((t // tm, MLA_HEADS * V_ROWS, tm), BF16),
                   jax.ShapeDtypeStruct((t, CROSS_WIDTH), BF16)],
        compiler_params=_params("parallel"),
        name="mla_proj",
    )(x2, *consts, cos_t, sin_t)


def _mla_attn_kernel(q_ref, k_ref, v_ref, o_ref, s_ref, mb_ref, m_ref, acc_ref, bias_ref,
                     *, nt):
    t = q_ref.shape[2]
    acc_ref[...] = jnp.zeros(acc_ref.shape, F32)
    key = lax.broadcasted_iota(jnp.int32, (t, t), 0)
    qry = lax.broadcasted_iota(jnp.int32, (t, t), 1)
    bias_ref[...] = jnp.where(key <= qry, 0.0, NEG)

    def scores(i, j, slot, h):
        start = pl.multiple_of(j * t, t)
        kb = k_ref[pl.ds(start, t), h * QK_PAD:(h + 1) * QK_PAD]
        s = _dot(kb, q_ref[i, h * QK_PAD:(h + 1) * QK_PAD, :])
        s_ref[slot, h] = s
        mb_ref[slot, h] = jnp.max(s, axis=0, keepdims=True)

    def consume(i, j, slot, h, diagonal):
        s = s_ref[slot, h]
        if diagonal:
            s = s + bias_ref[...]
            m_blk = jnp.max(s, axis=0, keepdims=True)
            fresh = i == 0
        else:
            m_blk = mb_ref[slot, h]
            fresh = j == 0
        m_prev = jnp.where(fresh, NEG, m_ref[i, h])
        m_new = jnp.maximum(m_prev, m_blk)
        a = jnp.exp2(m_prev - m_new)
        p = jnp.exp2(s - m_new).astype(BF16)
        rows = slice(h * V_ROWS, (h + 1) * V_ROWS)
        acc = a * acc_ref[i, rows, :] + _dot(v_ref[j, rows, :], p)
        if diagonal:
            o_ref[i, h * HEAD_DIM:(h + 1) * HEAD_DIM, :] = (
                acc[:HEAD_DIM] / acc[HEAD_DIM:HEAD_DIM + 1]).astype(BF16)
        else:
            m_ref[i, h] = m_new
            acc_ref[i, rows, :] = acc

    def block(cur, nxt, slot, diagonal):
        for h in range(2):
            scores(*nxt, 1 - slot, h)
            consume(*cur, slot, h, diagonal)

    def next_full(i, j):
        wrap = j + 1 == i
        i_n = jnp.where(wrap, i + 1, i)
        j_n = jnp.where(wrap, 0, j + 1)
        done = i_n == nt
        return jnp.where(done, 0, i_n), jnp.where(done, 0, j_n)

    def next_diag(i, j):
        i_n = jnp.minimum(i + 1, nt - 1)
        return i_n, i_n

    def run(n_blocks, per_trip, first, successor, diagonal):
        def body(_, cur):
            for b in range(per_trip):
                nxt = successor(*cur)
                block(cur, nxt, b % 2, diagonal)
                cur = nxt
            return cur
        return lax.fori_loop(0, n_blocks // per_trip, body, first)

    for h in range(2):
        scores(1, 0, 0, h)
    first_diag = run(nt * (nt - 1) // 2, FULL_BLOCKS_PER_TRIP,
                     (jnp.int32(1), jnp.int32(0)), next_full, False)
    run(nt, DIAG_BLOCKS_PER_TRIP, first_diag, next_diag, True)


def _mla_attn(q_t, k, v_t, batch, seq):
    t = ATTN_TILE
    nt = seq // t
    assert FULL_BLOCKS_PER_TRIP % 2 == 0 and DIAG_BLOCKS_PER_TRIP % 2 == 0
    assert (nt * (nt - 1) // 2) % FULL_BLOCKS_PER_TRIP == 0 and nt % DIAG_BLOCKS_PER_TRIP == 0
    pairs = MLA_HEADS // 2
    return pl.pallas_call(
        functools.partial(_mla_attn_kernel, nt=nt),
        grid=(batch, pairs),
        in_specs=[pl.BlockSpec((nt, 2 * QK_PAD, t), lambda b, p: (b, p, 0)),
                  pl.BlockSpec((seq, 2 * QK_PAD), lambda b, p: (b, p)),
                  pl.BlockSpec((nt, 2 * V_ROWS, t), lambda b, p: (b, p, 0))],
        out_specs=pl.BlockSpec((nt, LANES, t), lambda b, p: (b, p, 0)),
        out_shape=jax.ShapeDtypeStruct((batch * nt, MIX_WIDTH, t), BF16),
        scratch_shapes=[pltpu.VMEM((2, 2, t, t), F32), pltpu.VMEM((2, 2, 1, t), F32),
                        pltpu.VMEM((nt, 2, 1, t), F32), pltpu.VMEM((nt, 2 * V_ROWS, t), F32),
                        pltpu.VMEM((t, t), F32)],
        compiler_params=_params("parallel", "parallel"),
        name="mla_attn",
    )(q_t, k, v_t)


def _swa_proj_kernel(x_ref, g_ref, win_ref, q_ref, k_ref, v_ref, qc_ref):
    hn = _rms(x_ref[...], g_ref[...]).astype(BF16)
    proj = _dot(hn, win_ref[...])
    nq = SWA_Q_HEADS * HEAD_DIM
    nk = SWA_KV_HEADS * HEAD_DIM
    q_ref[...] = (proj[:, :nq] * (HEAD_DIM ** -0.5 * LOG2E)).astype(BF16)
    k_ref[...] = proj[:, nq:nq + nk].astype(BF16)
    v_ref[...] = proj[:, nq + nk:nq + 2 * nk].astype(BF16)
    qc_ref[...] = proj[:, nq + 2 * nk:].astype(BF16)


def _swa_proj(x2, layer, g, w_in):
    t = x2.shape[0]
    tm = ROW_TILE
    nk = SWA_KV_HEADS * HEAD_DIM
    row = lambda w: pl.BlockSpec((tm, w), lambda i: (i, 0))
    return pl.pallas_call(
        _swa_proj_kernel,
        grid=(t // tm,),
        in_specs=[row(D_MODEL), _layer_spec(g, layer), _layer_spec(w_in, layer)],
        out_specs=[row(MIX_WIDTH), row(nk), row(nk), row(CROSS_WIDTH)],
        out_shape=[jax.ShapeDtypeStruct((t, MIX_WIDTH), BF16),
                   jax.ShapeDtypeStruct((t, nk), BF16),
                   jax.ShapeDtypeStruct((t, nk), BF16),
                   jax.ShapeDtypeStruct((t, CROSS_WIDTH), BF16)],
        compiler_params=_params("parallel"),
        name="swa_proj",
    )(x2, g, w_in)


def _swa_attn_kernel(slope_ref, sink_ref, q_ref, ko_ref, kp_ref, vo_ref, vp_ref,
                     pq_ref, pko_ref, pkp_ref, o_ref, *, nwin):
    i = pl.program_id(1)
    lo = _low_half()
    w_ = WINDOW
    kcat = jnp.concatenate([kp_ref[...], ko_ref[...]], axis=0)
    vcat = jnp.concatenate([vp_ref[...], vo_ref[...]], axis=0)
    qi = lax.broadcasted_iota(jnp.int32, (w_, 2 * w_), 0)
    kj = lax.broadcasted_iota(jnp.int32, (w_, 2 * w_), 1)
    rel = w_ + qi - kj
    band = (rel >= 0) & (rel < w_)
    first_band = band & ((kj >= w_) | (i > 0))

    for w in range(nwin):
        rows = slice(w * w_, (w + 1) * w_)
        pos_q = pq_ref[rows, :]
        pos_prev = pkp_ref[0] if w == 0 else pko_ref[w - 1]
        pos_k = jnp.concatenate([pos_prev, pko_ref[w]], axis=1)
        dist = jnp.where(first_band if w == 0 else band,
                         (pos_q - pos_k).astype(F32), MASK_DIST)
        for pair in range(SWA_KV_HEADS // 2):
            ksl = slice(pair * LANES, (pair + 1) * LANES)
            kw = kcat[w * w_:(w + 2) * w_, ksl]
            vw = vcat[w * w_:(w + 2) * w_, ksl]
            outs = []
            for half in range(2):
                sel = lo if half == 0 else jnp.logical_not(lo)
                qs = []
                for g in range(SWA_GROUP):
                    tile = pair * SWA_GROUP + g
                    qt = q_ref[rows, tile * LANES:(tile + 1) * LANES]
                    qs.append(jnp.where(sel, qt, jnp.zeros_like(qt)))
                s3 = _dot_nt(jnp.concatenate(qs, axis=0), kw)
                ps = []
                dens = []
                for g in range(SWA_GROUP):
                    hq = (2 * pair + half) * SWA_GROUP + g
                    sink = sink_ref[hq]
                    s = s3[g * w_:(g + 1) * w_] - slope_ref[hq] * dist
                    m = jnp.maximum(jnp.max(s, axis=-1, keepdims=True), sink)
                    e = jnp.exp2(s - m)
                    dens.append(jnp.sum(e, axis=-1, keepdims=True) + jnp.exp2(sink - m))
                    ps.append(e.astype(BF16))
                o3 = _dot(jnp.concatenate(ps, axis=0), vw)
                outs.append([o3[g * w_:(g + 1) * w_] / dens[g] for g in range(SWA_GROUP)])
            for g in range(SWA_GROUP):
                tile = pair * SWA_GROUP + g
                o_ref[rows, tile * LANES:(tile + 1) * LANES] = jnp.where(
                    lo, outs[0][g], outs[1][g]).astype(BF16)


def _swa_attn(slopes, sinks, q, k, v, pos_col, pos_row, batch, seq):
    rows = SWA_ROWS
    nwin = rows // WINDOW
    nsteps = seq // rows
    nblk = seq // WINDOW
    nk = SWA_KV_HEADS * HEAD_DIM
    own = lambda b, i: (b * nsteps + i, 0)
    prev = lambda b, i: (b * nblk + jnp.maximum(i * nwin - 1, 0), 0)
    smem = pl.BlockSpec(memory_space=pltpu.SMEM)
    return pl.pallas_call(
        functools.partial(_swa_attn_kernel, nwin=nwin),
        grid=(batch, nsteps),
        in_specs=[smem, smem,
                  pl.BlockSpec((rows, MIX_WIDTH), own),
                  pl.BlockSpec((rows, nk), own),
                  pl.BlockSpec((WINDOW, nk), prev),
                  pl.BlockSpec((rows, nk), own),
                  pl.BlockSpec((WINDOW, nk), prev),
                  pl.BlockSpec((rows, 1), own),
                  pl.BlockSpec((nwin, 1, WINDOW), lambda b, i: (b * nsteps + i, 0, 0)),
                  pl.BlockSpec((1, 1, WINDOW),
                               lambda b, i: (b * nblk + jnp.maximum(i * nwin - 1, 0), 0, 0))],
        out_specs=pl.BlockSpec((rows, MIX_WIDTH), own),
        out_shape=jax.ShapeDtypeStruct((batch * seq, MIX_WIDTH), BF16),
        compiler_params=_params("parallel", "parallel"),
        name="swa_attn",
    )(slopes, sinks, q, k, k, v, v, pos_col, pos_row, pos_row)


def _out_mlp_kernel(*refs, final, mix_transposed):
    if final:
        (x_ref, mix_ref, qc_ref, km_ref, vm_ref, wo_ref, g_ref, wup_ref, wdn_ref,
         gf_ref, o_ref, acc_ref) = refs
    else:
        (x_ref, mix_ref, qc_ref, km_ref, vm_ref, wo_ref, g_ref, wup_ref, wdn_ref,
         o_ref, acc_ref) = refs
    lo = _low_half()
    cross = []
    for pair in range(CROSS_WIDTH // LANES):
        sl = slice(pair * LANES, (pair + 1) * LANES)
        qp = qc_ref[:, sl]
        kp = km_ref[:, sl]
        vp = vm_ref[:, sl]
        outs = []
        for half in range(2):
            sel = lo if half == 0 else jnp.logical_not(lo)
            s = _dot_nt(jnp.where(sel, qp, jnp.zeros_like(qp)), kp)
            e = jnp.exp(s - jnp.max(s, axis=-1, keepdims=True))
            den = jnp.sum(e, axis=-1, keepdims=True)
            outs.append(_dot(e.astype(BF16), vp) / den)
        cross.append(jnp.where(lo, outs[0], outs[1]).astype(BF16))
    if mix_transposed:
        mix = mix_ref[0].astype(F32).T.astype(BF16)
    else:
        mix = mix_ref[...]
    attn = jnp.concatenate([mix] + cross, axis=1)
    x1 = x_ref[...] + _dot(attn, wo_ref[...])
    hn = _rms(x1, g_ref[...]).astype(BF16)
    acc_ref[...] = x1
    for c in range(D_FF // FF_CHUNK):
        cols = slice(c * FF_CHUNK, (c + 1) * FF_CHUNK)
        h = jnp.maximum(_dot(hn, wup_ref[:, cols]), 0.0)
        acc_ref[...] += _dot((h * h).astype(BF16), wdn_ref[cols, :])
    if final:
        o_ref[...] = _rms(acc_ref[...], gf_ref[...])
    else:
        o_ref[...] = acc_ref[...]


def _out_mlp(x2, mix, qc, memkv, layer, w_o, wo_layer, g, w_up, w_dn, g_final, seq,
             mix_transposed):
    t = x2.shape[0]
    tm = ROW_TILE
    per_b = seq // tm
    final = g_final is not None
    row = lambda w: pl.BlockSpec((tm, w), lambda i: (i, 0))
    mix_spec = (pl.BlockSpec((1, MIX_WIDTH, tm), lambda i: (i, 0, 0)) if mix_transposed
                else row(MIX_WIDTH))
    in_specs = [row(D_MODEL), mix_spec, row(CROSS_WIDTH),
                pl.BlockSpec((N_MEM, CROSS_WIDTH), lambda i: (i // per_b, 2 * layer)),
                pl.BlockSpec((N_MEM, CROSS_WIDTH), lambda i: (i // per_b, 2 * layer + 1)),
                _layer_spec(w_o, wo_layer), _layer_spec(g, layer),
                _layer_spec(w_up, layer), _layer_spec(w_dn, layer)]
    args = [x2, mix, qc, memkv, memkv, w_o, g, w_up, w_dn]
    if final:
        in_specs.append(_const_spec(g_final.shape))
        args.append(g_final)
    return pl.pallas_call(
        functools.partial(_out_mlp_kernel, final=final, mix_transposed=mix_transposed),
        grid=(t // tm,),
        in_specs=in_specs,
        out_specs=row(D_MODEL),
        out_shape=jax.ShapeDtypeStruct((t, D_MODEL), F32),
        scratch_shapes=[pltpu.VMEM((tm, D_MODEL), F32)],
        compiler_params=_params("parallel"),
        name="out_mlp",
    )(*args)


def _prep_mla(w_in, w_uq, w_ukv):
    n = w_in.shape[0]
    n1 = MLA_Q_RANK + MLA_KV_RANK
    w_in = w_in.astype(BF16)
    w_a = jnp.concatenate([w_in[:, :, :n1], w_in[:, :, n1 + MLA_ROPE:]], axis=2)
    w_kr_t = w_in[:, :, n1:n1 + MLA_ROPE].transpose(0, 2, 1)
    w_uq_t = jnp.pad(w_uq.astype(BF16).reshape(n, MLA_Q_RANK, MLA_HEADS, MLA_QK),
                     ((0, 0), (0, 0), (0, 0), (0, QK_PAD - MLA_QK)))
    w_uq_t = w_uq_t.reshape(n, MLA_Q_RANK, MLA_HEADS * QK_PAD).transpose(0, 2, 1)
    kv = w_ukv.astype(BF16).reshape(n, MLA_KV_RANK, MLA_HEADS, 2 * HEAD_DIM)
    w_uk = jnp.pad(kv[..., :MLA_NOPE], ((0, 0), (0, 0), (0, 0), (0, QK_PAD - MLA_NOPE)))
    w_uk = w_uk.reshape(n, MLA_KV_RANK, MLA_HEADS * QK_PAD)
    w_uv_t = kv[..., MLA_NOPE:].reshape(n, MLA_KV_RANK, MIX_WIDTH).transpose(0, 2, 1)
    return w_a, w_kr_t, w_uq_t, w_uk, w_uv_t


def _pair_interleave(a, axis):
    shape = a.shape
    a = jnp.moveaxis(a, axis, 0)
    rest = a.shape[1:]
    a = a.reshape((SWA_KV_HEADS // 2, 2, SWA_GROUP, HEAD_DIM) + rest)
    a = a.transpose((0, 2, 1, 3) + tuple(range(4, 4 + len(rest))))
    a = a.reshape((MIX_WIDTH,) + rest)
    return jnp.moveaxis(a, 0, axis).reshape(shape)


def _prep_swa(w_in, w_o):
    nq = SWA_Q_HEADS * HEAD_DIM
    w_in = w_in.astype(BF16)
    w_in_p = jnp.concatenate([_pair_interleave(w_in[:, :, :nq], 2), w_in[:, :, nq:]], axis=2)
    w_o = w_o.astype(BF16)
    w_o_p = jnp.concatenate([_pair_interleave(w_o[:, :MIX_WIDTH], 1), w_o[:, MIX_WIDTH:]], axis=1)
    return w_in_p, w_o_p


def kernel(x, mem, positions, attn_norm_g, mlp_norm_g, mem_norm_g, final_norm_g,
           mla_w_in, mla_q_norm_g, mla_kv_norm_g, mla_w_uq, mla_w_ukv,
           swa_w_in, swa_sinks, w_mem_kv, w_o, mlp_w_up, mlp_w_down):
    batch, seq, d = x.shape
    depth = attn_norm_g.shape[0]
    t = batch * seq
    x2 = x.reshape(t, d)

    w_mem = w_mem_kv.astype(BF16)
    w_mem = jnp.concatenate([w_mem[:, :, :CROSS_WIDTH] * (HEAD_DIM ** -0.5),
                             w_mem[:, :, CROSS_WIDTH:]], axis=2)
    memkv = _memkv(mem.reshape(batch * N_MEM, d), mem_norm_g.reshape(1, d), w_mem)

    inv = ROPE_THETA ** (-(jnp.arange(ROPE_HALF, dtype=F32) * 2.0) / MLA_ROPE)
    cos_t, sin_t = _rope_tables(positions.astype(F32).reshape(1, t), inv.reshape(ROPE_HALF, 1))

    pos_col_i = positions.reshape(t, 1)
    pos_row_i = positions.reshape(t // WINDOW, 1, WINDOW)
    slopes = 2.0 ** (-8.0 * (jnp.arange(SWA_Q_HEADS, dtype=F32) + 1.0) / SWA_Q_HEADS) * LOG2E

    g_attn = attn_norm_g.reshape(depth, 1, d)
    g_mlp = mlp_norm_g.reshape(depth, 1, d)
    mla_w = _prep_mla(mla_w_in, mla_w_uq, mla_w_ukv)
    mla_qg = mla_q_norm_g.reshape(-1, 1, MLA_Q_RANK)
    mla_kvg = mla_kv_norm_g.reshape(-1, 1, MLA_KV_RANK)
    swa_w_in_p, w_o_swa = _prep_swa(swa_w_in, w_o[1::2])
    w_o_mla = w_o[0::2].astype(BF16)
    w_up = mlp_w_up.astype(BF16)
    w_dn = mlp_w_down.astype(BF16)
    sinks = swa_sinks.astype(F32) * LOG2E
    g_attn_mla, g_attn_swa = g_attn[0::2], g_attn[1::2]

    for i in range(depth):
        j = i // 2
        is_mla = i % 2 == 0
        if is_mla:
            w_a, w_kr_t, w_uq_t, w_uk, w_uv_t = mla_w
            q_t, k, v_t, qc = _mla_proj(x2, j, g_attn_mla, w_a, w_kr_t, mla_qg, mla_kvg,
                                        w_uq_t, w_uk, w_uv_t, cos_t, sin_t)
            mix = _mla_attn(q_t, k, v_t, batch, seq)
        else:
            q, k, v, qc = _swa_proj(x2, j, g_attn_swa, swa_w_in_p)
            mix = _swa_attn(slopes, sinks[j], q, k, v, pos_col_i, pos_row_i, batch, seq)
        g_final = final_norm_g.reshape(1, d) if i == depth - 1 else None
        x2 = _out_mlp(x2, mix, qc, memkv, i, w_o_mla if is_mla else w_o_swa, j, g_mlp,
                      w_up, w_dn, g_final, seq, is_mla)
    return x2.reshape(batch, seq, d)
```

```python
import functools
import math

import jax
import jax.numpy as jnp
from jax import lax
from jax.experimental import pallas as pl
from jax.experimental.pallas import tpu as pltpu

F32 = jnp.float32
BF16 = jnp.bfloat16

D_MODEL = 1024
HEAD_DIM = 64
N_MEM = 256
CROSS_WIDTH = 256
MIX_WIDTH = 768
EPS = 1e-6
NEG = -1e30
LOG2E = math.log2(math.e)
MASK_DIST = 1e32
MLA_HEADS = 12
MLA_Q_RANK = 384
MLA_KV_RANK = 256
MLA_NOPE = 64
MLA_ROPE = 32
MLA_QK = MLA_NOPE + MLA_ROPE
ROPE_THETA = 10000.0
SWA_Q_HEADS = 12
SWA_KV_HEADS = 4
SWA_GROUP = 3
WINDOW = 128
D_FF = 4 * D_MODEL

LANES = 128
QK_PAD = LANES
ROPE_HALF = MLA_ROPE // 2
BF16_ROWS = 16
V_ROWS = HEAD_DIM + BF16_ROWS

ROW_TILE = 512
ATTN_TILE = ROW_TILE
FULL_BLOCKS_PER_TRIP = 14
DIAG_BLOCKS_PER_TRIP = 4
SWA_ROWS = 512
FF_CHUNK = 1024
VMEM_LIMIT = 48 * 1024 * 1024


def _rms(x, g):
    ms = jnp.mean(x * x, axis=-1, keepdims=True)
    return x * lax.rsqrt(ms + EPS) * g


def _dot(a, b):
    return jnp.dot(a, b, preferred_element_type=F32)


def _dot_nt(a, b):
    return lax.dot_general(a, b, (((1,), (1,)), ((), ())), preferred_element_type=F32)


def _low_half():
    return lax.broadcasted_iota(jnp.int32, (1, LANES), 1) < HEAD_DIM


def _const_spec(shape):
    nd = len(shape)
    return pl.BlockSpec(shape, lambda *_: (0,) * nd, pipeline_mode=pl.Buffered(1))


def _layer_spec(stacked, layer):
    nd = stacked.ndim
    return pl.BlockSpec((None,) + stacked.shape[1:], lambda *_: (layer,) + (0,) * (nd - 1),
                        pipeline_mode=pl.Buffered(1))


def _params(*sem):
    return pltpu.CompilerParams(dimension_semantics=sem, vmem_limit_bytes=VMEM_LIMIT)


def _rope_table_kernel(pos_ref, inv_ref, cos_ref, sin_ref):
    ang = inv_ref[...] * pos_ref[...]
    cos_ref[...] = jnp.cos(ang)
    sin_ref[...] = jnp.sin(ang)


def _rope_tables(pos_row, inv_col):
    t = pos_row.shape[1]
    tn = 2048
    return pl.pallas_call(
        _rope_table_kernel,
        grid=(t // tn,),
        in_specs=[pl.BlockSpec((1, tn), lambda i: (0, i)),
                  pl.BlockSpec((ROPE_HALF, 1), lambda i: (0, 0))],
        out_specs=[pl.BlockSpec((ROPE_HALF, tn), lambda i: (0, i))] * 2,
        out_shape=[jax.ShapeDtypeStruct((ROPE_HALF, t), F32)] * 2,
        compiler_params=_params("parallel"),
        name="rope_tables",
    )(pos_row, inv_col)


def _memkv_kernel(mem_ref, g_ref, w_ref, o_ref):
    mn = _rms(mem_ref[...], g_ref[...]).astype(BF16)
    o_ref[...] = _dot(mn, w_ref[...]).astype(BF16)


def _memkv(mem2, g, w_all):
    rows = mem2.shape[0]
    depth, _, ncol = w_all.shape
    return pl.pallas_call(
        _memkv_kernel,
        grid=(rows // N_MEM, depth),
        in_specs=[pl.BlockSpec((N_MEM, D_MODEL), lambda b, l: (b, 0)),
                  _const_spec((1, D_MODEL)),
                  pl.BlockSpec((None, D_MODEL, ncol), lambda b, l: (l, 0, 0))],
        out_specs=pl.BlockSpec((N_MEM, ncol), lambda b, l: (b, l)),
        out_shape=jax.ShapeDtypeStruct((rows, depth * ncol), BF16),
        compiler_params=_params("parallel", "parallel"),
        name="mem_kv",
    )(mem2, g, w_all)


def _mla_proj_kernel(x_ref, g_ref, wa_ref, wkr_ref, qg_ref, kvg_ref, wuq_ref, wuk_ref,
                     wuv_ref, cos_ref, sin_ref, q_ref, k_ref, v_ref, qc_ref):
    tm = x_ref.shape[0]
    hn = _rms(x_ref[...], g_ref[...]).astype(BF16)
    proj = _dot(hn, wa_ref[...])
    c_q = proj[:, :MLA_Q_RANK]
    c_kv = proj[:, MLA_Q_RANK:MLA_Q_RANK + MLA_KV_RANK]
    qc_ref[...] = proj[:, MLA_Q_RANK + MLA_KV_RANK:].astype(BF16)
    cqn = _rms(c_q, qg_ref[...]).astype(BF16)
    ckvn = _rms(c_kv, kvg_ref[...]).astype(BF16)

    cos = cos_ref[...]
    sin = sin_ref[...]

    def rope(x1, x2):
        return x1 * cos - x2 * sin, x1 * sin + x2 * cos

    kr_t = _dot_nt(wkr_ref[...], hn)
    r1, r2 = rope(kr_t[:ROPE_HALF], kr_t[ROPE_HALF:])
    k_rope = jnp.concatenate(
        [jnp.zeros((MLA_NOPE, tm), F32), r1, r2,
         jnp.zeros((QK_PAD - MLA_QK, tm), F32)], axis=0).T

    k_nope = _dot(ckvn, wuk_ref[...])
    for h in range(MLA_HEADS):
        sl = slice(h * QK_PAD, (h + 1) * QK_PAD)
        k_ref[:, sl] = (k_nope[:, sl] + k_rope).astype(BF16)

    v_t = _dot_nt(wuv_ref[...], ckvn).astype(BF16)
    ones_rows = (lax.broadcasted_iota(jnp.int32, (BF16_ROWS, tm), 0) == 0).astype(BF16)
    for h in range(MLA_HEADS):
        v_ref[0, h * V_ROWS:h * V_ROWS + HEAD_DIM, :] = v_t[h * HEAD_DIM:(h + 1) * HEAD_DIM]
        v_ref[0, h * V_ROWS + HEAD_DIM:(h + 1) * V_ROWS, :] = ones_rows

    q_t = _dot_nt(wuq_ref[...], cqn)
    scale = MLA_QK ** -0.5 * LOG2E
    pad = jnp.zeros((QK_PAD - MLA_QK, tm), BF16)
    for h in range(MLA_HEADS):
        base = h * QK_PAD
        x1 = q_t[base + MLA_NOPE:base + MLA_NOPE + ROPE_HALF]
        x2 = q_t[base + MLA_NOPE + ROPE_HALF:base + MLA_QK]
        r1, r2 = rope(x1, x2)
        q_ref[0, base:base + MLA_NOPE, :] = (q_t[base:base + MLA_NOPE] * scale).astype(BF16)
        q_ref[0, base + MLA_NOPE:base + MLA_NOPE + ROPE_HALF, :] = (r1 * scale).astype(BF16)
        q_ref[0, base + MLA_NOPE + ROPE_HALF:base + MLA_QK, :] = (r2 * scale).astype(BF16)
        q_ref[0, base + MLA_QK:base + QK_PAD, :] = pad


def _mla_proj(x2, layer, g, w_a, w_kr, qg, kvg, w_uq, w_uk, w_uv, cos_t, sin_t):
    t = x2.shape[0]
    tm = ROW_TILE
    qw = MLA_HEADS * QK_PAD
    row = lambda w: pl.BlockSpec((tm, w), lambda i: (i, 0))
    col = lambda r: pl.BlockSpec((r, tm), lambda i: (0, i))
    tile = lambda r: pl.BlockSpec((1, r, tm), lambda i: (i, 0, 0))
    consts = [g, w_a, w_kr, qg, kvg, w_uq, w_uk, w_uv]
    return pl.pallas_call(
        _mla_proj_kernel,
        grid=(t // tm,),
        in_specs=[row(D_MODEL)] + [_layer_spec(c, layer) for c in consts]
                 + [col(ROPE_HALF), col(ROPE_HALF)],
        out_specs=[tile(qw), row(qw), tile(MLA_HEADS * V_ROWS), row(CROSS_WIDTH)],
        out_shape=[jax.ShapeDtypeStruct((t // tm, qw, tm), BF16),
                   jax.ShapeDtypeStruct((t, qw), BF16),
                   jax.ShapeDtypeStruct((t // tm, MLA_HEADS * V_ROWS, tm), BF16),
                   jax.ShapeDtypeStruct((t, CROSS_WIDTH), BF16)],
        compiler_params=_params("parallel"),
        name="mla_proj",
    )(x2, *consts, cos_t, sin_t)


def _mla_attn_kernel(q_ref, k_ref, v_ref, o_ref, s_ref, mb_ref, m_ref, acc_ref, bias_ref,
                     *, nt):
    t = q_ref.shape[2]
    half = t // 2
    acc_ref[...] = jnp.zeros(acc_ref.shape, F32)
    key = lax.broadcasted_iota(jnp.int32, (half, half), 0)
    qry = lax.broadcasted_iota(jnp.int32, (half, half), 1)
    bias_ref[...] = jnp.where(key <= qry, 0.0, NEG)

    def head_rows(h):
        return slice(h * QK_PAD, (h + 1) * QK_PAD)

    def scores(i, j, slot, h):
        start = pl.multiple_of(j * t, t)
        kb = k_ref[pl.ds(start, t), head_rows(h)]
        s = _dot(kb, q_ref[i, head_rows(h), :])
        s_ref[slot, h] = s
        mb_ref[slot, h] = jnp.max(s, axis=0, keepdims=True)

    def consume(i, j, slot, h):
        s = s_ref[slot, h]
        m_prev = jnp.where(j == 0, NEG, m_ref[i, h])
        m_new = jnp.maximum(m_prev, mb_ref[slot, h])
        a = jnp.exp2(m_prev - m_new)
        p = jnp.exp2(s - m_new).astype(BF16)
        rows = slice(h * V_ROWS, (h + 1) * V_ROWS)
        m_ref[i, h] = m_new
        acc_ref[i, rows, :] = a * acc_ref[i, rows, :] + _dot(v_ref[j, rows, :], p)

    def scores_diag(i, slot, h):
        start = pl.multiple_of(i * t, t)
        q = q_ref[i, head_rows(h), :]
        s_top = _dot(k_ref[pl.ds(start, half), head_rows(h)], q)
        s_bot = _dot(k_ref[pl.ds(start + half, half), head_rows(h)], q[:, half:])
        tri = bias_ref[...]
        s_left = s_top[:, :half] + tri
        s_bot = s_bot + tri
        s_ref[slot, h, :half, :half] = s_left
        s_ref[slot, h, :half, half:] = s_top[:, half:]
        s_ref[slot, h, half:, half:] = s_bot
        mb_ref[slot, h, :, :half] = jnp.max(s_left, axis=0, keepdims=True)
        mb_ref[slot, h, :, half:] = jnp.maximum(
            jnp.max(s_top[:, half:], axis=0, keepdims=True), jnp.max(s_bot, axis=0, keepdims=True))

    def consume_diag(i, slot, h):
        m_prev = jnp.where(i == 0, NEG, m_ref[i, h])
        m_new = jnp.maximum(m_prev, mb_ref[slot, h])
        a = jnp.exp2(m_prev - m_new)
        p_top = jnp.exp2(s_ref[slot, h, :half, :] - m_new).astype(BF16)
        p_bot = jnp.exp2(s_ref[slot, h, half:, half:] - m_new[:, half:]).astype(BF16)
        rows = slice(h * V_ROWS, (h + 1) * V_ROWS)
        acc = a * acc_ref[i, rows, :] + _dot(v_ref[i, rows, :half], p_top)
        acc_r = acc[:, half:] + _dot(v_ref[i, rows, half:], p_bot)
        out = slice(h * HEAD_DIM, (h + 1) * HEAD_DIM)
        o_ref[i, out, :half] = (acc[:HEAD_DIM, :half]
                                / acc[HEAD_DIM:HEAD_DIM + 1, :half]).astype(BF16)
        o_ref[i, out, half:] = (acc_r[:HEAD_DIM] / acc_r[HEAD_DIM:HEAD_DIM + 1]).astype(BF16)

    def next_full(i, j):
        wrap = j + 1 == i
        i_n = jnp.minimum(jnp.where(wrap, i + 1, i), nt - 1)
        return i_n, jnp.where(wrap, 0, j + 1)

    def full_trip(_, cur):
        for b in range(FULL_BLOCKS_PER_TRIP):
            nxt = next_full(*cur)
            for h in range(2):
                scores(*nxt, 1 - b % 2, h)
                consume(*cur, b % 2, h)
            cur = nxt
        return cur

    def diag_trip(_, i):
        for b in range(DIAG_BLOCKS_PER_TRIP):
            i_n = jnp.minimum(i + 1, nt - 1)
            for h in range(2):
                scores_diag(i_n, 1 - b % 2, h)
                consume_diag(i, b % 2, h)
            i = i + 1
        return i

    for h in range(2):
        scores(1, 0, 0, h)
    lax.fori_loop(0, nt * (nt - 1) // 2 // FULL_BLOCKS_PER_TRIP, full_trip,
                  (jnp.int32(1), jnp.int32(0)))
    for h in range(2):
        scores_diag(0, 0, h)
    lax.fori_loop(0, nt // DIAG_BLOCKS_PER_TRIP, diag_trip, jnp.int32(0))


def _mla_attn(q_t, k, v_t, batch, seq):
    t = ATTN_TILE
    nt = seq // t
    assert FULL_BLOCKS_PER_TRIP % 2 == 0 and DIAG_BLOCKS_PER_TRIP % 2 == 0
    assert (nt * (nt - 1) // 2) % FULL_BLOCKS_PER_TRIP == 0 and nt % DIAG_BLOCKS_PER_TRIP == 0
    pairs = MLA_HEADS // 2
    return pl.pallas_call(
        functools.partial(_mla_attn_kernel, nt=nt),
        grid=(batch, pairs),
        in_specs=[pl.BlockSpec((nt, 2 * QK_PAD, t), lambda b, p: (b, p, 0)),
                  pl.BlockSpec((seq, 2 * QK_PAD), lambda b, p: (b, p)),
                  pl.BlockSpec((nt, 2 * V_ROWS, t), lambda b, p: (b, p, 0))],
        out_specs=pl.BlockSpec((nt, LANES, t), lambda b, p: (b, p, 0)),
        out_shape=jax.ShapeDtypeStruct((batch * nt, MIX_WIDTH, t), BF16),
        scratch_shapes=[pltpu.VMEM((2, 2, t, t), F32), pltpu.VMEM((2, 2, 1, t), F32),
                        pltpu.VMEM((nt, 2, 1, t), F32), pltpu.VMEM((nt, 2 * V_ROWS, t), F32),
                        pltpu.VMEM((t // 2, t // 2), F32)],
        compiler_params=_params("parallel", "parallel"),
        name="mla_attn",
    )(q_t, k, v_t)


def _swa_proj_kernel(x_ref, g_ref, win_ref, q_ref, k_ref, v_ref, qc_ref):
    hn = _rms(x_ref[...], g_ref[...]).astype(BF16)
    proj = _dot(hn, win_ref[...])
    nq = SWA_Q_HEADS * HEAD_DIM
    nk = SWA_KV_HEADS * HEAD_DIM
    q_ref[...] = (proj[:, :nq] * (HEAD_DIM ** -0.5 * LOG2E)).astype(BF16)
    k_ref[...] = proj[:, nq:nq + nk].astype(BF16)
    v_ref[...] = proj[:, nq + nk:nq + 2 * nk].astype(BF16)
    qc_ref[...] = proj[:, nq + 2 * nk:].astype(BF16)


def _swa_proj(x2, layer, g, w_in):
    t = x2.shape[0]
    tm = ROW_TILE
    nk = SWA_KV_HEADS * HEAD_DIM
    row = lambda w: pl.BlockSpec((tm, w), lambda i: (i, 0))
    return pl.pallas_call(
        _swa_proj_kernel,
        grid=(t // tm,),
        in_specs=[row(D_MODEL), _layer_spec(g, layer), _layer_spec(w_in, layer)],
        out_specs=[row(MIX_WIDTH), row(nk), row(nk), row(CROSS_WIDTH)],
        out_shape=[jax.ShapeDtypeStruct((t, MIX_WIDTH), BF16),
                   jax.ShapeDtypeStruct((t, nk), BF16),
                   jax.ShapeDtypeStruct((t, nk), BF16),
                   jax.ShapeDtypeStruct((t, CROSS_WIDTH), BF16)],
        compiler_params=_params("parallel"),
        name="swa_proj",
    )(x2, g, w_in)


def _swa_attn_kernel(slope_ref, sink_ref, q_ref, ko_ref, kp_ref, vo_ref, vp_ref,
                     pq_ref, pko_ref, pkp_ref, o_ref, *, nwin):
    i = pl.program_id(1)
    lo = _low_half()
    w_ = WINDOW
    kcat = jnp.concatenate([kp_ref[...], ko_ref[...]], axis=0)
    vcat = jnp.concatenate([vp_ref[...], vo_ref[...]], axis=0)
    qi = lax.broadcasted_iota(jnp.int32, (w_, 2 * w_), 0)
    kj = lax.broadcasted_iota(jnp.int32, (w_, 2 * w_), 1)
    rel = w_ + qi - kj
    band = (rel >= 0) & (rel < w_)
    first_band = band & ((kj >= w_) | (i > 0))

    for w in range(nwin):
        rows = slice(w * w_, (w + 1) * w_)
        pos_q = pq_ref[rows, :]
        pos_prev = pkp_ref[0] if w == 0 else pko_ref[w - 1]
        pos_k = jnp.concatenate([pos_prev, pko_ref[w]], axis=1)
        dist = jnp.where(first_band if w == 0 else band,
                         (pos_q - pos_k).astype(F32), MASK_DIST)
        for pair in range(SWA_KV_HEADS // 2):
            ksl = slice(pair * LANES, (pair + 1) * LANES)
            kw = kcat[w * w_:(w + 2) * w_, ksl]
            vw = vcat[w * w_:(w + 2) * w_, ksl]
            outs = []
            for half in range(2):
                sel = lo if half == 0 else jnp.logical_not(lo)
                qs = []
                for g in range(SWA_GROUP):
                    tile = pair * SWA_GROUP + g
                    qt = q_ref[rows, tile * LANES:(tile + 1) * LANES]
                    qs.append(jnp.where(sel, qt, jnp.zeros_like(qt)))
                s3 = _dot_nt(jnp.concatenate(qs, axis=0), kw)
                ps = []
                dens = []
                for g in range(SWA_GROUP):
                    hq = (2 * pair + half) * SWA_GROUP + g
                    sink = sink_ref[hq]
                    s = s3[g * w_:(g + 1) * w_] - slope_ref[hq] * dist
                    m = jnp.maximum(jnp.max(s, axis=-1, keepdims=True), sink)
                    e = jnp.exp2(s - m)
                    dens.append(jnp.sum(e, axis=-1, keepdims=True) + jnp.exp2(sink - m))
                    ps.append(e.astype(BF16))
                o3 = _dot(jnp.concatenate(ps, axis=0), vw)
                outs.append([o3[g * w_:(g + 1) * w_] / dens[g] for g in range(SWA_GROUP)])
            for g in range(SWA_GROUP):
                tile = pair * SWA_GROUP + g
                o_ref[rows, tile * LANES:(tile + 1) * LANES] = jnp.where(
                    lo, outs[0][g], outs[1][g]).astype(BF16)


def _swa_attn(slopes, sinks, q, k, v, pos_col, pos_row, batch, seq):
    rows = SWA_ROWS
    nwin = rows // WINDOW
    nsteps = seq // rows
    nblk = seq // WINDOW
    nk = SWA_KV_HEADS * HEAD_DIM
    own = lambda b, i: (b * nsteps + i, 0)
    prev = lambda b, i: (b * nblk + jnp.maximum(i * nwin - 1, 0), 0)
    smem = pl.BlockSpec(memory_space=pltpu.SMEM)
    return pl.pallas_call(
        functools.partial(_swa_attn_kernel, nwin=nwin),
        grid=(batch, nsteps),
        in_specs=[smem, smem,
                  pl.BlockSpec((rows, MIX_WIDTH), own),
                  pl.BlockSpec((rows, nk), own),
                  pl.BlockSpec((WINDOW, nk), prev),
                  pl.BlockSpec((rows, nk), own),
                  pl.BlockSpec((WINDOW, nk), prev),
                  pl.BlockSpec((rows, 1), own),
                  pl.BlockSpec((nwin, 1, WINDOW), lambda b, i: (b * nsteps + i, 0, 0)),
                  pl.BlockSpec((1, 1, WINDOW),
                               lambda b, i: (b * nblk + jnp.maximum(i * nwin - 1, 0), 0, 0))],
        out_specs=pl.BlockSpec((rows, MIX_WIDTH), own),
        out_shape=jax.ShapeDtypeStruct((batch * seq, MIX_WIDTH), BF16),
        compiler_params=_params("parallel", "parallel"),
        name="swa_attn",
    )(slopes, sinks, q, k, k, v, v, pos_col, pos_row, pos_row)


def _out_mlp_kernel(*refs, final, mix_transposed):
    if final:
        (x_ref, mix_ref, qc_ref, km_ref, vm_ref, wo_ref, g_ref, wup_ref, wdn_ref,
         gf_ref, o_ref, acc_ref) = refs
    else:
        (x_ref, mix_ref, qc_ref, km_ref, vm_ref, wo_ref, g_ref, wup_ref, wdn_ref,
         o_ref, acc_ref) = refs
    lo = _low_half()
    cross = []
    for pair in range(CROSS_WIDTH // LANES):
        sl = slice(pair * LANES, (pair + 1) * LANES)
        qp = qc_ref[:, sl]
        kp = km_ref[:, sl]
        vp = vm_ref[:, sl]
        outs = []
        for half in range(2):
            sel = lo if half == 0 else jnp.logical_not(lo)
            s = _dot_nt(jnp.where(sel, qp, jnp.zeros_like(qp)), kp)
            e = jnp.exp(s - jnp.max(s, axis=-1, keepdims=True))
            den = jnp.sum(e, axis=-1, keepdims=True)
            outs.append(_dot(e.astype(BF16), vp) / den)
        cross.append(jnp.where(lo, outs[0], outs[1]).astype(BF16))
    if mix_transposed:
        mix = mix_ref[0].astype(F32).T.astype(BF16)
    else:
        mix = mix_ref[...]
    attn = jnp.concatenate([mix] + cross, axis=1)
    x1 = x_ref[...] + _dot(attn, wo_ref[...])
    hn = _rms(x1, g_ref[...]).astype(BF16)
    acc_ref[...] = x1
    for c in range(D_FF // FF_CHUNK):
        cols = slice(c * FF_CHUNK, (c + 1) * FF_CHUNK)
        h = jnp.maximum(_dot(hn, wup_ref[:, cols]), 0.0)
        acc_ref[...] += _dot((h * h).astype(BF16), wdn_ref[cols, :])
    if final:
        o_ref[...] = _rms(acc_ref[...], gf_ref[...])
    else:
        o_ref[...] = acc_ref[...]


def _out_mlp(x2, mix, qc, memkv, layer, w_o, wo_layer, g, w_up, w_dn, g_final, seq,
             mix_transposed):
    t = x2.shape[0]
    tm = ROW_TILE
    per_b = seq // tm
    final = g_final is not None
    row = lambda w: pl.BlockSpec((tm, w), lambda i: (i, 0))
    mix_spec = (pl.BlockSpec((1, MIX_WIDTH, tm), lambda i: (i, 0, 0)) if mix_transposed
                else row(MIX_WIDTH))
    in_specs = [row(D_MODEL), mix_spec, row(CROSS_WIDTH),
                pl.BlockSpec((N_MEM, CROSS_WIDTH), lambda i: (i // per_b, 2 * layer)),
                pl.BlockSpec((N_MEM, CROSS_WIDTH), lambda i: (i // per_b, 2 * layer + 1)),
                _layer_spec(w_o, wo_layer), _layer_spec(g, layer),
                _layer_spec(w_up, layer), _layer_spec(w_dn, layer)]
    args = [x2, mix, qc, memkv, memkv, w_o, g, w_up, w_dn]
    if final:
        in_specs.append(_const_spec(g_final.shape))
        args.append(g_final)
    return pl.pallas_call(
        functools.partial(_out_mlp_kernel, final=final, mix_transposed=mix_transposed),
        grid=(t // tm,),
        in_specs=in_specs,
        out_specs=row(D_MODEL),
        out_shape=jax.ShapeDtypeStruct((t, D_MODEL), F32),
        scratch_shapes=[pltpu.VMEM((tm, D_MODEL), F32)],
        compiler_params=_params("parallel"),
        name="out_mlp",
    )(*args)


def _prep_mla(w_in, w_uq, w_ukv):
    n = w_in.shape[0]
    n1 = MLA_Q_RANK + MLA_KV_RANK
    w_in = w_in.astype(BF16)
    w_a = jnp.concatenate([w_in[:, :, :n1], w_in[:, :, n1 + MLA_ROPE:]], axis=2)
    w_kr_t = w_in[:, :, n1:n1 + MLA_ROPE].transpose(0, 2, 1)
    w_uq_t = jnp.pad(w_uq.astype(BF16).reshape(n, MLA_Q_RANK, MLA_HEADS, MLA_QK),
                     ((0, 0), (0, 0), (0, 0), (0, QK_PAD - MLA_QK)))
    w_uq_t = w_uq_t.reshape(n, MLA_Q_RANK, MLA_HEADS * QK_PAD).transpose(0, 2, 1)
    kv = w_ukv.astype(BF16).reshape(n, MLA_KV_RANK, MLA_HEADS, 2 * HEAD_DIM)
    w_uk = jnp.pad(kv[..., :MLA_NOPE], ((0, 0), (0, 0), (0, 0), (0, QK_PAD - MLA_NOPE)))
    w_uk = w_uk.reshape(n, MLA_KV_RANK, MLA_HEADS * QK_PAD)
    w_uv_t = kv[..., MLA_NOPE:].reshape(n, MLA_KV_RANK, MIX_WIDTH).transpose(0, 2, 1)
    return w_a, w_kr_t, w_uq_t, w_uk, w_uv_t


def _pair_interleave(a, axis):
    shape = a.shape
    a = jnp.moveaxis(a, axis, 0)
    rest = a.shape[1:]
    a = a.reshape((SWA_KV_HEADS // 2, 2, SWA_GROUP, HEAD_DIM) + rest)
    a = a.transpose((0, 2, 1, 3) + tuple(range(4, 4 + len(rest))))
    a = a.reshape((MIX_WIDTH,) + rest)
    return jnp.moveaxis(a, 0, axis).reshape(shape)


def _prep_swa(w_in, w_o):
    nq = SWA_Q_HEADS * HEAD_DIM
    w_in = w_in.astype(BF16)
    w_in_p = jnp.concatenate([_pair_interleave(w_in[:, :, :nq], 2), w_in[:, :, nq:]], axis=2)
    w_o = w_o.astype(BF16)
    w_o_p = jnp.concatenate([_pair_interleave(w_o[:, :MIX_WIDTH], 1), w_o[:, MIX_WIDTH:]], axis=1)
    return w_in_p, w_o_p


def kernel(x, mem, positions, attn_norm_g, mlp_norm_g, mem_norm_g, final_norm_g,
           mla_w_in, mla_q_norm_g, mla_kv_norm_g, mla_w_uq, mla_w_ukv,
           swa_w_in, swa_sinks, w_mem_kv, w_o, mlp_w_up, mlp_w_down):
    batch, seq, d = x.shape
    depth = attn_norm_g.shape[0]
    t = batch * seq
    x2 = x.reshape(t, d)

    w_mem = w_mem_kv.astype(BF16)
    w_mem = jnp.concatenate([w_mem[:, :, :CROSS_WIDTH] * (HEAD_DIM ** -0.5),
                             w_mem[:, :, CROSS_WIDTH:]], axis=2)
    memkv = _memkv(mem.reshape(batch * N_MEM, d), mem_norm_g.reshape(1, d), w_mem)

    inv = ROPE_THETA ** (-(jnp.arange(ROPE_HALF, dtype=F32) * 2.0) / MLA_ROPE)
    cos_t, sin_t = _rope_tables(positions.astype(F32).reshape(1, t), inv.reshape(ROPE_HALF, 1))

    pos_col_i = positions.reshape(t, 1)
    pos_row_i = positions.reshape(t // WINDOW, 1, WINDOW)
    slopes = 2.0 ** (-8.0 * (jnp.arange(SWA_Q_HEADS, dtype=F32) + 1.0) / SWA_Q_HEADS) * LOG2E

    g_attn = attn_norm_g.reshape(depth, 1, d)
    g_mlp = mlp_norm_g.reshape(depth, 1, d)
    mla_w = _prep_mla(mla_w_in, mla_w_uq, mla_w_ukv)
    mla_qg = mla_q_norm_g.reshape(-1, 1, MLA_Q_RANK)
    mla_kvg = mla_kv_norm_g.reshape(-1, 1, MLA_KV_RANK)
    swa_w_in_p, w_o_swa = _prep_swa(swa_w_in, w_o[1::2])
    w_o_mla = w_o[0::2].astype(BF16)
    w_up = mlp_w_up.astype(BF16)
    w_dn = mlp_w_down.astype(BF16)
    sinks = swa_sinks.astype(F32) * LOG2E
    g_attn_mla, g_attn_swa = g_attn[0::2], g_attn[1::2]

    for i in range(depth):
        j = i // 2
        is_mla = i % 2 == 0
        if is_mla:
            w_a, w_kr_t, w_uq_t, w_uk, w_uv_t = mla_w
            q_t, k, v_t, qc = _mla_proj(x2, j, g_attn_mla, w_a, w_kr_t, mla_qg, mla_kvg,
                                        w_uq_t, w_uk, w_uv_t, cos_t, sin_t)
            mix = _mla_attn(q_t, k, v_t, batch, seq)
        else:
            q, k, v, qc = _swa_proj(x2, j, g_attn_swa, swa_w_in_p)
            mix = _swa_attn(slopes, sinks[j], q, k, v, pos_col_i, pos_row_i, batch, seq)
        g_final = final_norm_g.reshape(1, d) if i == depth - 1 else None
        x2 = _out_mlp(x2, mix, qc, memkv, i, w_o_mla if is_mla else w_o_swa, j, g_mlp,
                      w_up, w_dn, g_final, seq, is_mla)
    return x2.reshape(batch, seq, d)
```

```python
import functools
import math

import jax
import jax.numpy as jnp
from jax import lax
from jax.experimental import pallas as pl
from jax.experimental.pallas import tpu as pltpu

F32 = jnp.float32
BF16 = jnp.bfloat16

D_MODEL = 1024
HEAD_DIM = 64
N_MEM = 256
CROSS_WIDTH = 256
MIX_WIDTH = 768
EPS = 1e-6
NEG = -1e30
LOG2E = math.log2(math.e)
MASK_DIST = 1e32
MLA_HEADS = 12
MLA_Q_RANK = 384
MLA_KV_RANK = 256
MLA_NOPE = 64
MLA_ROPE = 32
MLA_QK = MLA_NOPE + MLA_ROPE
ROPE_THETA = 10000.0
SWA_Q_HEADS = 12
SWA_KV_HEADS = 4
SWA_GROUP = 3
WINDOW = 128
D_FF = 4 * D_MODEL

LANES = 128
QK_PAD = LANES
ROPE_HALF = MLA_ROPE // 2
BF16_ROWS = 16
V_ROWS = HEAD_DIM + BF16_ROWS

ROW_TILE = 512
ATTN_TILE = ROW_TILE
FULL_BLOCKS_PER_TRIP = 14
DIAG_BLOCKS_PER_TRIP = 4
SWA_ROWS = ROW_TILE
SWA_LOOKAHEAD = 2
FF_CHUNK = 1024
VMEM_LIMIT = 48 * 1024 * 1024


def _rms(x, g):
    ms = jnp.mean(x * x, axis=-1, keepdims=True)
    return x * lax.rsqrt(ms + EPS) * g


def _dot(a, b):
    return jnp.dot(a, b, preferred_element_type=F32)


def _dot_nt(a, b):
    return lax.dot_general(a, b, (((1,), (1,)), ((), ())), preferred_element_type=F32)


def _low_half():
    return lax.broadcasted_iota(jnp.int32, (1, LANES), 1) < HEAD_DIM


def _const_spec(shape):
    nd = len(shape)
    return pl.BlockSpec(shape, lambda *_: (0,) * nd, pipeline_mode=pl.Buffered(1))


def _layer_spec(stacked, layer):
    nd = stacked.ndim
    return pl.BlockSpec((None,) + stacked.shape[1:], lambda *_: (layer,) + (0,) * (nd - 1),
                        pipeline_mode=pl.Buffered(1))


def _params(*sem):
    return pltpu.CompilerParams(dimension_semantics=sem, vmem_limit_bytes=VMEM_LIMIT)


def _rope_table_kernel(pos_ref, inv_ref, cos_ref, sin_ref):
    ang = inv_ref[...] * pos_ref[...]
    cos_ref[...] = jnp.cos(ang)
    sin_ref[...] = jnp.sin(ang)


def _rope_tables(pos_row, inv_col):
    t = pos_row.shape[1]
    tn = 2048
    return pl.pallas_call(
        _rope_table_kernel,
        grid=(t // tn,),
        in_specs=[pl.BlockSpec((1, tn), lambda i: (0, i)),
                  pl.BlockSpec((ROPE_HALF, 1), lambda i: (0, 0))],
        out_specs=[pl.BlockSpec((ROPE_HALF, tn), lambda i: (0, i))] * 2,
        out_shape=[jax.ShapeDtypeStruct((ROPE_HALF, t), F32)] * 2,
        compiler_params=_params("parallel"),
        name="rope_tables",
    )(pos_row, inv_col)


def _memkv_kernel(mem_ref, g_ref, w_ref, o_ref):
    mn = _rms(mem_ref[...], g_ref[...]).astype(BF16)
    o_ref[...] = _dot(mn, w_ref[...]).astype(BF16)


def _memkv(mem2, g, w_all):
    rows = mem2.shape[0]
    depth, _, ncol = w_all.shape
    return pl.pallas_call(
        _memkv_kernel,
        grid=(rows // N_MEM, depth),
        in_specs=[pl.BlockSpec((N_MEM, D_MODEL), lambda b, l: (b, 0)),
                  _const_spec((1, D_MODEL)),
                  pl.BlockSpec((None, D_MODEL, ncol), lambda b, l: (l, 0, 0))],
        out_specs=pl.BlockSpec((N_MEM, ncol), lambda b, l: (b, l)),
        out_shape=jax.ShapeDtypeStruct((rows, depth * ncol), BF16),
        compiler_params=_params("parallel", "parallel"),
        name="mem_kv",
    )(mem2, g, w_all)


def _mla_proj_kernel(x_ref, g_ref, wa_ref, wkr_ref, qg_ref, kvg_ref, wuq_ref, wuk_ref,
                     wuv_ref, cos_ref, sin_ref, q_ref, k_ref, v_ref, qc_ref):
    tm = x_ref.shape[0]
    hn = _rms(x_ref[...], g_ref[...]).astype(BF16)
    proj = _dot(hn, wa_ref[...])
    c_q = proj[:, :MLA_Q_RANK]
    c_kv = proj[:, MLA_Q_RANK:MLA_Q_RANK + MLA_KV_RANK]
    qc_ref[...] = proj[:, MLA_Q_RANK + MLA_KV_RANK:].astype(BF16)
    cqn = _rms(c_q, qg_ref[...]).astype(BF16)
    ckvn = _rms(c_kv, kvg_ref[...]).astype(BF16)

    cos = cos_ref[...]
    sin = sin_ref[...]

    def rope(x1, x2):
        return x1 * cos - x2 * sin, x1 * sin + x2 * cos

    kr_t = _dot_nt(wkr_ref[...], hn)
    r1, r2 = rope(kr_t[:ROPE_HALF], kr_t[ROPE_HALF:])
    k_rope = jnp.concatenate(
        [jnp.zeros((MLA_NOPE, tm), F32), r1, r2,
         jnp.zeros((QK_PAD - MLA_QK, tm), F32)], axis=0).T

    k_nope = _dot(ckvn, wuk_ref[...])
    for h in range(MLA_HEADS):
        sl = slice(h * QK_PAD, (h + 1) * QK_PAD)
        k_ref[:, sl] = (k_nope[:, sl] + k_rope).astype(BF16)

    v_t = _dot_nt(wuv_ref[...], ckvn).astype(BF16)
    ones_rows = (lax.broadcasted_iota(jnp.int32, (BF16_ROWS, tm), 0) == 0).astype(BF16)
    for h in range(MLA_HEADS):
        v_ref[0, h * V_ROWS:h * V_ROWS + HEAD_DIM, :] = v_t[h * HEAD_DIM:(h + 1) * HEAD_DIM]
        v_ref[0, h * V_ROWS + HEAD_DIM:(h + 1) * V_ROWS, :] = ones_rows

    q_t = _dot_nt(wuq_ref[...], cqn)
    scale = MLA_QK ** -0.5 * LOG2E
    pad = jnp.zeros((QK_PAD - MLA_QK, tm), BF16)
    for h in range(MLA_HEADS):
        base = h * QK_PAD
        x1 = q_t[base + MLA_NOPE:base + MLA_NOPE + ROPE_HALF]
        x2 = q_t[base + MLA_NOPE + ROPE_HALF:base + MLA_QK]
        r1, r2 = rope(x1, x2)
        q_ref[0, base:base + MLA_NOPE, :] = (q_t[base:base + MLA_NOPE] * scale).astype(BF16)
        q_ref[0, base + MLA_NOPE:base + MLA_NOPE + ROPE_HALF, :] = (r1 * scale).astype(BF16)
        q_ref[0, base + MLA_NOPE + ROPE_HALF:base + MLA_QK, :] = (r2 * scale).astype(BF16)
        q_ref[0, base + MLA_QK:base + QK_PAD, :] = pad


def _mla_proj(x2, layer, g, w_a, w_kr, qg, kvg, w_uq, w_uk, w_uv, cos_t, sin_t):
    t = x2.shape[0]
    tm = ROW_TILE
    qw = MLA_HEADS * QK_PAD
    row = lambda w: pl.BlockSpec((tm, w), lambda i: (i, 0))
    col = lambda r: pl.BlockSpec((r, tm), lambda i: (0, i))
    tile = lambda r: pl.BlockSpec((1, r, tm), lambda i: (i, 0, 0))
    consts = [g, w_a, w_kr, qg, kvg, w_uq, w_uk, w_uv]
    return pl.pallas_call(
        _mla_proj_kernel,
        grid=(t // tm,),
        in_specs=[row(D_MODEL)] + [_layer_spec(c, layer) for c in consts]
                 + [col(ROPE_HALF), col(ROPE_HALF)],
        out_specs=[tile(qw), row(qw), tile(MLA_HEADS * V_ROWS), row(CROSS_WIDTH)],
        out_shape=[jax.ShapeDtypeStruct((t // tm, qw, tm), BF16),
                   jax.ShapeDtypeStruct((t, qw), BF16),
                   jax.ShapeDtypeStruct((t // tm, MLA_HEADS * V_ROWS, tm), BF16),
                   jax.ShapeDtypeStruct((t, CROSS_WIDTH), BF16)],
        compiler_params=_params("parallel"),
        name="mla_proj",
    )(x2, *consts, cos_t, sin_t)


def _mla_attn_kernel(q_ref, k_ref, v_ref, o_ref, s_ref, mb_ref, m_ref, acc_ref, bias_ref,
                     *, nt):
    t = q_ref.shape[2]
    half = t // 2
    acc_ref[...] = jnp.zeros(acc_ref.shape, F32)
    key = lax.broadcasted_iota(jnp.int32, (half, half), 0)
    qry = lax.broadcasted_iota(jnp.int32, (half, half), 1)
    bias_ref[...] = jnp.where(key <= qry, 0.0, NEG)

    def head_rows(h):
        return slice(h * QK_PAD, (h + 1) * QK_PAD)

    def scores(i, j, slot, h):
        start = pl.multiple_of(j * t, t)
        kb = k_ref[pl.ds(start, t), head_rows(h)]
        s = _dot(kb, q_ref[i, head_rows(h), :])
        s_ref[slot, h] = s
        mb_ref[slot, h] = jnp.max(s, axis=0, keepdims=True)

    def consume(i, j, slot, h):
        s = s_ref[slot, h]
        m_prev = jnp.where(j == 0, NEG, m_ref[i, h])
        m_new = jnp.maximum(m_prev, mb_ref[slot, h])
        a = jnp.exp2(m_prev - m_new)
        p = jnp.exp2(s - m_new).astype(BF16)
        rows = slice(h * V_ROWS, (h + 1) * V_ROWS)
        m_ref[i, h] = m_new
        acc_ref[i, rows, :] = a * acc_ref[i, rows, :] + _dot(v_ref[j, rows, :], p)

    def scores_diag(i, slot, h):
        start = pl.multiple_of(i * t, t)
        q = q_ref[i, head_rows(h), :]
        s_top = _dot(k_ref[pl.ds(start, half), head_rows(h)], q)
        s_bot = _dot(k_ref[pl.ds(start + half, half), head_rows(h)], q[:, half:])
        tri = bias_ref[...]
        s_left = s_top[:, :half] + tri
        s_bot = s_bot + tri
        s_ref[slot, h, :half, :half] = s_left
        s_ref[slot, h, :half, half:] = s_top[:, half:]
        s_ref[slot, h, half:, half:] = s_bot
        mb_ref[slot, h, :, :half] = jnp.max(s_left, axis=0, keepdims=True)
        mb_ref[slot, h, :, half:] = jnp.maximum(
            jnp.max(s_top[:, half:], axis=0, keepdims=True), jnp.max(s_bot, axis=0, keepdims=True))

    def consume_diag(i, slot, h):
        m_prev = jnp.where(i == 0, NEG, m_ref[i, h])
        m_new = jnp.maximum(m_prev, mb_ref[slot, h])
        a = jnp.exp2(m_prev - m_new)
        p_top = jnp.exp2(s_ref[slot, h, :half, :] - m_new).astype(BF16)
        p_bot = jnp.exp2(s_ref[slot, h, half:, half:] - m_new[:, half:]).astype(BF16)
        rows = slice(h * V_ROWS, (h + 1) * V_ROWS)
        acc = a * acc_ref[i, rows, :] + _dot(v_ref[i, rows, :half], p_top)
        acc_r = acc[:, half:] + _dot(v_ref[i, rows, half:], p_bot)
        out = slice(h * HEAD_DIM, (h + 1) * HEAD_DIM)
        o_ref[i, out, :half] = (acc[:HEAD_DIM, :half]
                                / acc[HEAD_DIM:HEAD_DIM + 1, :half]).astype(BF16)
        o_ref[i, out, half:] = (acc_r[:HEAD_DIM] / acc_r[HEAD_DIM:HEAD_DIM + 1]).astype(BF16)

    def next_full(i, j):
        wrap = j + 1 == i
        i_n = jnp.minimum(jnp.where(wrap, i + 1, i), nt - 1)
        return i_n, jnp.where(wrap, 0, j + 1)

    def full_trip(_, cur):
        for b in range(FULL_BLOCKS_PER_TRIP):
            nxt = next_full(*cur)
            for h in range(2):
                scores(*nxt, 1 - b % 2, h)
                consume(*cur, b % 2, h)
            cur = nxt
        return cur

    def diag_trip(_, i):
        for b in range(DIAG_BLOCKS_PER_TRIP):
            i_n = jnp.minimum(i + 1, nt - 1)
            for h in range(2):
                scores_diag(i_n, 1 - b % 2, h)
                consume_diag(i, b % 2, h)
            i = i + 1
        return i

    for h in range(2):
        scores(1, 0, 0, h)
    lax.fori_loop(0, nt * (nt - 1) // 2 // FULL_BLOCKS_PER_TRIP, full_trip,
                  (jnp.int32(1), jnp.int32(0)))
    for h in range(2):
        scores_diag(0, 0, h)
    lax.fori_loop(0, nt // DIAG_BLOCKS_PER_TRIP, diag_trip, jnp.int32(0))


def _mla_attn(q_t, k, v_t, batch, seq):
    t = ATTN_TILE
    nt = seq // t
    assert FULL_BLOCKS_PER_TRIP % 2 == 0 and DIAG_BLOCKS_PER_TRIP % 2 == 0
    assert (nt * (nt - 1) // 2) % FULL_BLOCKS_PER_TRIP == 0 and nt % DIAG_BLOCKS_PER_TRIP == 0
    pairs = MLA_HEADS // 2
    return pl.pallas_call(
        functools.partial(_mla_attn_kernel, nt=nt),
        grid=(batch, pairs),
        in_specs=[pl.BlockSpec((nt, 2 * QK_PAD, t), lambda b, p: (b, p, 0)),
                  pl.BlockSpec((seq, 2 * QK_PAD), lambda b, p: (b, p)),
                  pl.BlockSpec((nt, 2 * V_ROWS, t), lambda b, p: (b, p, 0))],
        out_specs=pl.BlockSpec((nt, LANES, t), lambda b, p: (b, p, 0)),
        out_shape=jax.ShapeDtypeStruct((batch * nt, MIX_WIDTH, t), BF16),
        scratch_shapes=[pltpu.VMEM((2, 2, t, t), F32), pltpu.VMEM((2, 2, 1, t), F32),
                        pltpu.VMEM((nt, 2, 1, t), F32), pltpu.VMEM((nt, 2 * V_ROWS, t), F32),
                        pltpu.VMEM((t // 2, t // 2), F32)],
        compiler_params=_params("parallel", "parallel"),
        name="mla_attn",
    )(q_t, k, v_t)


def _swa_proj_kernel(x_ref, g_ref, wq_ref, wkc_ref, wv_ref, q_ref, k_ref, v_ref, qc_ref):
    tm = x_ref.shape[0]
    nk = SWA_KV_HEADS * HEAD_DIM
    hn = _rms(x_ref[...], g_ref[...]).astype(BF16)
    q_ref[0] = (_dot_nt(wq_ref[...], hn) * (HEAD_DIM ** -0.5 * LOG2E)).astype(BF16)
    kc = _dot(hn, wkc_ref[...])
    k_ref[...] = kc[:, :nk].astype(BF16)
    qc_ref[...] = kc[:, nk:].astype(BF16)
    v_t = _dot_nt(wv_ref[...], hn).astype(BF16)
    ones_rows = (lax.broadcasted_iota(jnp.int32, (BF16_ROWS, WINDOW), 0) == 0).astype(BF16)
    for w in range(tm // WINDOW):
        lanes = slice(w * WINDOW, (w + 1) * WINDOW)
        for h in range(SWA_KV_HEADS):
            v_ref[w, h * V_ROWS:h * V_ROWS + HEAD_DIM, :] = v_t[h * HEAD_DIM:(h + 1) * HEAD_DIM,
                                                                lanes]
            v_ref[w, h * V_ROWS + HEAD_DIM:(h + 1) * V_ROWS, :] = ones_rows


def _swa_proj(x2, layer, g, w_q_t, w_kc, w_v_t):
    t = x2.shape[0]
    tm = ROW_TILE
    nk = SWA_KV_HEADS * HEAD_DIM
    nwin = tm // WINDOW
    row = lambda w: pl.BlockSpec((tm, w), lambda i: (i, 0))
    consts = [g, w_q_t, w_kc, w_v_t]
    return pl.pallas_call(
        _swa_proj_kernel,
        grid=(t // tm,),
        in_specs=[row(D_MODEL)] + [_layer_spec(c, layer) for c in consts],
        out_specs=[pl.BlockSpec((1, MIX_WIDTH, tm), lambda i: (i, 0, 0)), row(nk),
                   pl.BlockSpec((nwin, SWA_KV_HEADS * V_ROWS, WINDOW), lambda i: (i, 0, 0)),
                   row(CROSS_WIDTH)],
        out_shape=[jax.ShapeDtypeStruct((t // tm, MIX_WIDTH, tm), BF16),
                   jax.ShapeDtypeStruct((t, nk), BF16),
                   jax.ShapeDtypeStruct((t // WINDOW, SWA_KV_HEADS * V_ROWS, WINDOW), BF16),
                   jax.ShapeDtypeStruct((t, CROSS_WIDTH), BF16)],
        compiler_params=_params("parallel"),
        name="swa_proj",
    )(x2, *consts)


def _swa_attn_kernel(slope_ref, sink_ref, q_ref, ko_ref, kp_ref, vo_ref, vp_ref,
                     pqo_ref, pko_ref, pkp_ref, o_ref, *, nwin):
    i = pl.program_id(1)
    w_ = WINDOW
    kcat = jnp.concatenate([kp_ref[...], ko_ref[...]], axis=0)
    pos_k = jnp.concatenate([pkp_ref[...], pko_ref[...]], axis=0)
    kj = lax.broadcasted_iota(jnp.int32, (2 * w_, w_), 0)
    qi = lax.broadcasted_iota(jnp.int32, (2 * w_, w_), 1)
    rel = w_ + qi - kj
    band = (rel >= 0) & (rel < w_)
    first_band = band & ((kj >= w_) | (i > 0))
    zeros = jnp.zeros((HEAD_DIM, w_), BF16)

    def window_keys(w):
        return slice(w * w_, (w + 2) * w_)

    def scores(w, kh):
        k_pair = kcat[window_keys(w), (kh // 2) * LANES:(kh // 2 + 1) * LANES]
        qs = []
        for g in range(SWA_GROUP):
            hq = kh * SWA_GROUP + g
            qh = q_ref[0, hq * HEAD_DIM:(hq + 1) * HEAD_DIM, w * w_:(w + 1) * w_]
            qs.append(jnp.concatenate([qh, zeros] if kh % 2 == 0 else [zeros, qh], axis=0))
        return _dot(k_pair, jnp.concatenate(qs, axis=1))

    dists = {}

    def masked_dist(w):
        if w not in dists:
            dists[w] = jnp.where(first_band if w == 0 else band,
                                 (pqo_ref[w] - pos_k[window_keys(w)]).astype(F32), MASK_DIST)
        return dists[w]

    def finish(w, kh, s3):
        dist = masked_dist(w)
        v_prev = vp_ref[0] if w == 0 else vo_ref[w - 1]
        rows = slice(kh * V_ROWS, (kh + 1) * V_ROWS)
        v_win = jnp.concatenate([v_prev[rows], vo_ref[w, rows, :]], axis=1)
        ps = []
        sink_terms = []
        for g in range(SWA_GROUP):
            hq = kh * SWA_GROUP + g
            sink = sink_ref[hq]
            s = s3[:, g * w_:(g + 1) * w_] - slope_ref[hq] * dist
            m = jnp.maximum(jnp.max(s, axis=0, keepdims=True), sink)
            ps.append(jnp.exp2(s - m).astype(BF16))
            sink_terms.append(jnp.exp2(sink - m))
        o3 = _dot(v_win, jnp.concatenate(ps, axis=1))
        for g in range(SWA_GROUP):
            hq = kh * SWA_GROUP + g
            og = o3[:, g * w_:(g + 1) * w_]
            o_ref[0, hq * HEAD_DIM:(hq + 1) * HEAD_DIM, w * w_:(w + 1) * w_] = (
                og[:HEAD_DIM] / (og[HEAD_DIM:HEAD_DIM + 1] + sink_terms[g])).astype(BF16)

    units = [(w, kh) for w in range(nwin) for kh in range(SWA_KV_HEADS)]
    pending = {}
    for idx in range(len(units) + SWA_LOOKAHEAD):
        if idx < len(units):
            pending[idx] = scores(*units[idx])
        if idx >= SWA_LOOKAHEAD:
            done = idx - SWA_LOOKAHEAD
            finish(*units[done], pending.pop(done))


def _swa_attn(slopes, sinks, q_t, k, v_t, pos_col, pos_row, batch, seq):
    rows = SWA_ROWS
    nwin = rows // WINDOW
    nsteps = seq // rows
    nblk = seq // WINDOW
    nk = SWA_KV_HEADS * HEAD_DIM
    vr = SWA_KV_HEADS * V_ROWS
    own = lambda b, i: (b * nsteps + i, 0)
    own3 = lambda b, i: (b * nsteps + i, 0, 0)
    prev = lambda b, i: (b * nblk + jnp.maximum(i * nwin - 1, 0), 0)
    prev3 = lambda b, i: (b * nblk + jnp.maximum(i * nwin - 1, 0), 0, 0)
    smem = pl.BlockSpec(memory_space=pltpu.SMEM)
    return pl.pallas_call(
        functools.partial(_swa_attn_kernel, nwin=nwin),
        grid=(batch, nsteps),
        in_specs=[smem, smem,
                  pl.BlockSpec((1, MIX_WIDTH, rows), own3),
                  pl.BlockSpec((rows, nk), own),
                  pl.BlockSpec((WINDOW, nk), prev),
                  pl.BlockSpec((nwin, vr, WINDOW), own3),
                  pl.BlockSpec((1, vr, WINDOW), prev3),
                  pl.BlockSpec((nwin, 1, WINDOW), own3),
                  pl.BlockSpec((rows, 1), own),
                  pl.BlockSpec((WINDOW, 1), prev)],
        out_specs=pl.BlockSpec((1, MIX_WIDTH, rows), own3),
        out_shape=jax.ShapeDtypeStruct((batch * nsteps, MIX_WIDTH, rows), BF16),
        compiler_params=_params("parallel", "parallel"),
        name="swa_attn",
    )(slopes, sinks, q_t, k, k, v_t, v_t, pos_row, pos_col, pos_col)


def _out_mlp_kernel(*refs, final):
    if final:
        (x_ref, mix_ref, qc_ref, km_ref, vm_ref, wo_ref, g_ref, wup_ref, wdn_ref,
         gf_ref, o_ref, acc_ref) = refs
    else:
        (x_ref, mix_ref, qc_ref, km_ref, vm_ref, wo_ref, g_ref, wup_ref, wdn_ref,
         o_ref, acc_ref) = refs
    lo = _low_half()
    cross = []
    for pair in range(CROSS_WIDTH // LANES):
        sl = slice(pair * LANES, (pair + 1) * LANES)
        qp = qc_ref[:, sl]
        kp = km_ref[:, sl]
        vp = vm_ref[:, sl]
        outs = []
        for half in range(2):
            sel = lo if half == 0 else jnp.logical_not(lo)
            s = _dot_nt(jnp.where(sel, qp, jnp.zeros_like(qp)), kp)
            e = jnp.exp(s - jnp.max(s, axis=-1, keepdims=True))
            den = jnp.sum(e, axis=-1, keepdims=True)
            outs.append(_dot(e.astype(BF16), vp) / den)
        cross.append(jnp.where(lo, outs[0], outs[1]).astype(BF16))
    mix = mix_ref[0].astype(F32).T.astype(BF16)
    attn = jnp.concatenate([mix] + cross, axis=1)
    x1 = x_ref[...] + _dot(attn, wo_ref[...])
    hn = _rms(x1, g_ref[...]).astype(BF16)
    acc_ref[...] = x1
    for c in range(D_FF // FF_CHUNK):
        cols = slice(c * FF_CHUNK, (c + 1) * FF_CHUNK)
        h = jnp.maximum(_dot(hn, wup_ref[:, cols]), 0.0)
        acc_ref[...] += _dot((h * h).astype(BF16), wdn_ref[cols, :])
    if final:
        o_ref[...] = _rms(acc_ref[...], gf_ref[...])
    else:
        o_ref[...] = acc_ref[...]


def _out_mlp(x2, mix, qc, memkv, layer, w_o, g, w_up, w_dn, g_final, seq):
    t = x2.shape[0]
    tm = ROW_TILE
    per_b = seq // tm
    final = g_final is not None
    row = lambda w: pl.BlockSpec((tm, w), lambda i: (i, 0))
    in_specs = [row(D_MODEL), pl.BlockSpec((1, MIX_WIDTH, tm), lambda i: (i, 0, 0)),
                row(CROSS_WIDTH),
                pl.BlockSpec((N_MEM, CROSS_WIDTH), lambda i: (i // per_b, 2 * layer)),
                pl.BlockSpec((N_MEM, CROSS_WIDTH), lambda i: (i // per_b, 2 * layer + 1)),
                _layer_spec(w_o, layer), _layer_spec(g, layer),
                _layer_spec(w_up, layer), _layer_spec(w_dn, layer)]
    args = [x2, mix, qc, memkv, memkv, w_o, g, w_up, w_dn]
    if final:
        in_specs.append(_const_spec(g_final.shape))
        args.append(g_final)
    return pl.pallas_call(
        functools.partial(_out_mlp_kernel, final=final),
        grid=(t // tm,),
        in_specs=in_specs,
        out_specs=row(D_MODEL),
        out_shape=jax.ShapeDtypeStruct((t, D_MODEL), F32),
        scratch_shapes=[pltpu.VMEM((tm, D_MODEL), F32)],
        compiler_params=_params("parallel"),
        name="out_mlp",
    )(*args)


def _prep_mla(w_in, w_uq, w_ukv):
    n = w_in.shape[0]
    n1 = MLA_Q_RANK + MLA_KV_RANK
    w_in = w_in.astype(BF16)
    w_a = jnp.concatenate([w_in[:, :, :n1], w_in[:, :, n1 + MLA_ROPE:]], axis=2)
    w_kr_t = w_in[:, :, n1:n1 + MLA_ROPE].transpose(0, 2, 1)
    w_uq_t = jnp.pad(w_uq.astype(BF16).reshape(n, MLA_Q_RANK, MLA_HEADS, MLA_QK),
                     ((0, 0), (0, 0), (0, 0), (0, QK_PAD - MLA_QK)))
    w_uq_t = w_uq_t.reshape(n, MLA_Q_RANK, MLA_HEADS * QK_PAD).transpose(0, 2, 1)
    kv = w_ukv.astype(BF16).reshape(n, MLA_KV_RANK, MLA_HEADS, 2 * HEAD_DIM)
    w_uk = jnp.pad(kv[..., :MLA_NOPE], ((0, 0), (0, 0), (0, 0), (0, QK_PAD - MLA_NOPE)))
    w_uk = w_uk.reshape(n, MLA_KV_RANK, MLA_HEADS * QK_PAD)
    w_uv_t = kv[..., MLA_NOPE:].reshape(n, MLA_KV_RANK, MIX_WIDTH).transpose(0, 2, 1)
    return w_a, w_kr_t, w_uq_t, w_uk, w_uv_t


def _prep_swa(w_in):
    nq = SWA_Q_HEADS * HEAD_DIM
    nk = SWA_KV_HEADS * HEAD_DIM
    w_in = w_in.astype(BF16)
    w_q_t = w_in[:, :, :nq].transpose(0, 2, 1)
    w_kc = jnp.concatenate([w_in[:, :, nq:nq + nk], w_in[:, :, nq + 2 * nk:]], axis=2)
    w_v_t = w_in[:, :, nq + nk:nq + 2 * nk].transpose(0, 2, 1)
    return w_q_t, w_kc, w_v_t


def kernel(x, mem, positions, attn_norm_g, mlp_norm_g, mem_norm_g, final_norm_g,
           mla_w_in, mla_q_norm_g, mla_kv_norm_g, mla_w_uq, mla_w_ukv,
           swa_w_in, swa_sinks, w_mem_kv, w_o, mlp_w_up, mlp_w_down):
    batch, seq, d = x.shape
    depth = attn_norm_g.shape[0]
    t = batch * seq
    x2 = x.reshape(t, d)

    w_mem = w_mem_kv.astype(BF16)
    w_mem = jnp.concatenate([w_mem[:, :, :CROSS_WIDTH] * (HEAD_DIM ** -0.5),
                             w_mem[:, :, CROSS_WIDTH:]], axis=2)
    memkv = _memkv(mem.reshape(batch * N_MEM, d), mem_norm_g.reshape(1, d), w_mem)

    inv = ROPE_THETA ** (-(jnp.arange(ROPE_HALF, dtype=F32) * 2.0) / MLA_ROPE)
    cos_t, sin_t = _rope_tables(positions.astype(F32).reshape(1, t), inv.reshape(ROPE_HALF, 1))

    pos_col_i = positions.reshape(t, 1)
    pos_row_i = positions.reshape(t // WINDOW, 1, WINDOW)
    slopes = 2.0 ** (-8.0 * (jnp.arange(SWA_Q_HEADS, dtype=F32) + 1.0) / SWA_Q_HEADS) * LOG2E

    g_attn = attn_norm_g.reshape(depth, 1, d)
    g_mlp = mlp_norm_g.reshape(depth, 1, d)
    mla_w = _prep_mla(mla_w_in, mla_w_uq, mla_w_ukv)
    mla_qg = mla_q_norm_g.reshape(-1, 1, MLA_Q_RANK)
    mla_kvg = mla_kv_norm_g.reshape(-1, 1, MLA_KV_RANK)
    swa_w = _prep_swa(swa_w_in)
    w_o_b = w_o.astype(BF16)
    w_up = mlp_w_up.astype(BF16)
    w_dn = mlp_w_down.astype(BF16)
    sinks = swa_sinks.astype(F32) * LOG2E
    g_attn_mla, g_attn_swa = g_attn[0::2], g_attn[1::2]

    for i in range(depth):
        j = i // 2
        is_mla = i % 2 == 0
        if is_mla:
            w_a, w_kr_t, w_uq_t, w_uk, w_uv_t = mla_w
            q_t, k, v_t, qc = _mla_proj(x2, j, g_attn_mla, w_a, w_kr_t, mla_qg, mla_kvg,
                                        w_uq_t, w_uk, w_uv_t, cos_t, sin_t)
            mix = _mla_attn(q_t, k, v_t, batch, seq)
        else:
            q_t, k, v_t, qc = _swa_proj(x2, j, g_attn_swa, *swa_w)
            mix = _swa_attn(slopes, sinks[j], q_t, k, v_t, pos_col_i, pos_row_i, batch, seq)
        g_final = final_norm_g.reshape(1, d) if i == depth - 1 else None
        x2 = _out_mlp(x2, mix, qc, memkv, i, w_o_b, g_mlp, w_up, w_dn, g_final, seq)
    return x2.reshape(batch, seq, d)
```

```python
import functools
import math

import jax
import jax.numpy as jnp
from jax import lax
from jax.experimental import pallas as pl
from jax.experimental.pallas import tpu as pltpu

F32 = jnp.float32
BF16 = jnp.bfloat16

D_MODEL = 1024
HEAD_DIM = 64
N_MEM = 256
CROSS_WIDTH = 256
MIX_WIDTH = 768
EPS = 1e-6
NEG = -1e30
LOG2E = math.log2(math.e)
MASK_DIST = 1e32
MLA_HEADS = 12
MLA_Q_RANK = 384
MLA_KV_RANK = 256
MLA_NOPE = 64
MLA_ROPE = 32
MLA_QK = MLA_NOPE + MLA_ROPE
ROPE_THETA = 10000.0
SWA_Q_HEADS = 12
SWA_KV_HEADS = 4
SWA_GROUP = 3
WINDOW = 128
D_FF = 4 * D_MODEL

LANES = 128
QK_PAD = LANES
ROPE_HALF = MLA_ROPE // 2
BF16_ROWS = 16
V_ROWS = HEAD_DIM + BF16_ROWS

ROW_TILE = 512
ATTN_TILE = ROW_TILE
FULL_BLOCKS_PER_TRIP = 14
DIAG_BLOCKS_PER_TRIP = 4
SWA_ROWS = ROW_TILE
SWA_LOOKAHEAD = 2
FF_CHUNK = 1024
VMEM_LIMIT = 48 * 1024 * 1024


def _rms(x, g):
    ms = jnp.mean(x * x, axis=-1, keepdims=True)
    return x * lax.rsqrt(ms + EPS) * g


def _dot(a, b):
    return jnp.dot(a, b, preferred_element_type=F32)


def _dot_nt(a, b):
    return lax.dot_general(a, b, (((1,), (1,)), ((), ())), preferred_element_type=F32)


def _low_half():
    return lax.broadcasted_iota(jnp.int32, (1, LANES), 1) < HEAD_DIM


def _const_spec(shape):
    nd = len(shape)
    return pl.BlockSpec(shape, lambda *_: (0,) * nd, pipeline_mode=pl.Buffered(1))


def _layer_spec(stacked, layer):
    nd = stacked.ndim
    return pl.BlockSpec((None,) + stacked.shape[1:], lambda *_: (layer,) + (0,) * (nd - 1),
                        pipeline_mode=pl.Buffered(1))


def _params(*sem):
    return pltpu.CompilerParams(dimension_semantics=sem, vmem_limit_bytes=VMEM_LIMIT)


def _rope_table_kernel(pos_ref, inv_ref, cos_ref, sin_ref):
    ang = inv_ref[...] * pos_ref[...]
    cos_ref[...] = jnp.cos(ang)
    sin_ref[...] = jnp.sin(ang)


def _rope_tables(pos_row, inv_col):
    t = pos_row.shape[1]
    tn = 2048
    return pl.pallas_call(
        _rope_table_kernel,
        grid=(t // tn,),
        in_specs=[pl.BlockSpec((1, tn), lambda i: (0, i)),
                  pl.BlockSpec((ROPE_HALF, 1), lambda i: (0, 0))],
        out_specs=[pl.BlockSpec((ROPE_HALF, tn), lambda i: (0, i))] * 2,
        out_shape=[jax.ShapeDtypeStruct((ROPE_HALF, t), F32)] * 2,
        compiler_params=_params("parallel"),
        name="rope_tables",
    )(pos_row, inv_col)


def _memkv_kernel(mem_ref, g_ref, w_ref, o_ref):
    mn = _rms(mem_ref[...], g_ref[...]).astype(BF16)
    o_ref[...] = _dot(mn, w_ref[...]).astype(BF16)


def _memkv(mem2, g, w_all):
    rows = mem2.shape[0]
    depth, _, ncol = w_all.shape
    return pl.pallas_call(
        _memkv_kernel,
        grid=(rows // N_MEM, depth),
        in_specs=[pl.BlockSpec((N_MEM, D_MODEL), lambda b, l: (b, 0)),
                  _const_spec((1, D_MODEL)),
                  pl.BlockSpec((None, D_MODEL, ncol), lambda b, l: (l, 0, 0))],
        out_specs=pl.BlockSpec((N_MEM, ncol), lambda b, l: (b, l)),
        out_shape=jax.ShapeDtypeStruct((rows, depth * ncol), BF16),
        compiler_params=_params("parallel", "parallel"),
        name="mem_kv",
    )(mem2, g, w_all)


def _mla_proj_kernel(x_ref, g_ref, wa_ref, wkr_ref, qg_ref, kvg_ref, wuq_ref, wuk_ref,
                     wuv_ref, cos_ref, sin_ref, q_ref, k_ref, v_ref, qc_ref):
    tm = x_ref.shape[0]
    half = tm // 2
    scale = MLA_QK ** -0.5 * LOG2E

    def compress(rows):
        hn = _rms(x_ref[rows, :], g_ref[...]).astype(BF16)
        return hn, _dot(hn, wa_ref[...])

    def expand(rows, hn, proj):
        c_q = proj[:, :MLA_Q_RANK]
        c_kv = proj[:, MLA_Q_RANK:MLA_Q_RANK + MLA_KV_RANK]
        qc_ref[rows, :] = proj[:, MLA_Q_RANK + MLA_KV_RANK:].astype(BF16)
        cqn = _rms(c_q, qg_ref[...]).astype(BF16)
        ckvn = _rms(c_kv, kvg_ref[...]).astype(BF16)
        cos = cos_ref[:, rows]
        sin = sin_ref[:, rows]

        def rope(x1, x2):
            return x1 * cos - x2 * sin, x1 * sin + x2 * cos

        kr_t = _dot_nt(wkr_ref[...], hn)
        r1, r2 = rope(kr_t[:ROPE_HALF], kr_t[ROPE_HALF:])
        k_rope = jnp.concatenate(
            [jnp.zeros((MLA_NOPE, half), F32), r1, r2,
             jnp.zeros((QK_PAD - MLA_QK, half), F32)], axis=0).T

        k_nope = _dot(ckvn, wuk_ref[...])
        for h in range(MLA_HEADS):
            sl = slice(h * QK_PAD, (h + 1) * QK_PAD)
            k_ref[rows, sl] = (k_nope[:, sl] + k_rope).astype(BF16)

        v_t = _dot_nt(wuv_ref[...], ckvn).astype(BF16)
        ones_rows = (lax.broadcasted_iota(jnp.int32, (BF16_ROWS, half), 0) == 0).astype(BF16)
        for h in range(MLA_HEADS):
            v_ref[0, h * V_ROWS:h * V_ROWS + HEAD_DIM, rows] = v_t[h * HEAD_DIM:(h + 1) * HEAD_DIM]
            v_ref[0, h * V_ROWS + HEAD_DIM:(h + 1) * V_ROWS, rows] = ones_rows

        q_t = _dot_nt(wuq_ref[...], cqn)
        pad = jnp.zeros((QK_PAD - MLA_QK, half), BF16)
        for h in range(MLA_HEADS):
            src = h * MLA_QK
            dst = h * QK_PAD
            r1, r2 = rope(q_t[src + MLA_NOPE:src + MLA_NOPE + ROPE_HALF],
                          q_t[src + MLA_NOPE + ROPE_HALF:src + MLA_QK])
            q_ref[0, dst:dst + MLA_NOPE, rows] = (q_t[src:src + MLA_NOPE] * scale).astype(BF16)
            q_ref[0, dst + MLA_NOPE:dst + MLA_NOPE + ROPE_HALF, rows] = (r1 * scale).astype(BF16)
            q_ref[0, dst + MLA_NOPE + ROPE_HALF:dst + MLA_QK, rows] = (r2 * scale).astype(BF16)
            q_ref[0, dst + MLA_QK:dst + QK_PAD, rows] = pad

    first, second = slice(0, half), slice(half, tm)
    compressed = [compress(first), compress(second)]
    expand(first, *compressed[0])
    expand(second, *compressed[1])


def _mla_proj(x2, layer, g, w_a, w_kr, qg, kvg, w_uq, w_uk, w_uv, cos_t, sin_t):
    t = x2.shape[0]
    tm = ROW_TILE
    qw = MLA_HEADS * QK_PAD
    row = lambda w: pl.BlockSpec((tm, w), lambda i: (i, 0))
    col = lambda r: pl.BlockSpec((r, tm), lambda i: (0, i))
    tile = lambda r: pl.BlockSpec((1, r, tm), lambda i: (i, 0, 0))
    consts = [g, w_a, w_kr, qg, kvg, w_uq, w_uk, w_uv]
    return pl.pallas_call(
        _mla_proj_kernel,
        grid=(t // tm,),
        in_specs=[row(D_MODEL)] + [_layer_spec(c, layer) for c in consts]
                 + [col(ROPE_HALF), col(ROPE_HALF)],
        out_specs=[tile(qw), row(qw), tile(MLA_HEADS * V_ROWS), row(CROSS_WIDTH)],
        out_shape=[jax.ShapeDtypeStruct((t // tm, qw, tm), BF16),
                   jax.ShapeDtypeStruct((t, qw), BF16),
                   jax.ShapeDtypeStruct((t // tm, MLA_HEADS * V_ROWS, tm), BF16),
                   jax.ShapeDtypeStruct((t, CROSS_WIDTH), BF16)],
        compiler_params=_params("parallel"),
        name="mla_proj",
    )(x2, *consts, cos_t, sin_t)


def _mla_attn_kernel(q_ref, k_ref, v_ref, o_ref, s_ref, mb_ref, m_ref, acc_ref, bias_ref,
                     *, nt):
    t = q_ref.shape[2]
    half = t // 2
    acc_ref[...] = jnp.zeros(acc_ref.shape, F32)
    key = lax.broadcasted_iota(jnp.int32, (half, half), 0)
    qry = lax.broadcasted_iota(jnp.int32, (half, half), 1)
    bias_ref[...] = jnp.where(key <= qry, 0.0, NEG)

    def head_rows(h):
        return slice(h * QK_PAD, (h + 1) * QK_PAD)

    def scores(i, j, slot, h):
        start = pl.multiple_of(j * t, t)
        kb = k_ref[pl.ds(start, t), head_rows(h)]
        s = _dot(kb, q_ref[i, head_rows(h), :])
        s_ref[slot, h] = s
        mb_ref[slot, h] = jnp.max(s, axis=0, keepdims=True)

    def consume(i, j, slot, h):
        s = s_ref[slot, h]
        m_prev = jnp.where(j == 0, NEG, m_ref[i, h])
        m_new = jnp.maximum(m_prev, mb_ref[slot, h])
        a = jnp.exp2(m_prev - m_new)
        p = jnp.exp2(s - m_new).astype(BF16)
        rows = slice(h * V_ROWS, (h + 1) * V_ROWS)
        m_ref[i, h] = m_new
        acc_ref[i, rows, :] = a * acc_ref[i, rows, :] + _dot(v_ref[j, rows, :], p)

    def scores_diag(i, slot, h):
        start = pl.multiple_of(i * t, t)
        q = q_ref[i, head_rows(h), :]
        s_top = _dot(k_ref[pl.ds(start, half), head_rows(h)], q)
        s_bot = _dot(k_ref[pl.ds(start + half, half), head_rows(h)], q[:, half:])
        tri = bias_ref[...]
        s_left = s_top[:, :half] + tri
        s_bot = s_bot + tri
        s_ref[slot, h, :half, :half] = s_left
        s_ref[slot, h, :half, half:] = s_top[:, half:]
        s_ref[slot, h, half:, half:] = s_bot
        mb_ref[slot, h, :, :half] = jnp.max(s_left, axis=0, keepdims=True)
        mb_ref[slot, h, :, half:] = jnp.maximum(
            jnp.max(s_top[:, half:], axis=0, keepdims=True), jnp.max(s_bot, axis=0, keepdims=True))

    def consume_diag(i, slot, h):
        m_prev = jnp.where(i == 0, NEG, m_ref[i, h])
        m_new = jnp.maximum(m_prev, mb_ref[slot, h])
        a = jnp.exp2(m_prev - m_new)
        p_top = jnp.exp2(s_ref[slot, h, :half, :] - m_new).astype(BF16)
        p_bot = jnp.exp2(s_ref[slot, h, half:, half:] - m_new[:, half:]).astype(BF16)
        rows = slice(h * V_ROWS, (h + 1) * V_ROWS)
        acc = a * acc_ref[i, rows, :] + _dot(v_ref[i, rows, :half], p_top)
        acc_r = acc[:, half:] + _dot(v_ref[i, rows, half:], p_bot)
        out = slice(h * HEAD_DIM, (h + 1) * HEAD_DIM)
        o_ref[i, out, :half] = (acc[:HEAD_DIM, :half]
                                / acc[HEAD_DIM:HEAD_DIM + 1, :half]).astype(BF16)
        o_ref[i, out, half:] = (acc_r[:HEAD_DIM] / acc_r[HEAD_DIM:HEAD_DIM + 1]).astype(BF16)

    def next_full(i, j):
        wrap = j + 1 == i
        i_n = jnp.minimum(jnp.where(wrap, i + 1, i), nt - 1)
        return i_n, jnp.where(wrap, 0, j + 1)

    def full_trip(_, cur):
        for b in range(FULL_BLOCKS_PER_TRIP):
            nxt = next_full(*cur)
            for h in range(2):
                scores(*nxt, 1 - b % 2, h)
                consume(*cur, b % 2, h)
            cur = nxt
        return cur

    def diag_trip(_, i):
        for b in range(DIAG_BLOCKS_PER_TRIP):
            i_n = jnp.minimum(i + 1, nt - 1)
            for h in range(2):
                scores_diag(i_n, 1 - b % 2, h)
                consume_diag(i, b % 2, h)
            i = i + 1
        return i

    for h in range(2):
        scores(1, 0, 0, h)
    lax.fori_loop(0, nt * (nt - 1) // 2 // FULL_BLOCKS_PER_TRIP, full_trip,
                  (jnp.int32(1), jnp.int32(0)))
    for h in range(2):
        scores_diag(0, 0, h)
    lax.fori_loop(0, nt // DIAG_BLOCKS_PER_TRIP, diag_trip, jnp.int32(0))


def _mla_attn(q_t, k, v_t, batch, seq):
    t = ATTN_TILE
    nt = seq // t
    assert FULL_BLOCKS_PER_TRIP % 2 == 0 and DIAG_BLOCKS_PER_TRIP % 2 == 0
    assert (nt * (nt - 1) // 2) % FULL_BLOCKS_PER_TRIP == 0 and nt % DIAG_BLOCKS_PER_TRIP == 0
    pairs = MLA_HEADS // 2
    return pl.pallas_call(
        functools.partial(_mla_attn_kernel, nt=nt),
        grid=(batch, pairs),
        in_specs=[pl.BlockSpec((nt, 2 * QK_PAD, t), lambda b, p: (b, p, 0)),
                  pl.BlockSpec((seq, 2 * QK_PAD), lambda b, p: (b, p)),
                  pl.BlockSpec((nt, 2 * V_ROWS, t), lambda b, p: (b, p, 0))],
        out_specs=pl.BlockSpec((nt, LANES, t), lambda b, p: (b, p, 0)),
        out_shape=jax.ShapeDtypeStruct((batch * nt, MIX_WIDTH, t), BF16),
        scratch_shapes=[pltpu.VMEM((2, 2, t, t), F32), pltpu.VMEM((2, 2, 1, t), F32),
                        pltpu.VMEM((nt, 2, 1, t), F32), pltpu.VMEM((nt, 2 * V_ROWS, t), F32),
                        pltpu.VMEM((t // 2, t // 2), F32)],
        compiler_params=_params("parallel", "parallel"),
        name="mla_attn",
    )(q_t, k, v_t)


def _swa_proj_kernel(x_ref, g_ref, wq_ref, wkc_ref, wv_ref, q_ref, k_ref, v_ref, qc_ref):
    tm = x_ref.shape[0]
    half = tm // 2
    nk = SWA_KV_HEADS * HEAD_DIM
    ones_rows = (lax.broadcasted_iota(jnp.int32, (BF16_ROWS, WINDOW), 0) == 0).astype(BF16)
    norms = [_rms(x_ref[r, :], g_ref[...]).astype(BF16)
             for r in (slice(0, half), slice(half, tm))]
    for part, hn in enumerate(norms):
        rows = slice(part * half, (part + 1) * half)
        q_ref[0, :, rows] = (_dot_nt(wq_ref[...], hn) * (HEAD_DIM ** -0.5 * LOG2E)).astype(BF16)
        kc = _dot(hn, wkc_ref[...])
        k_ref[rows, :] = kc[:, :nk].astype(BF16)
        qc_ref[rows, :] = kc[:, nk:].astype(BF16)
        v_t = _dot_nt(wv_ref[...], hn).astype(BF16)
        for w in range(half // WINDOW):
            lanes = slice(w * WINDOW, (w + 1) * WINDOW)
            win = part * (half // WINDOW) + w
            for h in range(SWA_KV_HEADS):
                v_ref[win, h * V_ROWS:h * V_ROWS + HEAD_DIM, :] = v_t[
                    h * HEAD_DIM:(h + 1) * HEAD_DIM, lanes]
                v_ref[win, h * V_ROWS + HEAD_DIM:(h + 1) * V_ROWS, :] = ones_rows


def _swa_proj(x2, layer, g, w_q_t, w_kc, w_v_t):
    t = x2.shape[0]
    tm = ROW_TILE
    nk = SWA_KV_HEADS * HEAD_DIM
    nwin = tm // WINDOW
    row = lambda w: pl.BlockSpec((tm, w), lambda i: (i, 0))
    consts = [g, w_q_t, w_kc, w_v_t]
    return pl.pallas_call(
        _swa_proj_kernel,
        grid=(t // tm,),
        in_specs=[row(D_MODEL)] + [_layer_spec(c, layer) for c in consts],
        out_specs=[pl.BlockSpec((1, MIX_WIDTH, tm), lambda i: (i, 0, 0)), row(nk),
                   pl.BlockSpec((nwin, SWA_KV_HEADS * V_ROWS, WINDOW), lambda i: (i, 0, 0)),
                   row(CROSS_WIDTH)],
        out_shape=[jax.ShapeDtypeStruct((t // tm, MIX_WIDTH, tm), BF16),
                   jax.ShapeDtypeStruct((t, nk), BF16),
                   jax.ShapeDtypeStruct((t // WINDOW, SWA_KV_HEADS * V_ROWS, WINDOW), BF16),
                   jax.ShapeDtypeStruct((t, CROSS_WIDTH), BF16)],
        compiler_params=_params("parallel"),
        name="swa_proj",
    )(x2, *consts)


def _swa_attn_kernel(slope_ref, sink_ref, q_ref, ko_ref, kp_ref, vo_ref, vp_ref,
                     pqo_ref, pko_ref, pkp_ref, o_ref, *, nwin):
    i = pl.program_id(1)
    w_ = WINDOW
    kcat = jnp.concatenate([kp_ref[...], ko_ref[...]], axis=0)
    pos_k = jnp.concatenate([pkp_ref[...], pko_ref[...]], axis=0)
    kj = lax.broadcasted_iota(jnp.int32, (2 * w_, w_), 0)
    qi = lax.broadcasted_iota(jnp.int32, (2 * w_, w_), 1)
    rel = w_ + qi - kj
    band = (rel >= 0) & (rel < w_)
    first_band = band & ((kj >= w_) | (i > 0))
    zeros = jnp.zeros((HEAD_DIM, w_), BF16)

    def window_keys(w):
        return slice(w * w_, (w + 2) * w_)

    def scores(w, kh):
        k_pair = kcat[window_keys(w), (kh // 2) * LANES:(kh // 2 + 1) * LANES]
        qs = []
        for g in range(SWA_GROUP):
            hq = kh * SWA_GROUP + g
            qh = q_ref[0, hq * HEAD_DIM:(hq + 1) * HEAD_DIM, w * w_:(w + 1) * w_]
            qs.append(jnp.concatenate([qh, zeros] if kh % 2 == 0 else [zeros, qh], axis=0))
        return _dot(k_pair, jnp.concatenate(qs, axis=1))

    dists = {}

    def masked_dist(w):
        if w not in dists:
            dists[w] = jnp.where(first_band if w == 0 else band,
                                 (pqo_ref[w] - pos_k[window_keys(w)]).astype(F32), MASK_DIST)
        return dists[w]

    def finish(w, kh, s3):
        dist = masked_dist(w)
        v_prev = vp_ref[0] if w == 0 else vo_ref[w - 1]
        rows = slice(kh * V_ROWS, (kh + 1) * V_ROWS)
        v_win = jnp.concatenate([v_prev[rows], vo_ref[w, rows, :]], axis=1)
        ps = []
        sink_terms = []
        for g in range(SWA_GROUP):
            hq = kh * SWA_GROUP + g
            sink = sink_ref[hq]
            s = s3[:, g * w_:(g + 1) * w_] - slope_ref[hq] * dist
            m = jnp.maximum(jnp.max(s, axis=0, keepdims=True), sink)
            ps.append(jnp.exp2(s - m).astype(BF16))
            sink_terms.append(jnp.exp2(sink - m))
        o3 = _dot(v_win, jnp.concatenate(ps, axis=1))
        for g in range(SWA_GROUP):
            hq = kh * SWA_GROUP + g
            og = o3[:, g * w_:(g + 1) * w_]
            o_ref[0, hq * HEAD_DIM:(hq + 1) * HEAD_DIM, w * w_:(w + 1) * w_] = (
                og[:HEAD_DIM] / (og[HEAD_DIM:HEAD_DIM + 1] + sink_terms[g])).astype(BF16)

    units = [(w, kh) for w in range(nwin) for kh in range(SWA_KV_HEADS)]
    pending = {}
    for idx in range(len(units) + SWA_LOOKAHEAD):
        if idx < len(units):
            pending[idx] = scores(*units[idx])
        if idx >= SWA_LOOKAHEAD:
            done = idx - SWA_LOOKAHEAD
            finish(*units[done], pending.pop(done))


def _swa_attn(slopes, sinks, q_t, k, v_t, pos_col, pos_row, batch, seq):
    rows = SWA_ROWS
    nwin = rows // WINDOW
    nsteps = seq // rows
    nblk = seq // WINDOW
    nk = SWA_KV_HEADS * HEAD_DIM
    vr = SWA_KV_HEADS * V_ROWS
    own = lambda b, i: (b * nsteps + i, 0)
    own3 = lambda b, i: (b * nsteps + i, 0, 0)
    prev = lambda b, i: (b * nblk + jnp.maximum(i * nwin - 1, 0), 0)
    prev3 = lambda b, i: (b * nblk + jnp.maximum(i * nwin - 1, 0), 0, 0)
    smem = pl.BlockSpec(memory_space=pltpu.SMEM)
    return pl.pallas_call(
        functools.partial(_swa_attn_kernel, nwin=nwin),
        grid=(batch, nsteps),
        in_specs=[smem, smem,
                  pl.BlockSpec((1, MIX_WIDTH, rows), own3),
                  pl.BlockSpec((rows, nk), own),
                  pl.BlockSpec((WINDOW, nk), prev),
                  pl.BlockSpec((nwin, vr, WINDOW), own3),
                  pl.BlockSpec((1, vr, WINDOW), prev3),
                  pl.BlockSpec((nwin, 1, WINDOW), own3),
                  pl.BlockSpec((rows, 1), own),
                  pl.BlockSpec((WINDOW, 1), prev)],
        out_specs=pl.BlockSpec((1, MIX_WIDTH, rows), own3),
        out_shape=jax.ShapeDtypeStruct((batch * nsteps, MIX_WIDTH, rows), BF16),
        compiler_params=_params("parallel", "parallel"),
        name="swa_attn",
    )(slopes, sinks, q_t, k, k, v_t, v_t, pos_row, pos_col, pos_col)


def _out_mlp_kernel(*refs, final):
    if final:
        (x_ref, mix_ref, qc_ref, km_ref, vm_ref, wo_ref, g_ref, wup_ref, wdn_ref,
         gf_ref, o_ref, acc_ref) = refs
    else:
        (x_ref, mix_ref, qc_ref, km_ref, vm_ref, wo_ref, g_ref, wup_ref, wdn_ref,
         o_ref, acc_ref) = refs
    lo = _low_half()
    n_pairs = CROSS_WIDTH // LANES
    scores = []
    for pair in range(n_pairs):
        sl = slice(pair * LANES, (pair + 1) * LANES)
        qp = qc_ref[:, sl]
        for half in range(2):
            sel = lo if half == 0 else jnp.logical_not(lo)
            scores.append(_dot_nt(jnp.where(sel, qp, jnp.zeros_like(qp)), km_ref[:, sl]))
    cross = []
    for pair in range(n_pairs):
        vp = vm_ref[:, pair * LANES:(pair + 1) * LANES]
        outs = []
        for half in range(2):
            s = scores[2 * pair + half]
            e = jnp.exp(s - jnp.max(s, axis=-1, keepdims=True))
            den = jnp.sum(e, axis=-1, keepdims=True)
            outs.append(_dot(e.astype(BF16), vp) / den)
        cross.append(jnp.where(lo, outs[0], outs[1]).astype(BF16))
    mix = mix_ref[0].astype(F32).T.astype(BF16)
    attn = jnp.concatenate([mix] + cross, axis=1)
    x1 = x_ref[...] + _dot(attn, wo_ref[...])
    hn = _rms(x1, g_ref[...]).astype(BF16)
    acc_ref[...] = x1
    for c in range(D_FF // FF_CHUNK):
        cols = slice(c * FF_CHUNK, (c + 1) * FF_CHUNK)
        h = jnp.maximum(_dot(hn, wup_ref[:, cols]), 0.0)
        acc_ref[...] += _dot((h * h).astype(BF16), wdn_ref[cols, :])
    if final:
        o_ref[...] = _rms(acc_ref[...], gf_ref[...])
    else:
        o_ref[...] = acc_ref[...]


def _out_mlp(x2, mix, qc, memkv, layer, w_o, g, w_up, w_dn, g_final, seq):
    t = x2.shape[0]
    tm = ROW_TILE
    per_b = seq // tm
    final = g_final is not None
    row = lambda w: pl.BlockSpec((tm, w), lambda i: (i, 0))
    in_specs = [row(D_MODEL), pl.BlockSpec((1, MIX_WIDTH, tm), lambda i: (i, 0, 0)),
                row(CROSS_WIDTH),
                pl.BlockSpec((N_MEM, CROSS_WIDTH), lambda i: (i // per_b, 2 * layer)),
                pl.BlockSpec((N_MEM, CROSS_WIDTH), lambda i: (i // per_b, 2 * layer + 1)),
                _layer_spec(w_o, layer), _layer_spec(g, layer),
                _layer_spec(w_up, layer), _layer_spec(w_dn, layer)]
    args = [x2, mix, qc, memkv, memkv, w_o, g, w_up, w_dn]
    if final:
        in_specs.append(_const_spec(g_final.shape))
        args.append(g_final)
    return pl.pallas_call(
        functools.partial(_out_mlp_kernel, final=final),
        grid=(t // tm,),
        in_specs=in_specs,
        out_specs=row(D_MODEL),
        out_shape=jax.ShapeDtypeStruct((t, D_MODEL), F32),
        scratch_shapes=[pltpu.VMEM((tm, D_MODEL), F32)],
        compiler_params=_params("parallel"),
        name="out_mlp",
    )(*args)


def _prep_mla(w_in, w_uq, w_ukv):
    n = w_in.shape[0]
    n1 = MLA_Q_RANK + MLA_KV_RANK
    w_in = w_in.astype(BF16)
    w_a = jnp.concatenate([w_in[:, :, :n1], w_in[:, :, n1 + MLA_ROPE:]], axis=2)
    w_kr_t = w_in[:, :, n1:n1 + MLA_ROPE].transpose(0, 2, 1)
    w_uq_t = w_uq.astype(BF16).transpose(0, 2, 1)
    kv = w_ukv.astype(BF16).reshape(n, MLA_KV_RANK, MLA_HEADS, 2 * HEAD_DIM)
    w_uk = jnp.pad(kv[..., :MLA_NOPE], ((0, 0), (0, 0), (0, 0), (0, QK_PAD - MLA_NOPE)))
    w_uk = w_uk.reshape(n, MLA_KV_RANK, MLA_HEADS * QK_PAD)
    w_uv_t = kv[..., MLA_NOPE:].reshape(n, MLA_KV_RANK, MIX_WIDTH).transpose(0, 2, 1)
    return w_a, w_kr_t, w_uq_t, w_uk, w_uv_t


def _prep_swa(w_in):
    nq = SWA_Q_HEADS * HEAD_DIM
    nk = SWA_KV_HEADS * HEAD_DIM
    w_in = w_in.astype(BF16)
    w_q_t = w_in[:, :, :nq].transpose(0, 2, 1)
    w_kc = jnp.concatenate([w_in[:, :, nq:nq + nk], w_in[:, :, nq + 2 * nk:]], axis=2)
    w_v_t = w_in[:, :, nq + nk:nq + 2 * nk].transpose(0, 2, 1)
    return w_q_t, w_kc, w_v_t


def kernel(x, mem, positions, attn_norm_g, mlp_norm_g, mem_norm_g, final_norm_g,
           mla_w_in, mla_q_norm_g, mla_kv_norm_g, mla_w_uq, mla_w_ukv,
           swa_w_in, swa_sinks, w_mem_kv, w_o, mlp_w_up, mlp_w_down):
    batch, seq, d = x.shape
    depth = attn_norm_g.shape[0]
    t = batch * seq
    x2 = x.reshape(t, d)

    w_mem = w_mem_kv.astype(BF16)
    w_mem = jnp.concatenate([w_mem[:, :, :CROSS_WIDTH] * (HEAD_DIM ** -0.5),
                             w_mem[:, :, CROSS_WIDTH:]], axis=2)
    memkv = _memkv(mem.reshape(batch * N_MEM, d), mem_norm_g.reshape(1, d), w_mem)

    inv = ROPE_THETA ** (-(jnp.arange(ROPE_HALF, dtype=F32) * 2.0) / MLA_ROPE)
    cos_t, sin_t = _rope_tables(positions.astype(F32).reshape(1, t), inv.reshape(ROPE_HALF, 1))

    pos_col_i = positions.reshape(t, 1)
    pos_row_i = positions.reshape(t // WINDOW, 1, WINDOW)
    slopes = 2.0 ** (-8.0 * (jnp.arange(SWA_Q_HEADS, dtype=F32) + 1.0) / SWA_Q_HEADS) * LOG2E

    g_attn = attn_norm_g.reshape(depth, 1, d)
    g_mlp = mlp_norm_g.reshape(depth, 1, d)
    mla_w = _prep_mla(mla_w_in, mla_w_uq, mla_w_ukv)
    mla_qg = mla_q_norm_g.reshape(-1, 1, MLA_Q_RANK)
    mla_kvg = mla_kv_norm_g.reshape(-1, 1, MLA_KV_RANK)
    swa_w = _prep_swa(swa_w_in)
    w_o_b = w_o.astype(BF16)
    w_up = mlp_w_up.astype(BF16)
    w_dn = mlp_w_down.astype(BF16)
    sinks = swa_sinks.astype(F32) * LOG2E
    g_attn_mla, g_attn_swa = g_attn[0::2], g_attn[1::2]

    for i in range(depth):
        j = i // 2
        is_mla = i % 2 == 0
        if is_mla:
            w_a, w_kr_t, w_uq_t, w_uk, w_uv_t = mla_w
            q_t, k, v_t, qc = _mla_proj(x2, j, g_attn_mla, w_a, w_kr_t, mla_qg, mla_kvg,
                                        w_uq_t, w_uk, w_uv_t, cos_t, sin_t)
            mix = _mla_attn(q_t, k, v_t, batch, seq)
        else:
            q_t, k, v_t, qc = _swa_proj(x2, j, g_attn_swa, *swa_w)
            mix = _swa_attn(slopes, sinks[j], q_t, k, v_t, pos_col_i, pos_row_i, batch, seq)
        g_final = final_norm_g.reshape(1, d) if i == depth - 1 else None
        x2 = _out_mlp(x2, mix, qc, memkv, i, w_o_b, g_mlp, w_up, w_dn, g_final, seq)
    return x2.reshape(batch, seq, d)
```

```python
import functools
import math

import jax
import jax.numpy as jnp
from jax import lax
from jax.experimental import pallas as pl
from jax.experimental.pallas import tpu as pltpu

F32 = jnp.float32
BF16 = jnp.bfloat16

D_MODEL = 1024
HEAD_DIM = 64
N_MEM = 256
CROSS_WIDTH = 256
MIX_WIDTH = 768
EPS = 1e-6
NEG = -1e30
LOG2E = math.log2(math.e)
MASK_DIST = 1e32
MLA_HEADS = 12
MLA_Q_RANK = 384
MLA_KV_RANK = 256
MLA_NOPE = 64
MLA_ROPE = 32
MLA_QK = MLA_NOPE + MLA_ROPE
ROPE_THETA = 10000.0
SWA_Q_HEADS = 12
SWA_KV_HEADS = 4
SWA_GROUP = 3
WINDOW = 128
D_FF = 4 * D_MODEL

LANES = 128
QK_PAD = LANES
ROPE_HALF = MLA_ROPE // 2
BF16_ROWS = 16
V_ROWS = HEAD_DIM + BF16_ROWS

SUB_TILE = 512
ROW_TILE = 2 * SUB_TILE
ATTN_TILE = SUB_TILE
FULL_BLOCKS_PER_TRIP = 14
DIAG_BLOCKS_PER_TRIP = 8
SWA_ROWS = SUB_TILE
SWA_LOOKAHEAD = 2
FF_CHUNK = 1024
VMEM_LIMIT = 48 * 1024 * 1024


def _rms(x, g):
    ms = jnp.mean(x * x, axis=-1, keepdims=True)
    return x * lax.rsqrt(ms + EPS) * g


def _dot(a, b):
    return jnp.dot(a, b, preferred_element_type=F32)


def _dot_nt(a, b):
    return lax.dot_general(a, b, (((1,), (1,)), ((), ())), preferred_element_type=F32)


def _low_half():
    return lax.broadcasted_iota(jnp.int32, (1, LANES), 1) < HEAD_DIM


def _const_spec(shape):
    nd = len(shape)
    return pl.BlockSpec(shape, lambda *_: (0,) * nd, pipeline_mode=pl.Buffered(1))


def _layer_spec(stacked, layer):
    nd = stacked.ndim
    return pl.BlockSpec((None,) + stacked.shape[1:], lambda *_: (layer,) + (0,) * (nd - 1),
                        pipeline_mode=pl.Buffered(1))


def _params(*sem):
    return pltpu.CompilerParams(dimension_semantics=sem, vmem_limit_bytes=VMEM_LIMIT)


def _rope_table_kernel(pos_ref, inv_ref, cos_ref, sin_ref):
    ang = inv_ref[...] * pos_ref[...]
    cos_ref[...] = jnp.cos(ang)
    sin_ref[...] = jnp.sin(ang)


def _rope_tables(pos_row, inv_col):
    t = pos_row.shape[1]
    tn = 2048
    return pl.pallas_call(
        _rope_table_kernel,
        grid=(t // tn,),
        in_specs=[pl.BlockSpec((1, tn), lambda i: (0, i)),
                  pl.BlockSpec((ROPE_HALF, 1), lambda i: (0, 0))],
        out_specs=[pl.BlockSpec((ROPE_HALF, tn), lambda i: (0, i))] * 2,
        out_shape=[jax.ShapeDtypeStruct((ROPE_HALF, t), F32)] * 2,
        compiler_params=_params("parallel"),
        name="rope_tables",
    )(pos_row, inv_col)


def _memkv_kernel(mem_ref, g_ref, w_ref, o_ref):
    mn = _rms(mem_ref[...], g_ref[...]).astype(BF16)
    o_ref[...] = _dot(mn, w_ref[...]).astype(BF16)


def _memkv(mem2, g, w_all):
    rows = mem2.shape[0]
    depth, _, ncol = w_all.shape
    return pl.pallas_call(
        _memkv_kernel,
        grid=(rows // N_MEM, depth),
        in_specs=[pl.BlockSpec((N_MEM, D_MODEL), lambda b, l: (b, 0)),
                  _const_spec((1, D_MODEL)),
                  pl.BlockSpec((None, D_MODEL, ncol), lambda b, l: (l, 0, 0))],
        out_specs=pl.BlockSpec((N_MEM, ncol), lambda b, l: (b, l)),
        out_shape=jax.ShapeDtypeStruct((rows, depth * ncol), BF16),
        compiler_params=_params("parallel", "parallel"),
        name="mem_kv",
    )(mem2, g, w_all)


def _mla_proj_kernel(x_ref, g_ref, wa_ref, wkr_ref, qg_ref, kvg_ref, wuq_ref, wuk_ref,
                     wuv_ref, cos_ref, sin_ref, q_ref, k_ref, v_ref, qc_ref):
    half = SUB_TILE
    scale = MLA_QK ** -0.5 * LOG2E

    def compress(rows):
        hn = _rms(x_ref[rows, :], g_ref[...]).astype(BF16)
        return hn, _dot(hn, wa_ref[...])

    def expand(part, hn, proj):
        rows = slice(part * half, (part + 1) * half)
        c_q = proj[:, :MLA_Q_RANK]
        c_kv = proj[:, MLA_Q_RANK:MLA_Q_RANK + MLA_KV_RANK]
        qc_ref[rows, :] = proj[:, MLA_Q_RANK + MLA_KV_RANK:].astype(BF16)
        cqn = _rms(c_q, qg_ref[...]).astype(BF16)
        ckvn = _rms(c_kv, kvg_ref[...]).astype(BF16)
        cos = cos_ref[:, rows]
        sin = sin_ref[:, rows]

        def rope(x1, x2):
            return x1 * cos - x2 * sin, x1 * sin + x2 * cos

        kr_t = _dot_nt(wkr_ref[...], hn)
        r1, r2 = rope(kr_t[:ROPE_HALF], kr_t[ROPE_HALF:])
        k_rope = jnp.concatenate(
            [jnp.zeros((MLA_NOPE, half), F32), r1, r2,
             jnp.zeros((QK_PAD - MLA_QK, half), F32)], axis=0).T

        k_nope = _dot(ckvn, wuk_ref[...])
        for h in range(MLA_HEADS):
            sl = slice(h * QK_PAD, (h + 1) * QK_PAD)
            k_ref[rows, sl] = (k_nope[:, sl] + k_rope).astype(BF16)

        v_t = _dot_nt(wuv_ref[...], ckvn).astype(BF16)
        ones_rows = (lax.broadcasted_iota(jnp.int32, (BF16_ROWS, half), 0) == 0).astype(BF16)
        for h in range(MLA_HEADS):
            v_ref[part, h * V_ROWS:h * V_ROWS + HEAD_DIM, :] = v_t[h * HEAD_DIM:(h + 1) * HEAD_DIM]
            v_ref[part, h * V_ROWS + HEAD_DIM:(h + 1) * V_ROWS, :] = ones_rows

        q_t = _dot_nt(wuq_ref[...], cqn)
        pad = jnp.zeros((QK_PAD - MLA_QK, half), BF16)
        for h in range(MLA_HEADS):
            src = h * MLA_QK
            dst = h * QK_PAD
            r1, r2 = rope(q_t[src + MLA_NOPE:src + MLA_NOPE + ROPE_HALF],
                          q_t[src + MLA_NOPE + ROPE_HALF:src + MLA_QK])
            q_ref[part, dst:dst + MLA_NOPE, :] = (q_t[src:src + MLA_NOPE] * scale).astype(BF16)
            q_ref[part, dst + MLA_NOPE:dst + MLA_NOPE + ROPE_HALF, :] = (r1 * scale).astype(BF16)
            q_ref[part, dst + MLA_NOPE + ROPE_HALF:dst + MLA_QK, :] = (r2 * scale).astype(BF16)
            q_ref[part, dst + MLA_QK:dst + QK_PAD, :] = pad

    n_sub = x_ref.shape[0] // half
    compressed = [compress(slice(p * half, (p + 1) * half)) for p in range(n_sub)]
    for p in range(n_sub):
        expand(p, *compressed[p])


def _mla_proj(x2, layer, g, w_a, w_kr, qg, kvg, w_uq, w_uk, w_uv, cos_t, sin_t):
    t = x2.shape[0]
    tm = ROW_TILE
    qw = MLA_HEADS * QK_PAD
    row = lambda w: pl.BlockSpec((tm, w), lambda i: (i, 0))
    col = lambda r: pl.BlockSpec((r, tm), lambda i: (0, i))
    n_sub = tm // SUB_TILE
    tile = lambda r: pl.BlockSpec((n_sub, r, SUB_TILE), lambda i: (i, 0, 0))
    consts = [g, w_a, w_kr, qg, kvg, w_uq, w_uk, w_uv]
    return pl.pallas_call(
        _mla_proj_kernel,
        grid=(t // tm,),
        in_specs=[row(D_MODEL)] + [_layer_spec(c, layer) for c in consts]
                 + [col(ROPE_HALF), col(ROPE_HALF)],
        out_specs=[tile(qw), row(qw), tile(MLA_HEADS * V_ROWS), row(CROSS_WIDTH)],
        out_shape=[jax.ShapeDtypeStruct((t // SUB_TILE, qw, SUB_TILE), BF16),
                   jax.ShapeDtypeStruct((t, qw), BF16),
                   jax.ShapeDtypeStruct((t // SUB_TILE, MLA_HEADS * V_ROWS, SUB_TILE), BF16),
                   jax.ShapeDtypeStruct((t, CROSS_WIDTH), BF16)],
        compiler_params=_params("parallel"),
        name="mla_proj",
    )(x2, *consts, cos_t, sin_t)


def _mla_attn_kernel(q_ref, k_ref, v_ref, o_ref, s_ref, mb_ref, m_ref, acc_ref, bias_ref,
                     *, nt):
    t = q_ref.shape[2]
    half = t // 2
    acc_ref[...] = jnp.zeros(acc_ref.shape, F32)
    key = lax.broadcasted_iota(jnp.int32, (half, half), 0)
    qry = lax.broadcasted_iota(jnp.int32, (half, half), 1)
    bias_ref[...] = jnp.where(key <= qry, 0.0, NEG)

    def head_rows(h):
        return slice(h * QK_PAD, (h + 1) * QK_PAD)

    def scores(i, j, slot, h):
        start = pl.multiple_of(j * t, t)
        kb = k_ref[pl.ds(start, t), head_rows(h)]
        s = _dot(kb, q_ref[i, head_rows(h), :])
        s_ref[slot, h] = s
        mb_ref[slot, h] = jnp.max(s, axis=0, keepdims=True)

    def consume(i, j, slot, h):
        s = s_ref[slot, h]
        m_prev = jnp.where(j == 0, NEG, m_ref[i, h])
        m_new = jnp.maximum(m_prev, mb_ref[slot, h])
        a = jnp.exp2(m_prev - m_new)
        p = jnp.exp2(s - m_new).astype(BF16)
        rows = slice(h * V_ROWS, (h + 1) * V_ROWS)
        m_ref[i, h] = m_new
        acc_ref[i, rows, :] = a * acc_ref[i, rows, :] + _dot(v_ref[j, rows, :], p)

    def scores_diag(i, slot, h):
        start = pl.multiple_of(i * t, t)
        q = q_ref[i, head_rows(h), :]
        s_top = _dot(k_ref[pl.ds(start, half), head_rows(h)], q)
        s_bot = _dot(k_ref[pl.ds(start + half, half), head_rows(h)], q[:, half:])
        tri = bias_ref[...]
        s_left = s_top[:, :half] + tri
        s_bot = s_bot + tri
        s_ref[slot, h, :half, :half] = s_left
        s_ref[slot, h, :half, half:] = s_top[:, half:]
        s_ref[slot, h, half:, half:] = s_bot
        mb_ref[slot, h, :, :half] = jnp.max(s_left, axis=0, keepdims=True)
        mb_ref[slot, h, :, half:] = jnp.maximum(
            jnp.max(s_top[:, half:], axis=0, keepdims=True), jnp.max(s_bot, axis=0, keepdims=True))

    def consume_diag(i, slot, h):
        m_prev = jnp.where(i == 0, NEG, m_ref[i, h])
        m_new = jnp.maximum(m_prev, mb_ref[slot, h])
        a = jnp.exp2(m_prev - m_new)
        p_top = jnp.exp2(s_ref[slot, h, :half, :] - m_new).astype(BF16)
        p_bot = jnp.exp2(s_ref[slot, h, half:, half:] - m_new[:, half:]).astype(BF16)
        rows = slice(h * V_ROWS, (h + 1) * V_ROWS)
        acc = a * acc_ref[i, rows, :] + _dot(v_ref[i, rows, :half], p_top)
        acc_r = acc[:, half:] + _dot(v_ref[i, rows, half:], p_bot)
        out = slice(h * HEAD_DIM, (h + 1) * HEAD_DIM)
        o_ref[i, out, :half] = (acc[:HEAD_DIM, :half]
                                / acc[HEAD_DIM:HEAD_DIM + 1, :half]).astype(BF16)
        o_ref[i, out, half:] = (acc_r[:HEAD_DIM] / acc_r[HEAD_DIM:HEAD_DIM + 1]).astype(BF16)

    def next_full(i, j):
        wrap = j + 1 == i
        i_n = jnp.minimum(jnp.where(wrap, i + 1, i), nt - 1)
        return i_n, jnp.where(wrap, 0, j + 1)

    def full_trip(_, cur):
        for b in range(FULL_BLOCKS_PER_TRIP):
            nxt = next_full(*cur)
            for h in range(2):
                scores(*nxt, 1 - b % 2, h)
                consume(*cur, b % 2, h)
            cur = nxt
        return cur

    def diag_trip(_, i):
        for b in range(DIAG_BLOCKS_PER_TRIP):
            i_n = jnp.minimum(i + 1, nt - 1)
            for h in range(2):
                scores_diag(i_n, 1 - b % 2, h)
                consume_diag(i, b % 2, h)
            i = i + 1
        return i

    for h in range(2):
        scores(1, 0, 0, h)
    lax.fori_loop(0, nt * (nt - 1) // 2 // FULL_BLOCKS_PER_TRIP, full_trip,
                  (jnp.int32(1), jnp.int32(0)))
    for h in range(2):
        scores_diag(0, 0, h)
    lax.fori_loop(0, nt // DIAG_BLOCKS_PER_TRIP, diag_trip, jnp.int32(0))


def _mla_attn(q_t, k, v_t, batch, seq):
    t = ATTN_TILE
    nt = seq // t
    assert FULL_BLOCKS_PER_TRIP % 2 == 0 and DIAG_BLOCKS_PER_TRIP % 2 == 0
    assert (nt * (nt - 1) // 2) % FULL_BLOCKS_PER_TRIP == 0 and nt % DIAG_BLOCKS_PER_TRIP == 0
    pairs = MLA_HEADS // 2
    return pl.pallas_call(
        functools.partial(_mla_attn_kernel, nt=nt),
        grid=(batch, pairs),
        in_specs=[pl.BlockSpec((nt, 2 * QK_PAD, t), lambda b, p: (b, p, 0)),
                  pl.BlockSpec((seq, 2 * QK_PAD), lambda b, p: (b, p)),
                  pl.BlockSpec((nt, 2 * V_ROWS, t), lambda b, p: (b, p, 0))],
        out_specs=pl.BlockSpec((nt, LANES, t), lambda b, p: (b, p, 0)),
        out_shape=jax.ShapeDtypeStruct((batch * nt, MIX_WIDTH, t), BF16),
        scratch_shapes=[pltpu.VMEM((2, 2, t, t), F32), pltpu.VMEM((2, 2, 1, t), F32),
                        pltpu.VMEM((nt, 2, 1, t), F32), pltpu.VMEM((nt, 2 * V_ROWS, t), F32),
                        pltpu.VMEM((t // 2, t // 2), F32)],
        compiler_params=_params("parallel", "parallel"),
        name="mla_attn",
    )(q_t, k, v_t)


def _swa_proj_kernel(x_ref, g_ref, wq_ref, wkc_ref, wv_ref, q_ref, k_ref, v_ref, qc_ref):
    half = SUB_TILE
    nk = SWA_KV_HEADS * HEAD_DIM
    ones_rows = (lax.broadcasted_iota(jnp.int32, (BF16_ROWS, WINDOW), 0) == 0).astype(BF16)
    norms = [_rms(x_ref[p * half:(p + 1) * half, :], g_ref[...]).astype(BF16)
             for p in range(x_ref.shape[0] // half)]
    for part, hn in enumerate(norms):
        rows = slice(part * half, (part + 1) * half)
        q_ref[part] = (_dot_nt(wq_ref[...], hn) * (HEAD_DIM ** -0.5 * LOG2E)).astype(BF16)
        kc = _dot(hn, wkc_ref[...])
        k_ref[rows, :] = kc[:, :nk].astype(BF16)
        qc_ref[rows, :] = kc[:, nk:].astype(BF16)
        v_t = _dot_nt(wv_ref[...], hn).astype(BF16)
        for w in range(half // WINDOW):
            lanes = slice(w * WINDOW, (w + 1) * WINDOW)
            win = part * (half // WINDOW) + w
            for h in range(SWA_KV_HEADS):
                v_ref[win, h * V_ROWS:h * V_ROWS + HEAD_DIM, :] = v_t[
                    h * HEAD_DIM:(h + 1) * HEAD_DIM, lanes]
                v_ref[win, h * V_ROWS + HEAD_DIM:(h + 1) * V_ROWS, :] = ones_rows


def _swa_proj(x2, layer, g, w_q_t, w_kc, w_v_t):
    t = x2.shape[0]
    tm = ROW_TILE
    nk = SWA_KV_HEADS * HEAD_DIM
    nwin = tm // WINDOW
    row = lambda w: pl.BlockSpec((tm, w), lambda i: (i, 0))
    consts = [g, w_q_t, w_kc, w_v_t]
    return pl.pallas_call(
        _swa_proj_kernel,
        grid=(t // tm,),
        in_specs=[row(D_MODEL)] + [_layer_spec(c, layer) for c in consts],
        out_specs=[pl.BlockSpec((tm // SUB_TILE, MIX_WIDTH, SUB_TILE), lambda i: (i, 0, 0)),
                   row(nk),
                   pl.BlockSpec((nwin, SWA_KV_HEADS * V_ROWS, WINDOW), lambda i: (i, 0, 0)),
                   row(CROSS_WIDTH)],
        out_shape=[jax.ShapeDtypeStruct((t // SUB_TILE, MIX_WIDTH, SUB_TILE), BF16),
                   jax.ShapeDtypeStruct((t, nk), BF16),
                   jax.ShapeDtypeStruct((t // WINDOW, SWA_KV_HEADS * V_ROWS, WINDOW), BF16),
                   jax.ShapeDtypeStruct((t, CROSS_WIDTH), BF16)],
        compiler_params=_params("parallel"),
        name="swa_proj",
    )(x2, *consts)


def _swa_attn_kernel(slope_ref, sink_ref, q_ref, ko_ref, kp_ref, vo_ref, vp_ref,
                     pqo_ref, pko_ref, pkp_ref, o_ref, *, nwin):
    i = pl.program_id(1)
    w_ = WINDOW
    kcat = jnp.concatenate([kp_ref[...], ko_ref[...]], axis=0)
    pos_k = jnp.concatenate([pkp_ref[...], pko_ref[...]], axis=0)
    kj = lax.broadcasted_iota(jnp.int32, (2 * w_, w_), 0)
    qi = lax.broadcasted_iota(jnp.int32, (2 * w_, w_), 1)
    rel = w_ + qi - kj
    band = (rel >= 0) & (rel < w_)
    first_band = band & ((kj >= w_) | (i > 0))
    zeros = jnp.zeros((HEAD_DIM, w_), BF16)

    def window_keys(w):
        return slice(w * w_, (w + 2) * w_)

    def scores(w, kh):
        k_pair = kcat[window_keys(w), (kh // 2) * LANES:(kh // 2 + 1) * LANES]
        qs = []
        for g in range(SWA_GROUP):
            hq = kh * SWA_GROUP + g
            qh = q_ref[0, hq * HEAD_DIM:(hq + 1) * HEAD_DIM, w * w_:(w + 1) * w_]
            qs.append(jnp.concatenate([qh, zeros] if kh % 2 == 0 else [zeros, qh], axis=0))
        return _dot(k_pair, jnp.concatenate(qs, axis=1))

    dists = {}

    def masked_dist(w):
        if w not in dists:
            dists[w] = jnp.where(first_band if w == 0 else band,
                                 (pqo_ref[w] - pos_k[window_keys(w)]).astype(F32), MASK_DIST)
        return dists[w]

    def finish(w, kh, s3):
        dist = masked_dist(w)
        v_prev = vp_ref[0] if w == 0 else vo_ref[w - 1]
        rows = slice(kh * V_ROWS, (kh + 1) * V_ROWS)
        v_win = jnp.concatenate([v_prev[rows], vo_ref[w, rows, :]], axis=1)
        ps = []
        sink_terms = []
        for g in range(SWA_GROUP):
            hq = kh * SWA_GROUP + g
            sink = sink_ref[hq]
            s = s3[:, g * w_:(g + 1) * w_] - slope_ref[hq] * dist
            m = jnp.maximum(jnp.max(s, axis=0, keepdims=True), sink)
            ps.append(jnp.exp2(s - m).astype(BF16))
            sink_terms.append(jnp.exp2(sink - m))
        o3 = _dot(v_win, jnp.concatenate(ps, axis=1))
        for g in range(SWA_GROUP):
            hq = kh * SWA_GROUP + g
            og = o3[:, g * w_:(g + 1) * w_]
            o_ref[0, hq * HEAD_DIM:(hq + 1) * HEAD_DIM, w * w_:(w + 1) * w_] = (
                og[:HEAD_DIM] / (og[HEAD_DIM:HEAD_DIM + 1] + sink_terms[g])).astype(BF16)

    units = [(w, kh) for w in range(nwin) for kh in range(SWA_KV_HEADS)]
    pending = {}
    for idx in range(len(units) + SWA_LOOKAHEAD):
        if idx < len(units):
            pending[idx] = scores(*units[idx])
        if idx >= SWA_LOOKAHEAD:
            done = idx - SWA_LOOKAHEAD
            finish(*units[done], pending.pop(done))


def _swa_attn(slopes, sinks, q_t, k, v_t, pos_col, pos_row, batch, seq):
    rows = SWA_ROWS
    nwin = rows // WINDOW
    nsteps = seq // rows
    nblk = seq // WINDOW
    nk = SWA_KV_HEADS * HEAD_DIM
    vr = SWA_KV_HEADS * V_ROWS
    own = lambda b, i: (b * nsteps + i, 0)
    own3 = lambda b, i: (b * nsteps + i, 0, 0)
    prev = lambda b, i: (b * nblk + jnp.maximum(i * nwin - 1, 0), 0)
    prev3 = lambda b, i: (b * nblk + jnp.maximum(i * nwin - 1, 0), 0, 0)
    smem = pl.BlockSpec(memory_space=pltpu.SMEM)
    return pl.pallas_call(
        functools.partial(_swa_attn_kernel, nwin=nwin),
        grid=(batch, nsteps),
        in_specs=[smem, smem,
                  pl.BlockSpec((1, MIX_WIDTH, rows), own3),
                  pl.BlockSpec((rows, nk), own),
                  pl.BlockSpec((WINDOW, nk), prev),
                  pl.BlockSpec((nwin, vr, WINDOW), own3),
                  pl.BlockSpec((1, vr, WINDOW), prev3),
                  pl.BlockSpec((nwin, 1, WINDOW), own3),
                  pl.BlockSpec((rows, 1), own),
                  pl.BlockSpec((WINDOW, 1), prev)],
        out_specs=pl.BlockSpec((1, MIX_WIDTH, rows), own3),
        out_shape=jax.ShapeDtypeStruct((batch * nsteps, MIX_WIDTH, rows), BF16),
        compiler_params=_params("parallel", "parallel"),
        name="swa_attn",
    )(slopes, sinks, q_t, k, k, v_t, v_t, pos_row, pos_col, pos_col)


def _out_mlp_kernel(*refs, final):
    if final:
        (x_ref, mix_ref, qc_ref, km_ref, vm_ref, wo_ref, g_ref, wup_ref, wdn_ref,
         gf_ref, o_ref, acc_ref) = refs
    else:
        (x_ref, mix_ref, qc_ref, km_ref, vm_ref, wo_ref, g_ref, wup_ref, wdn_ref,
         o_ref, acc_ref) = refs
    lo = _low_half()
    n_pairs = CROSS_WIDTH // LANES
    for part in range(x_ref.shape[0] // SUB_TILE):
        rows = slice(part * SUB_TILE, (part + 1) * SUB_TILE)
        scores = []
        for pair in range(n_pairs):
            sl = slice(pair * LANES, (pair + 1) * LANES)
            qp = qc_ref[rows, sl]
            for half in range(2):
                sel = lo if half == 0 else jnp.logical_not(lo)
                scores.append(_dot_nt(jnp.where(sel, qp, jnp.zeros_like(qp)), km_ref[:, sl]))
        cross = []
        for pair in range(n_pairs):
            vp = vm_ref[:, pair * LANES:(pair + 1) * LANES]
            outs = []
            for half in range(2):
                s = scores[2 * pair + half]
                e = jnp.exp(s - jnp.max(s, axis=-1, keepdims=True))
                den = jnp.sum(e, axis=-1, keepdims=True)
                outs.append(_dot(e.astype(BF16), vp) / den)
            cross.append(jnp.where(lo, outs[0], outs[1]).astype(BF16))
        mix = mix_ref[part].astype(F32).T.astype(BF16)
        attn = jnp.concatenate([mix] + cross, axis=1)
        x1 = x_ref[rows, :] + _dot(attn, wo_ref[...])
        hn = _rms(x1, g_ref[...]).astype(BF16)
        acc_ref[...] = x1
        for c in range(D_FF // FF_CHUNK):
            cols = slice(c * FF_CHUNK, (c + 1) * FF_CHUNK)
            h = jnp.maximum(_dot(hn, wup_ref[:, cols]), 0.0)
            acc_ref[...] += _dot((h * h).astype(BF16), wdn_ref[cols, :])
        if final:
            o_ref[rows, :] = _rms(acc_ref[...], gf_ref[...])
        else:
            o_ref[rows, :] = acc_ref[...]


def _out_mlp(x2, mix, qc, memkv, layer, w_o, g, w_up, w_dn, g_final, seq):
    t = x2.shape[0]
    tm = ROW_TILE
    per_b = seq // tm
    final = g_final is not None
    row = lambda w: pl.BlockSpec((tm, w), lambda i: (i, 0))
    in_specs = [row(D_MODEL),
                pl.BlockSpec((tm // SUB_TILE, MIX_WIDTH, SUB_TILE), lambda i: (i, 0, 0)),
                row(CROSS_WIDTH),
                pl.BlockSpec((N_MEM, CROSS_WIDTH), lambda i: (i // per_b, 2 * layer)),
                pl.BlockSpec((N_MEM, CROSS_WIDTH), lambda i: (i // per_b, 2 * layer + 1)),
                _layer_spec(w_o, layer), _layer_spec(g, layer),
                _layer_spec(w_up, layer), _layer_spec(w_dn, layer)]
    args = [x2, mix, qc, memkv, memkv, w_o, g, w_up, w_dn]
    if final:
        in_specs.append(_const_spec(g_final.shape))
        args.append(g_final)
    return pl.pallas_call(
        functools.partial(_out_mlp_kernel, final=final),
        grid=(t // tm,),
        in_specs=in_specs,
        out_specs=row(D_MODEL),
        out_shape=jax.ShapeDtypeStruct((t, D_MODEL), F32),
        scratch_shapes=[pltpu.VMEM((SUB_TILE, D_MODEL), F32)],
        compiler_params=_params("parallel"),
        name="out_mlp",
    )(*args)


def _prep_mla(w_in, w_uq, w_ukv):
    n = w_in.shape[0]
    n1 = MLA_Q_RANK + MLA_KV_RANK
    w_in = w_in.astype(BF16)
    w_a = jnp.concatenate([w_in[:, :, :n1], w_in[:, :, n1 + MLA_ROPE:]], axis=2)
    w_kr_t = w_in[:, :, n1:n1 + MLA_ROPE].transpose(0, 2, 1)
    w_uq_t = w_uq.astype(BF16).transpose(0, 2, 1)
    kv = w_ukv.astype(BF16).reshape(n, MLA_KV_RANK, MLA_HEADS, 2 * HEAD_DIM)
    w_uk = jnp.pad(kv[..., :MLA_NOPE], ((0, 0), (0, 0), (0, 0), (0, QK_PAD - MLA_NOPE)))
    w_uk = w_uk.reshape(n, MLA_KV_RANK, MLA_HEADS * QK_PAD)
    w_uv_t = kv[..., MLA_NOPE:].reshape(n, MLA_KV_RANK, MIX_WIDTH).transpose(0, 2, 1)
    return w_a, w_kr_t, w_uq_t, w_uk, w_uv_t


def _prep_swa(w_in):
    nq = SWA_Q_HEADS * HEAD_DIM
    nk = SWA_KV_HEADS * HEAD_DIM
    w_in = w_in.astype(BF16)
    w_q_t = w_in[:, :, :nq].transpose(0, 2, 1)
    w_kc = jnp.concatenate([w_in[:, :, nq:nq + nk], w_in[:, :, nq + 2 * nk:]], axis=2)
    w_v_t = w_in[:, :, nq + nk:nq + 2 * nk].transpose(0, 2, 1)
    return w_q_t, w_kc, w_v_t


def kernel(x, mem, positions, attn_norm_g, mlp_norm_g, mem_norm_g, final_norm_g,
           mla_w_in, mla_q_norm_g, mla_kv_norm_g, mla_w_uq, mla_w_ukv,
           swa_w_in, swa_sinks, w_mem_kv, w_o, mlp_w_up, mlp_w_down):
    batch, seq, d = x.shape
    depth = attn_norm_g.shape[0]
    t = batch * seq
    x2 = x.reshape(t, d)

    w_mem = w_mem_kv.astype(BF16)
    w_mem = jnp.concatenate([w_mem[:, :, :CROSS_WIDTH] * (HEAD_DIM ** -0.5),
                             w_mem[:, :, CROSS_WIDTH:]], axis=2)
    memkv = _memkv(mem.reshape(batch * N_MEM, d), mem_norm_g.reshape(1, d), w_mem)

    inv = ROPE_THETA ** (-(jnp.arange(ROPE_HALF, dtype=F32) * 2.0) / MLA_ROPE)
    cos_t, sin_t = _rope_tables(positions.astype(F32).reshape(1, t), inv.reshape(ROPE_HALF, 1))

    pos_col_i = positions.reshape(t, 1)
    pos_row_i = positions.reshape(t // WINDOW, 1, WINDOW)
    slopes = 2.0 ** (-8.0 * (jnp.arange(SWA_Q_HEADS, dtype=F32) + 1.0) / SWA_Q_HEADS) * LOG2E

    g_attn = attn_norm_g.reshape(depth, 1, d)
    g_mlp = mlp_norm_g.reshape(depth, 1, d)
    mla_w = _prep_mla(mla_w_in, mla_w_uq, mla_w_ukv)
    mla_qg = mla_q_norm_g.reshape(-1, 1, MLA_Q_RANK)
    mla_kvg = mla_kv_norm_g.reshape(-1, 1, MLA_KV_RANK)
    swa_w = _prep_swa(swa_w_in)
    w_o_b = w_o.astype(BF16)
    w_up = mlp_w_up.astype(BF16)
    w_dn = mlp_w_down.astype(BF16)
    sinks = swa_sinks.astype(F32) * LOG2E
    g_attn_mla, g_attn_swa = g_attn[0::2], g_attn[1::2]

    for i in range(depth):
        j = i // 2
        is_mla = i % 2 == 0
        if is_mla:
            w_a, w_kr_t, w_uq_t, w_uk, w_uv_t = mla_w
            q_t, k, v_t, qc = _mla_proj(x2, j, g_attn_mla, w_a, w_kr_t, mla_qg, mla_kvg,
                                        w_uq_t, w_uk, w_uv_t, cos_t, sin_t)
            mix = _mla_attn(q_t, k, v_t, batch, seq)
        else:
            q_t, k, v_t, qc = _swa_proj(x2, j, g_attn_swa, *swa_w)
            mix = _swa_attn(slopes, sinks[j], q_t, k, v_t, pos_col_i, pos_row_i, batch, seq)
        g_final = final_norm_g.reshape(1, d) if i == depth - 1 else None
        x2 = _out_mlp(x2, mix, qc, memkv, i, w_o_b, g_mlp, w_up, w_dn, g_final, seq)
    return x2.reshape(batch, seq, d)
```

```python
import functools
import math

import jax
import jax.numpy as jnp
from jax import lax
from jax.experimental import pallas as pl
from jax.experimental.pallas import tpu as pltpu

F32 = jnp.float32
BF16 = jnp.bfloat16

D_MODEL = 1024
HEAD_DIM = 64
N_MEM = 256
CROSS_WIDTH = 256
MIX_WIDTH = 768
EPS = 1e-6
NEG = -1e30
LOG2E = math.log2(math.e)
MASK_DIST = 1e32
MLA_HEADS = 12
MLA_Q_RANK = 384
MLA_KV_RANK = 256
MLA_NOPE = 64
MLA_ROPE = 32
MLA_QK = MLA_NOPE + MLA_ROPE
ROPE_THETA = 10000.0
SWA_Q_HEADS = 12
SWA_KV_HEADS = 4
SWA_GROUP = 3
WINDOW = 128
D_FF = 4 * D_MODEL

LANES = 128
QK_PAD = LANES
ROPE_HALF = MLA_ROPE // 2
BF16_ROWS = 16
V_ROWS = HEAD_DIM + BF16_ROWS

SUB_TILE = 512
ROW_TILE = 2 * SUB_TILE
ATTN_TILE = SUB_TILE
FULL_BLOCKS_PER_TRIP = 14
DIAG_BLOCKS_PER_TRIP = 8
SWA_ROWS = ROW_TILE
SWA_LOOKAHEAD = 2
FF_CHUNK = 1024
VMEM_LIMIT = 48 * 1024 * 1024


def _rms(x, g):
    ms = jnp.mean(x * x, axis=-1, keepdims=True)
    return x * lax.rsqrt(ms + EPS) * g


def _dot(a, b):
    return jnp.dot(a, b, preferred_element_type=F32)


def _dot_nt(a, b):
    return lax.dot_general(a, b, (((1,), (1,)), ((), ())), preferred_element_type=F32)


def _dot_tt(w, a):
    return lax.dot_general(w, a, (((0,), (1,)), ((), ())), preferred_element_type=F32)


def _low_half():
    return lax.broadcasted_iota(jnp.int32, (1, LANES), 1) < HEAD_DIM


def _const_spec(shape):
    nd = len(shape)
    return pl.BlockSpec(shape, lambda *_: (0,) * nd, pipeline_mode=pl.Buffered(1))


def _layer_spec(stacked, layer):
    nd = stacked.ndim
    return pl.BlockSpec((None,) + stacked.shape[1:], lambda *_: (layer,) + (0,) * (nd - 1),
                        pipeline_mode=pl.Buffered(1))


def _params(*sem):
    return pltpu.CompilerParams(dimension_semantics=sem, vmem_limit_bytes=VMEM_LIMIT)


def _rope_table_kernel(pos_ref, inv_ref, cos_ref, sin_ref):
    ang = inv_ref[...] * pos_ref[...]
    cos_ref[...] = jnp.cos(ang)
    sin_ref[...] = jnp.sin(ang)


def _rope_tables(pos_row, inv_col):
    t = pos_row.shape[1]
    tn = 2048
    return pl.pallas_call(
        _rope_table_kernel,
        grid=(t // tn,),
        in_specs=[pl.BlockSpec((1, tn), lambda i: (0, i)),
                  pl.BlockSpec((ROPE_HALF, 1), lambda i: (0, 0))],
        out_specs=[pl.BlockSpec((ROPE_HALF, tn), lambda i: (0, i))] * 2,
        out_shape=[jax.ShapeDtypeStruct((ROPE_HALF, t), F32)] * 2,
        compiler_params=_params("parallel"),
        name="rope_tables",
    )(pos_row, inv_col)


def _memkv_kernel(mem_ref, g_ref, w_ref, o_ref):
    mn = _rms(mem_ref[...], g_ref[...]).astype(BF16)
    ncol = w_ref.shape[2]
    for layer in range(w_ref.shape[0]):
        o_ref[:, layer * ncol:(layer + 1) * ncol] = _dot(mn, w_ref[layer]).astype(BF16)


def _memkv(mem2, g, w_all):
    rows = mem2.shape[0]
    depth, _, ncol = w_all.shape
    return pl.pallas_call(
        _memkv_kernel,
        grid=(rows // N_MEM,),
        in_specs=[pl.BlockSpec((N_MEM, D_MODEL), lambda b: (b, 0)),
                  _const_spec((1, D_MODEL)), _const_spec(w_all.shape)],
        out_specs=pl.BlockSpec((N_MEM, depth * ncol), lambda b: (b, 0)),
        out_shape=jax.ShapeDtypeStruct((rows, depth * ncol), BF16),
        compiler_params=_params("parallel"),
        name="mem_kv",
    )(mem2, g, w_all)


def _mla_proj_kernel(x_ref, g_ref, wa_ref, wkr_ref, qg_ref, kvg_ref, wuq_ref, wuk_ref,
                     wuv_ref, cos_ref, sin_ref, q_ref, k_ref, v_ref, qc_ref):
    half = SUB_TILE
    scale = MLA_QK ** -0.5 * LOG2E
    lo = _low_half()

    def compress(rows):
        hn = _rms(x_ref[rows, :], g_ref[...]).astype(BF16)
        return hn, _dot(hn, wa_ref[...])

    def expand(part, hn, proj):
        rows = slice(part * half, (part + 1) * half)
        c_q = proj[:, :MLA_Q_RANK]
        c_kv = proj[:, MLA_Q_RANK:MLA_Q_RANK + MLA_KV_RANK]
        qc_ref[rows, :] = proj[:, MLA_Q_RANK + MLA_KV_RANK:].astype(BF16)
        cqn = _rms(c_q, qg_ref[...]).astype(BF16)
        ckvn = _rms(c_kv, kvg_ref[...]).astype(BF16)
        cos = cos_ref[:, rows]
        sin = sin_ref[:, rows]

        def rope(x1, x2):
            return x1 * cos - x2 * sin, x1 * sin + x2 * cos

        kr_t = _dot_tt(wkr_ref[...], hn)
        r1, r2 = rope(kr_t[:ROPE_HALF], kr_t[ROPE_HALF:])
        k_rope = jnp.concatenate(
            [jnp.zeros((MLA_NOPE, half), F32), r1, r2,
             jnp.zeros((QK_PAD - MLA_QK, half), F32)], axis=0).T

        k_nope = _dot(ckvn, wuk_ref[...])
        for pair in range(MLA_HEADS // 2):
            both = k_nope[:, pair * LANES:(pair + 1) * LANES]
            for parity, tile in enumerate((both, pltpu.roll(both, HEAD_DIM, 1))):
                h = 2 * pair + parity
                k_ref[rows, h * QK_PAD:(h + 1) * QK_PAD] = jnp.where(lo, tile, k_rope).astype(BF16)

        v_t = _dot_tt(wuv_ref[...], ckvn).astype(BF16)
        ones_rows = (lax.broadcasted_iota(jnp.int32, (BF16_ROWS, half), 0) == 0).astype(BF16)
        for h in range(MLA_HEADS):
            v_ref[part, h * V_ROWS:h * V_ROWS + HEAD_DIM, :] = v_t[h * HEAD_DIM:(h + 1) * HEAD_DIM]
            v_ref[part, h * V_ROWS + HEAD_DIM:(h + 1) * V_ROWS, :] = ones_rows

        q_t = _dot_tt(wuq_ref[...], cqn)
        pad = jnp.zeros((QK_PAD - MLA_QK, half), BF16)
        for h in range(MLA_HEADS):
            src = h * MLA_QK
            dst = h * QK_PAD
            r1, r2 = rope(q_t[src + MLA_NOPE:src + MLA_NOPE + ROPE_HALF],
                          q_t[src + MLA_NOPE + ROPE_HALF:src + MLA_QK])
            q_ref[part, dst:dst + MLA_NOPE, :] = (q_t[src:src + MLA_NOPE] * scale).astype(BF16)
            q_ref[part, dst + MLA_NOPE:dst + MLA_NOPE + ROPE_HALF, :] = (r1 * scale).astype(BF16)
            q_ref[part, dst + MLA_NOPE + ROPE_HALF:dst + MLA_QK, :] = (r2 * scale).astype(BF16)
            q_ref[part, dst + MLA_QK:dst + QK_PAD, :] = pad

    n_sub = x_ref.shape[0] // half
    compressed = [compress(slice(p * half, (p + 1) * half)) for p in range(n_sub)]
    for p in range(n_sub):
        expand(p, *compressed[p])


def _mla_proj(x2, layer, g, w_a, w_kr, qg, kvg, w_uq, w_uk, w_uv, cos_t, sin_t):
    t = x2.shape[0]
    tm = ROW_TILE
    qw = MLA_HEADS * QK_PAD
    row = lambda w: pl.BlockSpec((tm, w), lambda i: (i, 0))
    col = lambda r: pl.BlockSpec((r, tm), lambda i: (0, i))
    n_sub = tm // SUB_TILE
    tile = lambda r: pl.BlockSpec((n_sub, r, SUB_TILE), lambda i: (i, 0, 0))
    consts = [g, w_a, w_kr, qg, kvg, w_uq, w_uk, w_uv]
    return pl.pallas_call(
        _mla_proj_kernel,
        grid=(t // tm,),
        in_specs=[row(D_MODEL)] + [_layer_spec(c, layer) for c in consts]
                 + [col(ROPE_HALF), col(ROPE_HALF)],
        out_specs=[tile(qw), row(qw), tile(MLA_HEADS * V_ROWS), row(CROSS_WIDTH)],
        out_shape=[jax.ShapeDtypeStruct((t // SUB_TILE, qw, SUB_TILE), BF16),
                   jax.ShapeDtypeStruct((t, qw), BF16),
                   jax.ShapeDtypeStruct((t // SUB_TILE, MLA_HEADS * V_ROWS, SUB_TILE), BF16),
                   jax.ShapeDtypeStruct((t, CROSS_WIDTH), BF16)],
        compiler_params=_params("parallel"),
        name="mla_proj",
    )(x2, *consts, cos_t, sin_t)


def _mla_attn_kernel(q_ref, k_ref, v_ref, o_ref, s_ref, mb_ref, m_ref, acc_ref, bias_ref,
                     *, nt):
    t = q_ref.shape[2]
    half = t // 2
    acc_ref[...] = jnp.zeros(acc_ref.shape, F32)
    key = lax.broadcasted_iota(jnp.int32, (half, half), 0)
    qry = lax.broadcasted_iota(jnp.int32, (half, half), 1)
    bias_ref[...] = jnp.where(key <= qry, 0.0, NEG)

    def head_rows(h):
        return slice(h * QK_PAD, (h + 1) * QK_PAD)

    def scores(i, j, slot, h):
        start = pl.multiple_of(j * t, t)
        kb = k_ref[pl.ds(start, t), head_rows(h)]
        s = _dot(kb, q_ref[i, head_rows(h), :])
        s_ref[slot, h] = s
        mb_ref[slot, h] = jnp.max(s, axis=0, keepdims=True)

    def consume(i, j, slot, h):
        s = s_ref[slot, h]
        m_prev = jnp.where(j == 0, NEG, m_ref[i, h])
        m_new = jnp.maximum(m_prev, mb_ref[slot, h])
        a = jnp.exp2(m_prev - m_new)
        p = jnp.exp2(s - m_new).astype(BF16)
        rows = slice(h * V_ROWS, (h + 1) * V_ROWS)
        m_ref[i, h] = m_new
        acc_ref[i, rows, :] = a * acc_ref[i, rows, :] + _dot(v_ref[j, rows, :], p)

    def scores_diag(i, slot, h):
        start = pl.multiple_of(i * t, t)
        q = q_ref[i, head_rows(h), :]
        s_top = _dot(k_ref[pl.ds(start, half), head_rows(h)], q)
        s_bot = _dot(k_ref[pl.ds(start + half, half), head_rows(h)], q[:, half:])
        tri = bias_ref[...]
        s_left = s_top[:, :half] + tri
        s_bot = s_bot + tri
        s_ref[slot, h, :half, :half] = s_left
        s_ref[slot, h, :half, half:] = s_top[:, half:]
        s_ref[slot, h, half:, half:] = s_bot
        mb_ref[slot, h, :, :half] = jnp.max(s_left, axis=0, keepdims=True)
        mb_ref[slot, h, :, half:] = jnp.maximum(
            jnp.max(s_top[:, half:], axis=0, keepdims=True), jnp.max(s_bot, axis=0, keepdims=True))

    def consume_diag(i, slot, h):
        m_prev = jnp.where(i == 0, NEG, m_ref[i, h])
        m_new = jnp.maximum(m_prev, mb_ref[slot, h])
        a = jnp.exp2(m_prev - m_new)
        p_top = jnp.exp2(s_ref[slot, h, :half, :] - m_new).astype(BF16)
        p_bot = jnp.exp2(s_ref[slot, h, half:, half:] - m_new[:, half:]).astype(BF16)
        rows = slice(h * V_ROWS, (h + 1) * V_ROWS)
        acc = a * acc_ref[i, rows, :] + _dot(v_ref[i, rows, :half], p_top)
        acc_r = acc[:, half:] + _dot(v_ref[i, rows, half:], p_bot)
        out = slice(h * HEAD_DIM, (h + 1) * HEAD_DIM)
        o_ref[i, out, :half] = (acc[:HEAD_DIM, :half]
                                / acc[HEAD_DIM:HEAD_DIM + 1, :half]).astype(BF16)
        o_ref[i, out, half:] = (acc_r[:HEAD_DIM] / acc_r[HEAD_DIM:HEAD_DIM + 1]).astype(BF16)

    def next_full(i, j):
        wrap = j + 1 == i
        i_n = jnp.minimum(jnp.where(wrap, i + 1, i), nt - 1)
        return i_n, jnp.where(wrap, 0, j + 1)

    def full_trip(_, cur):
        for b in range(FULL_BLOCKS_PER_TRIP):
            nxt = next_full(*cur)
            for h in range(2):
                scores(*nxt, 1 - b % 2, h)
                consume(*cur, b % 2, h)
            cur = nxt
        return cur

    def diag_trip(_, i):
        for b in range(DIAG_BLOCKS_PER_TRIP):
            i_n = jnp.minimum(i + 1, nt - 1)
            for h in range(2):
                scores_diag(i_n, 1 - b % 2, h)
                consume_diag(i, b % 2, h)
            i = i + 1
        return i

    for h in range(2):
        scores(1, 0, 0, h)
    lax.fori_loop(0, nt * (nt - 1) // 2 // FULL_BLOCKS_PER_TRIP, full_trip,
                  (jnp.int32(1), jnp.int32(0)))
    for h in range(2):
        scores_diag(0, 0, h)
    lax.fori_loop(0, nt // DIAG_BLOCKS_PER_TRIP, diag_trip, jnp.int32(0))


def _mla_attn(q_t, k, v_t, batch, seq):
    t = ATTN_TILE
    nt = seq // t
    assert FULL_BLOCKS_PER_TRIP % 2 == 0 and DIAG_BLOCKS_PER_TRIP % 2 == 0
    assert (nt * (nt - 1) // 2) % FULL_BLOCKS_PER_TRIP == 0 and nt % DIAG_BLOCKS_PER_TRIP == 0
    pairs = MLA_HEADS // 2
    return pl.pallas_call(
        functools.partial(_mla_attn_kernel, nt=nt),
        grid=(batch, pairs),
        in_specs=[pl.BlockSpec((nt, 2 * QK_PAD, t), lambda b, p: (b, p, 0)),
                  pl.BlockSpec((seq, 2 * QK_PAD), lambda b, p: (b, p)),
                  pl.BlockSpec((nt, 2 * V_ROWS, t), lambda b, p: (b, p, 0))],
        out_specs=pl.BlockSpec((nt, LANES, t), lambda b, p: (b, p, 0)),
        out_shape=jax.ShapeDtypeStruct((batch * nt, MIX_WIDTH, t), BF16),
        scratch_shapes=[pltpu.VMEM((2, 2, t, t), F32), pltpu.VMEM((2, 2, 1, t), F32),
                        pltpu.VMEM((nt, 2, 1, t), F32), pltpu.VMEM((nt, 2 * V_ROWS, t), F32),
                        pltpu.VMEM((t // 2, t // 2), F32)],
        compiler_params=_params("parallel", "parallel"),
        name="mla_attn",
    )(q_t, k, v_t)


def _swa_proj_kernel(x_ref, g_ref, wq_ref, wkc_ref, wv_ref, q_ref, k_ref, v_ref, qc_ref):
    half = SUB_TILE
    nk = SWA_KV_HEADS * HEAD_DIM
    ones_rows = (lax.broadcasted_iota(jnp.int32, (BF16_ROWS, WINDOW), 0) == 0).astype(BF16)
    norms = [_rms(x_ref[p * half:(p + 1) * half, :], g_ref[...]).astype(BF16)
             for p in range(x_ref.shape[0] // half)]
    for part, hn in enumerate(norms):
        rows = slice(part * half, (part + 1) * half)
        q_ref[part] = (_dot_tt(wq_ref[...], hn) * (HEAD_DIM ** -0.5 * LOG2E)).astype(BF16)
        kc = _dot(hn, wkc_ref[...])
        k_ref[rows, :] = kc[:, :nk].astype(BF16)
        qc_ref[rows, :] = kc[:, nk:].astype(BF16)
        v_t = _dot_tt(wv_ref[...], hn).astype(BF16)
        for w in range(half // WINDOW):
            lanes = slice(w * WINDOW, (w + 1) * WINDOW)
            win = part * (half // WINDOW) + w
            for h in range(SWA_KV_HEADS):
                v_ref[win, h * V_ROWS:h * V_ROWS + HEAD_DIM, :] = v_t[
                    h * HEAD_DIM:(h + 1) * HEAD_DIM, lanes]
                v_ref[win, h * V_ROWS + HEAD_DIM:(h + 1) * V_ROWS, :] = ones_rows


def _swa_proj(x2, layer, g, w_q_t, w_kc, w_v_t):
    t = x2.shape[0]
    tm = ROW_TILE
    nk = SWA_KV_HEADS * HEAD_DIM
    nwin = tm // WINDOW
    row = lambda w: pl.BlockSpec((tm, w), lambda i: (i, 0))
    consts = [g, w_q_t, w_kc, w_v_t]
    return pl.pallas_call(
        _swa_proj_kernel,
        grid=(t // tm,),
        in_specs=[row(D_MODEL)] + [_layer_spec(c, layer) for c in consts],
        out_specs=[pl.BlockSpec((tm // SUB_TILE, MIX_WIDTH, SUB_TILE), lambda i: (i, 0, 0)),
                   row(nk),
                   pl.BlockSpec((nwin, SWA_KV_HEADS * V_ROWS, WINDOW), lambda i: (i, 0, 0)),
                   row(CROSS_WIDTH)],
        out_shape=[jax.ShapeDtypeStruct((t // SUB_TILE, MIX_WIDTH, SUB_TILE), BF16),
                   jax.ShapeDtypeStruct((t, nk), BF16),
                   jax.ShapeDtypeStruct((t // WINDOW, SWA_KV_HEADS * V_ROWS, WINDOW), BF16),
                   jax.ShapeDtypeStruct((t, CROSS_WIDTH), BF16)],
        compiler_params=_params("parallel"),
        name="swa_proj",
    )(x2, *consts)


def _swa_attn_kernel(slope_ref, sink_ref, q_ref, ko_ref, kp_ref, vo_ref, vp_ref,
                     pqo_ref, pko_ref, pkp_ref, o_ref, *, nwin):
    i = pl.program_id(1)
    w_ = WINDOW
    kcat = jnp.concatenate([kp_ref[...], ko_ref[...]], axis=0)
    pos_k = jnp.concatenate([pkp_ref[...], pko_ref[...]], axis=0)
    kj = lax.broadcasted_iota(jnp.int32, (2 * w_, w_), 0)
    qi = lax.broadcasted_iota(jnp.int32, (2 * w_, w_), 1)
    rel = w_ + qi - kj
    band = (rel >= 0) & (rel < w_)
    first_band = band & ((kj >= w_) | (i > 0))
    zeros = jnp.zeros((HEAD_DIM, w_), BF16)

    def window_keys(w):
        return slice(w * w_, (w + 2) * w_)

    def tile_lanes(w):
        tile, off = divmod(w * w_, SUB_TILE)
        return tile, slice(off, off + w_)

    def scores(w, kh):
        k_pair = kcat[window_keys(w), (kh // 2) * LANES:(kh // 2 + 1) * LANES]
        qs = []
        for g in range(SWA_GROUP):
            hq = kh * SWA_GROUP + g
            tile, lanes = tile_lanes(w)
            qh = q_ref[tile, hq * HEAD_DIM:(hq + 1) * HEAD_DIM, lanes]
            qs.append(jnp.concatenate([qh, zeros] if kh % 2 == 0 else [zeros, qh], axis=0))
        return _dot(k_pair, jnp.concatenate(qs, axis=1))

    dists = {}

    def masked_dist(w):
        if w not in dists:
            dists[w] = jnp.where(first_band if w == 0 else band,
                                 (pqo_ref[w] - pos_k[window_keys(w)]).astype(F32), MASK_DIST)
        return dists[w]

    def finish(w, kh, s3):
        dist = masked_dist(w)
        v_prev = vp_ref[0] if w == 0 else vo_ref[w - 1]
        rows = slice(kh * V_ROWS, (kh + 1) * V_ROWS)
        v_win = jnp.concatenate([v_prev[rows], vo_ref[w, rows, :]], axis=1)
        ps = []
        sink_terms = []
        for g in range(SWA_GROUP):
            hq = kh * SWA_GROUP + g
            sink = sink_ref[hq]
            s = s3[:, g * w_:(g + 1) * w_] - slope_ref[hq] * dist
            m = jnp.maximum(jnp.max(s, axis=0, keepdims=True), sink)
            ps.append(jnp.exp2(s - m).astype(BF16))
            sink_terms.append(jnp.exp2(sink - m))
        o3 = _dot(v_win, jnp.concatenate(ps, axis=1))
        tile, lanes = tile_lanes(w)
        for g in range(SWA_GROUP):
            hq = kh * SWA_GROUP + g
            og = o3[:, g * w_:(g + 1) * w_]
            o_ref[tile, hq * HEAD_DIM:(hq + 1) * HEAD_DIM, lanes] = (
                og[:HEAD_DIM] / (og[HEAD_DIM:HEAD_DIM + 1] + sink_terms[g])).astype(BF16)

    units = [(w, kh) for w in range(nwin) for kh in range(SWA_KV_HEADS)]
    pending = {}
    for idx in range(len(units) + SWA_LOOKAHEAD):
        if idx < len(units):
            pending[idx] = scores(*units[idx])
        if idx >= SWA_LOOKAHEAD:
            done = idx - SWA_LOOKAHEAD
            finish(*units[done], pending.pop(done))


def _swa_attn(slopes, sinks, q_t, k, v_t, pos_col, pos_row, batch, seq):
    rows = SWA_ROWS
    nwin = rows // WINDOW
    nsteps = seq // rows
    nblk = seq // WINDOW
    nk = SWA_KV_HEADS * HEAD_DIM
    vr = SWA_KV_HEADS * V_ROWS
    own = lambda b, i: (b * nsteps + i, 0)
    own3 = lambda b, i: (b * nsteps + i, 0, 0)
    prev = lambda b, i: (b * nblk + jnp.maximum(i * nwin - 1, 0), 0)
    prev3 = lambda b, i: (b * nblk + jnp.maximum(i * nwin - 1, 0), 0, 0)
    smem = pl.BlockSpec(memory_space=pltpu.SMEM)
    return pl.pallas_call(
        functools.partial(_swa_attn_kernel, nwin=nwin),
        grid=(batch, nsteps),
        in_specs=[smem, smem,
                  pl.BlockSpec((rows // SUB_TILE, MIX_WIDTH, SUB_TILE), own3),
                  pl.BlockSpec((rows, nk), own),
                  pl.BlockSpec((WINDOW, nk), prev),
                  pl.BlockSpec((nwin, vr, WINDOW), own3),
                  pl.BlockSpec((1, vr, WINDOW), prev3),
                  pl.BlockSpec((nwin, 1, WINDOW), own3),
                  pl.BlockSpec((rows, 1), own),
                  pl.BlockSpec((WINDOW, 1), prev)],
        out_specs=pl.BlockSpec((rows // SUB_TILE, MIX_WIDTH, SUB_TILE), own3),
        out_shape=jax.ShapeDtypeStruct((batch * seq // SUB_TILE, MIX_WIDTH, SUB_TILE), BF16),
        compiler_params=_params("parallel", "parallel"),
        name="swa_attn",
    )(slopes, sinks, q_t, k, k, v_t, v_t, pos_row, pos_col, pos_col)


def _out_mlp_kernel(*refs, final):
    if final:
        (x_ref, mix_ref, qc_ref, km_ref, vm_ref, wo_ref, g_ref, wup_ref, wdn_ref,
         gf_ref, o_ref, acc_ref) = refs
    else:
        (x_ref, mix_ref, qc_ref, km_ref, vm_ref, wo_ref, g_ref, wup_ref, wdn_ref,
         o_ref, acc_ref) = refs
    lo = _low_half()
    n_pairs = CROSS_WIDTH // LANES
    for part in range(x_ref.shape[0] // SUB_TILE):
        rows = slice(part * SUB_TILE, (part + 1) * SUB_TILE)
        scores = []
        for pair in range(n_pairs):
            sl = slice(pair * LANES, (pair + 1) * LANES)
            qp = qc_ref[rows, sl]
            for half in range(2):
                sel = lo if half == 0 else jnp.logical_not(lo)
                scores.append(_dot_nt(jnp.where(sel, qp, jnp.zeros_like(qp)), km_ref[:, sl]))
        cross = []
        for pair in range(n_pairs):
            vp = vm_ref[:, pair * LANES:(pair + 1) * LANES]
            outs = []
            for half in range(2):
                s = scores[2 * pair + half]
                e = jnp.exp(s - jnp.max(s, axis=-1, keepdims=True))
                den = jnp.sum(e, axis=-1, keepdims=True)
                outs.append(_dot(e.astype(BF16), vp) / den)
            cross.append(jnp.where(lo, outs[0], outs[1]).astype(BF16))
        mix = mix_ref[part].astype(F32).T.astype(BF16)
        attn = jnp.concatenate([mix] + cross, axis=1)
        x1 = x_ref[rows, :] + _dot(attn, wo_ref[...])
        hn = _rms(x1, g_ref[...]).astype(BF16)
        acc_ref[...] = x1
        for c in range(D_FF // FF_CHUNK):
            cols = slice(c * FF_CHUNK, (c + 1) * FF_CHUNK)
            h = jnp.maximum(_dot(hn, wup_ref[:, cols]), 0.0)
            acc_ref[...] += _dot((h * h).astype(BF16), wdn_ref[cols, :])
        if final:
            o_ref[rows, :] = _rms(acc_ref[...], gf_ref[...])
        else:
            o_ref[rows, :] = acc_ref[...]


def _out_mlp(x2, mix, qc, memkv, layer, w_o, g, w_up, w_dn, g_final, seq):
    t = x2.shape[0]
    tm = ROW_TILE
    per_b = seq // tm
    final = g_final is not None
    row = lambda w: pl.BlockSpec((tm, w), lambda i: (i, 0))
    in_specs = [row(D_MODEL),
                pl.BlockSpec((tm // SUB_TILE, MIX_WIDTH, SUB_TILE), lambda i: (i, 0, 0)),
                row(CROSS_WIDTH),
                pl.BlockSpec((N_MEM, CROSS_WIDTH), lambda i: (i // per_b, 2 * layer)),
                pl.BlockSpec((N_MEM, CROSS_WIDTH), lambda i: (i // per_b, 2 * layer + 1)),
                _layer_spec(w_o, layer), _layer_spec(g, layer),
                _layer_spec(w_up, layer), _layer_spec(w_dn, layer)]
    args = [x2, mix, qc, memkv, memkv, w_o, g, w_up, w_dn]
    if final:
        in_specs.append(_const_spec(g_final.shape))
        args.append(g_final)
    return pl.pallas_call(
        functools.partial(_out_mlp_kernel, final=final),
        grid=(t // tm,),
        in_specs=in_specs,
        out_specs=row(D_MODEL),
        out_shape=jax.ShapeDtypeStruct((t, D_MODEL), F32),
        scratch_shapes=[pltpu.VMEM((SUB_TILE, D_MODEL), F32)],
        compiler_params=_params("parallel"),
        name="out_mlp",
    )(*args)


def _prep_mla(w_in, w_uq, w_ukv):
    n = w_in.shape[0]
    n1 = MLA_Q_RANK + MLA_KV_RANK
    w_in = w_in.astype(BF16)
    w_a = jnp.concatenate([w_in[:, :, :n1], w_in[:, :, n1 + MLA_ROPE:]], axis=2)
    w_kr_t = w_in[:, :, n1:n1 + MLA_ROPE]
    w_uq_t = w_uq.astype(BF16)
    kv = w_ukv.astype(BF16).reshape(n, MLA_KV_RANK, MLA_HEADS, 2 * HEAD_DIM)
    w_uk = kv[..., :MLA_NOPE].reshape(n, MLA_KV_RANK, MLA_HEADS * MLA_NOPE)
    w_uv_t = kv[..., MLA_NOPE:].reshape(n, MLA_KV_RANK, MIX_WIDTH)
    return w_a, w_kr_t, w_uq_t, w_uk, w_uv_t


def _prep_swa(w_in):
    nq = SWA_Q_HEADS * HEAD_DIM
    nk = SWA_KV_HEADS * HEAD_DIM
    w_in = w_in.astype(BF16)
    w_q_t = w_in[:, :, :nq]
    w_kc = jnp.concatenate([w_in[:, :, nq:nq + nk], w_in[:, :, nq + 2 * nk:]], axis=2)
    w_v_t = w_in[:, :, nq + nk:nq + 2 * nk]
    return w_q_t, w_kc, w_v_t


def kernel(x, mem, positions, attn_norm_g, mlp_norm_g, mem_norm_g, final_norm_g,
           mla_w_in, mla_q_norm_g, mla_kv_norm_g, mla_w_uq, mla_w_ukv,
           swa_w_in, swa_sinks, w_mem_kv, w_o, mlp_w_up, mlp_w_down):
    batch, seq, d = x.shape
    depth = attn_norm_g.shape[0]
    t = batch * seq
    x2 = x.reshape(t, d)

    w_mem = w_mem_kv.astype(BF16)
    w_mem = jnp.concatenate([w_mem[:, :, :CROSS_WIDTH] * (HEAD_DIM ** -0.5),
                             w_mem[:, :, CROSS_WIDTH:]], axis=2)
    memkv = _memkv(mem.reshape(batch * N_MEM, d), mem_norm_g.reshape(1, d), w_mem)

    inv = ROPE_THETA ** (-(jnp.arange(ROPE_HALF, dtype=F32) * 2.0) / MLA_ROPE)
    cos_t, sin_t = _rope_tables(positions.astype(F32).reshape(1, t), inv.reshape(ROPE_HALF, 1))

    pos_col_i = positions.reshape(t, 1)
    pos_row_i = positions.reshape(t // WINDOW, 1, WINDOW)
    slopes = 2.0 ** (-8.0 * (jnp.arange(SWA_Q_HEADS, dtype=F32) + 1.0) / SWA_Q_HEADS) * LOG2E

    g_attn = attn_norm_g.reshape(depth, 1, d)
    g_mlp = mlp_norm_g.reshape(depth, 1, d)
    mla_w = _prep_mla(mla_w_in, mla_w_uq, mla_w_ukv)
    mla_qg = mla_q_norm_g.reshape(-1, 1, MLA_Q_RANK)
    mla_kvg = mla_kv_norm_g.reshape(-1, 1, MLA_KV_RANK)
    swa_w = _prep_swa(swa_w_in)
    w_o_b = w_o.astype(BF16)
    w_up = mlp_w_up.astype(BF16)
    w_dn = mlp_w_down.astype(BF16)
    sinks = swa_sinks.astype(F32) * LOG2E
    g_attn_mla, g_attn_swa = g_attn[0::2], g_attn[1::2]

    for i in range(depth):
        j = i // 2
        is_mla = i % 2 == 0
        if is_mla:
            w_a, w_kr_t, w_uq_t, w_uk, w_uv_t = mla_w
            q_t, k, v_t, qc = _mla_proj(x2, j, g_attn_mla, w_a, w_kr_t, mla_qg, mla_kvg,
                                        w_uq_t, w_uk, w_uv_t, cos_t, sin_t)
            mix = _mla_attn(q_t, k, v_t, batch, seq)
        else:
            q_t, k, v_t, qc = _swa_proj(x2, j, g_attn_swa, *swa_w)
            mix = _swa_attn(slopes, sinks[j], q_t, k, v_t, pos_col_i, pos_row_i, batch, seq)
        g_final = final_norm_g.reshape(1, d) if i == depth - 1 else None
        x2 = _out_mlp(x2, mix, qc, memkv, i, w_o_b, g_mlp, w_up, w_dn, g_final, seq)
    return x2.reshape(batch, seq, d)
```

```python
import functools
import math

import jax
import jax.numpy as jnp
from jax import lax
from jax.experimental import pallas as pl
from jax.experimental.pallas import tpu as pltpu

F32 = jnp.float32
BF16 = jnp.bfloat16

D_MODEL = 1024
HEAD_DIM = 64
N_MEM = 256
CROSS_WIDTH = 256
MIX_WIDTH = 768
EPS = 1e-6
NEG = -1e30
LOG2E = math.log2(math.e)
MASK_DIST = 1e32
MLA_HEADS = 12
MLA_Q_RANK = 384
MLA_KV_RANK = 256
MLA_NOPE = 64
MLA_ROPE = 32
MLA_QK = MLA_NOPE + MLA_ROPE
ROPE_THETA = 10000.0
SWA_Q_HEADS = 12
SWA_KV_HEADS = 4
SWA_GROUP = 3
WINDOW = 128
D_FF = 4 * D_MODEL

LANES = 128
QK_PAD = LANES
ROPE_HALF = MLA_ROPE // 2
BF16_ROWS = 16
V_ROWS = HEAD_DIM + BF16_ROWS

SUB_TILE = 512
ROW_TILE = 2 * SUB_TILE
ATTN_TILE = SUB_TILE
FULL_BLOCKS_PER_TRIP = 14
DIAG_BLOCKS_PER_TRIP = 8
SWA_ROWS = ROW_TILE
SWA_LOOKAHEAD = 2
FF_CHUNK = 1024
VMEM_LIMIT = 48 * 1024 * 1024
MLP_VMEM_LIMIT = 56 * 1024 * 1024
W_CHUNK_ROWS = 256


def _rms(x, g):
    ms = jnp.mean(x * x, axis=-1, keepdims=True)
    return x * lax.rsqrt(ms + EPS) * g


def _dot(a, b):
    return jnp.dot(a, b, preferred_element_type=F32)


def _dot_nt(a, b):
    return lax.dot_general(a, b, (((1,), (1,)), ((), ())), preferred_element_type=F32)


def _low_half():
    return lax.broadcasted_iota(jnp.int32, (1, LANES), 1) < HEAD_DIM


def _const_spec(shape):
    nd = len(shape)
    return pl.BlockSpec(shape, lambda *_: (0,) * nd, pipeline_mode=pl.Buffered(1))


def _layer_spec(stacked, layer):
    nd = stacked.ndim
    return pl.BlockSpec((None,) + stacked.shape[1:], lambda *_: (layer,) + (0,) * (nd - 1),
                        pipeline_mode=pl.Buffered(1))


def _params(*sem):
    return pltpu.CompilerParams(dimension_semantics=sem, vmem_limit_bytes=VMEM_LIMIT)


def _rope_table_kernel(pos_ref, inv_ref, cos_ref, sin_ref):
    ang = inv_ref[...] * pos_ref[...]
    cos_ref[...] = jnp.cos(ang)
    sin_ref[...] = jnp.sin(ang)


def _rope_tables(pos_row, inv_col):
    t = pos_row.shape[1]
    tn = 2048
    return pl.pallas_call(
        _rope_table_kernel,
        grid=(t // tn,),
        in_specs=[pl.BlockSpec((1, tn), lambda i: (0, i)),
                  pl.BlockSpec((ROPE_HALF, 1), lambda i: (0, 0))],
        out_specs=[pl.BlockSpec((ROPE_HALF, tn), lambda i: (0, i))] * 2,
        out_shape=[jax.ShapeDtypeStruct((ROPE_HALF, t), F32)] * 2,
        compiler_params=_params("parallel"),
        name="rope_tables",
    )(pos_row, inv_col)


def _memkv_kernel(mem_ref, g_ref, w_ref, o_ref):
    mn = _rms(mem_ref[...], g_ref[...]).astype(BF16)
    ncol = w_ref.shape[2]
    for layer in range(w_ref.shape[0]):
        o_ref[:, layer * ncol:(layer + 1) * ncol] = _dot(mn, w_ref[layer]).astype(BF16)


def _memkv(mem2, g, w_all):
    rows = mem2.shape[0]
    depth, _, ncol = w_all.shape
    return pl.pallas_call(
        _memkv_kernel,
        grid=(rows // N_MEM,),
        in_specs=[pl.BlockSpec((N_MEM, D_MODEL), lambda b: (b, 0)),
                  _const_spec((1, D_MODEL)), _const_spec(w_all.shape)],
        out_specs=pl.BlockSpec((N_MEM, depth * ncol), lambda b: (b, 0)),
        out_shape=jax.ShapeDtypeStruct((rows, depth * ncol), BF16),
        compiler_params=_params("parallel"),
        name="mem_kv",
    )(mem2, g, w_all)


def _mla_proj_kernel(x_ref, g_ref, wa_ref, wkr_ref, qg_ref, kvg_ref, wuq_ref, wuk_ref,
                     wuv_ref, cos_ref, sin_ref, q_ref, k_ref, v_ref, qc_ref):
    half = SUB_TILE
    scale = MLA_QK ** -0.5 * LOG2E
    lo = _low_half()

    def compress(rows):
        hn = _rms(x_ref[rows, :], g_ref[...]).astype(BF16)
        return hn, _dot(hn, wa_ref[...])

    def expand(part, hn, proj):
        rows = slice(part * half, (part + 1) * half)
        c_q = proj[:, :MLA_Q_RANK]
        c_kv = proj[:, MLA_Q_RANK:MLA_Q_RANK + MLA_KV_RANK]
        qc_ref[rows, :] = proj[:, MLA_Q_RANK + MLA_KV_RANK:].astype(BF16)
        cqn = _rms(c_q, qg_ref[...]).astype(BF16)
        ckvn = _rms(c_kv, kvg_ref[...]).astype(BF16)
        cos = cos_ref[:, rows]
        sin = sin_ref[:, rows]

        def rope(x1, x2):
            return x1 * cos - x2 * sin, x1 * sin + x2 * cos

        kr_t = _dot_nt(wkr_ref[...], hn)
        r1, r2 = rope(kr_t[:ROPE_HALF], kr_t[ROPE_HALF:])
        k_rope = jnp.concatenate(
            [jnp.zeros((MLA_NOPE, half), F32), r1, r2,
             jnp.zeros((QK_PAD - MLA_QK, half), F32)], axis=0).T

        k_nope = _dot(ckvn, wuk_ref[...])
        for pair in range(MLA_HEADS // 2):
            both = k_nope[:, pair * LANES:(pair + 1) * LANES]
            for parity, tile in enumerate((both, pltpu.roll(both, HEAD_DIM, 1))):
                h = 2 * pair + parity
                k_ref[rows, h * QK_PAD:(h + 1) * QK_PAD] = jnp.where(lo, tile, k_rope).astype(BF16)

        v_t = _dot_nt(wuv_ref[...], ckvn).astype(BF16)
        ones_rows = (lax.broadcasted_iota(jnp.int32, (BF16_ROWS, half), 0) == 0).astype(BF16)
        for h in range(MLA_HEADS):
            v_ref[part, h * V_ROWS:h * V_ROWS + HEAD_DIM, :] = v_t[h * HEAD_DIM:(h + 1) * HEAD_DIM]
            v_ref[part, h * V_ROWS + HEAD_DIM:(h + 1) * V_ROWS, :] = ones_rows

        q_t = _dot_nt(wuq_ref[...], cqn)
        pad = jnp.zeros((QK_PAD - MLA_QK, half), BF16)
        for h in range(MLA_HEADS):
            src = h * MLA_QK
            dst = h * QK_PAD
            r1, r2 = rope(q_t[src + MLA_NOPE:src + MLA_NOPE + ROPE_HALF],
                          q_t[src + MLA_NOPE + ROPE_HALF:src + MLA_QK])
            q_ref[part, dst:dst + MLA_NOPE, :] = (q_t[src:src + MLA_NOPE] * scale).astype(BF16)
            q_ref[part, dst + MLA_NOPE:dst + MLA_NOPE + ROPE_HALF, :] = (r1 * scale).astype(BF16)
            q_ref[part, dst + MLA_NOPE + ROPE_HALF:dst + MLA_QK, :] = (r2 * scale).astype(BF16)
            q_ref[part, dst + MLA_QK:dst + QK_PAD, :] = pad

    n_sub = x_ref.shape[0] // half
    compressed = [compress(slice(p * half, (p + 1) * half)) for p in range(n_sub)]
    for p in range(n_sub):
        expand(p, *compressed[p])


def _mla_proj(x2, layer, g, w_a, w_kr, qg, kvg, w_uq, w_uk, w_uv, cos_t, sin_t):
    t = x2.shape[0]
    tm = ROW_TILE
    qw = MLA_HEADS * QK_PAD
    row = lambda w: pl.BlockSpec((tm, w), lambda i: (i, 0))
    col = lambda r: pl.BlockSpec((r, tm), lambda i: (0, i))
    n_sub = tm // SUB_TILE
    tile = lambda r: pl.BlockSpec((n_sub, r, SUB_TILE), lambda i: (i, 0, 0))
    consts = [g, w_a, w_kr, qg, kvg, w_uq, w_uk, w_uv]
    return pl.pallas_call(
        _mla_proj_kernel,
        grid=(t // tm,),
        in_specs=[row(D_MODEL)] + [_layer_spec(c, layer) for c in consts]
                 + [col(ROPE_HALF), col(ROPE_HALF)],
        out_specs=[tile(qw), row(qw), tile(MLA_HEADS * V_ROWS), row(CROSS_WIDTH)],
        out_shape=[jax.ShapeDtypeStruct((t // SUB_TILE, qw, SUB_TILE), BF16),
                   jax.ShapeDtypeStruct((t, qw), BF16),
                   jax.ShapeDtypeStruct((t // SUB_TILE, MLA_HEADS * V_ROWS, SUB_TILE), BF16),
                   jax.ShapeDtypeStruct((t, CROSS_WIDTH), BF16)],
        compiler_params=_params("parallel"),
        name="mla_proj",
    )(x2, *consts, cos_t, sin_t)


def _mla_attn_kernel(q_ref, k_ref, v_ref, o_ref, s_ref, mb_ref, m_ref, acc_ref, bias_ref,
                     *, nt):
    t = q_ref.shape[2]
    half = t // 2
    acc_ref[...] = jnp.zeros(acc_ref.shape, F32)
    key = lax.broadcasted_iota(jnp.int32, (half, half), 0)
    qry = lax.broadcasted_iota(jnp.int32, (half, half), 1)
    bias_ref[...] = jnp.where(key <= qry, 0.0, NEG)

    def head_rows(h):
        return slice(h * QK_PAD, (h + 1) * QK_PAD)

    def scores(i, j, slot, h):
        start = pl.multiple_of(j * t, t)
        kb = k_ref[pl.ds(start, t), head_rows(h)]
        s = _dot(kb, q_ref[i, head_rows(h), :])
        s_ref[slot, h] = s
        mb_ref[slot, h] = jnp.max(s, axis=0, keepdims=True)

    def consume(i, j, slot, h):
        s = s_ref[slot, h]
        m_prev = jnp.where(j == 0, NEG, m_ref[i, h])
        m_new = jnp.maximum(m_prev, mb_ref[slot, h])
        a = jnp.exp2(m_prev - m_new)
        p = jnp.exp2(s - m_new).astype(BF16)
        rows = slice(h * V_ROWS, (h + 1) * V_ROWS)
        m_ref[i, h] = m_new
        acc_ref[i, rows, :] = a * acc_ref[i, rows, :] + _dot(v_ref[j, rows, :], p)

    def scores_diag(i, slot, h):
        start = pl.multiple_of(i * t, t)
        q = q_ref[i, head_rows(h), :]
        s_top = _dot(k_ref[pl.ds(start, half), head_rows(h)], q)
        s_bot = _dot(k_ref[pl.ds(start + half, half), head_rows(h)], q[:, half:])
        tri = bias_ref[...]
        s_left = s_top[:, :half] + tri
        s_bot = s_bot + tri
        s_ref[slot, h, :half, :half] = s_left
        s_ref[slot, h, :half, half:] = s_top[:, half:]
        s_ref[slot, h, half:, half:] = s_bot
        mb_ref[slot, h, :, :half] = jnp.max(s_left, axis=0, keepdims=True)
        mb_ref[slot, h, :, half:] = jnp.maximum(
            jnp.max(s_top[:, half:], axis=0, keepdims=True), jnp.max(s_bot, axis=0, keepdims=True))

    def consume_diag(i, slot, h):
        m_prev = jnp.where(i == 0, NEG, m_ref[i, h])
        m_new = jnp.maximum(m_prev, mb_ref[slot, h])
        a = jnp.exp2(m_prev - m_new)
        p_top = jnp.exp2(s_ref[slot, h, :half, :] - m_new).astype(BF16)
        p_bot = jnp.exp2(s_ref[slot, h, half:, half:] - m_new[:, half:]).astype(BF16)
        rows = slice(h * V_ROWS, (h + 1) * V_ROWS)
        acc = a * acc_ref[i, rows, :] + _dot(v_ref[i, rows, :half], p_top)
        acc_r = acc[:, half:] + _dot(v_ref[i, rows, half:], p_bot)
        out = slice(h * HEAD_DIM, (h + 1) * HEAD_DIM)
        o_ref[i, out, :half] = (acc[:HEAD_DIM, :half]
                                / acc[HEAD_DIM:HEAD_DIM + 1, :half]).astype(BF16)
        o_ref[i, out, half:] = (acc_r[:HEAD_DIM] / acc_r[HEAD_DIM:HEAD_DIM + 1]).astype(BF16)

    def next_full(i, j):
        wrap = j + 1 == i
        i_n = jnp.minimum(jnp.where(wrap, i + 1, i), nt - 1)
        return i_n, jnp.where(wrap, 0, j + 1)

    def full_trip(_, cur):
        for b in range(FULL_BLOCKS_PER_TRIP):
            nxt = next_full(*cur)
            for h in range(2):
                scores(*nxt, 1 - b % 2, h)
                consume(*cur, b % 2, h)
            cur = nxt
        return cur

    def diag_trip(_, i):
        for b in range(DIAG_BLOCKS_PER_TRIP):
            i_n = jnp.minimum(i + 1, nt - 1)
            for h in range(2):
                scores_diag(i_n, 1 - b % 2, h)
                consume_diag(i, b % 2, h)
            i = i + 1
        return i

    for h in range(2):
        scores(1, 0, 0, h)
    lax.fori_loop(0, nt * (nt - 1) // 2 // FULL_BLOCKS_PER_TRIP, full_trip,
                  (jnp.int32(1), jnp.int32(0)))
    for h in range(2):
        scores_diag(0, 0, h)
    lax.fori_loop(0, nt // DIAG_BLOCKS_PER_TRIP, diag_trip, jnp.int32(0))


def _mla_attn(q_t, k, v_t, batch, seq):
    t = ATTN_TILE
    nt = seq // t
    assert FULL_BLOCKS_PER_TRIP % 2 == 0 and DIAG_BLOCKS_PER_TRIP % 2 == 0
    assert (nt * (nt - 1) // 2) % FULL_BLOCKS_PER_TRIP == 0 and nt % DIAG_BLOCKS_PER_TRIP == 0
    pairs = MLA_HEADS // 2
    return pl.pallas_call(
        functools.partial(_mla_attn_kernel, nt=nt),
        grid=(batch, pairs),
        in_specs=[pl.BlockSpec((nt, 2 * QK_PAD, t), lambda b, p: (b, p, 0)),
                  pl.BlockSpec((seq, 2 * QK_PAD), lambda b, p: (b, p)),
                  pl.BlockSpec((nt, 2 * V_ROWS, t), lambda b, p: (b, p, 0))],
        out_specs=pl.BlockSpec((nt, LANES, t), lambda b, p: (b, p, 0)),
        out_shape=jax.ShapeDtypeStruct((batch * nt, MIX_WIDTH, t), BF16),
        scratch_shapes=[pltpu.VMEM((2, 2, t, t), F32), pltpu.VMEM((2, 2, 1, t), F32),
                        pltpu.VMEM((nt, 2, 1, t), F32), pltpu.VMEM((nt, 2 * V_ROWS, t), F32),
                        pltpu.VMEM((t // 2, t // 2), F32)],
        compiler_params=_params("parallel", "parallel"),
        name="mla_attn",
    )(q_t, k, v_t)


def _swa_proj_kernel(x_ref, g_ref, wq_ref, wkc_ref, wv_ref, q_ref, k_ref, v_ref, qc_ref):
    half = SUB_TILE
    nk = SWA_KV_HEADS * HEAD_DIM
    ones_rows = (lax.broadcasted_iota(jnp.int32, (BF16_ROWS, WINDOW), 0) == 0).astype(BF16)
    norms = [_rms(x_ref[p * half:(p + 1) * half, :], g_ref[...]).astype(BF16)
             for p in range(x_ref.shape[0] // half)]
    for part, hn in enumerate(norms):
        rows = slice(part * half, (part + 1) * half)
        q_ref[part] = (_dot_nt(wq_ref[...], hn) * (HEAD_DIM ** -0.5 * LOG2E)).astype(BF16)
        kc = _dot(hn, wkc_ref[...])
        k_ref[rows, :] = kc[:, :nk].astype(BF16)
        qc_ref[rows, :] = kc[:, nk:].astype(BF16)
        v_t = _dot_nt(wv_ref[...], hn).astype(BF16)
        for w in range(half // WINDOW):
            lanes = slice(w * WINDOW, (w + 1) * WINDOW)
            win = part * (half // WINDOW) + w
            for h in range(SWA_KV_HEADS):
                v_ref[win, h * V_ROWS:h * V_ROWS + HEAD_DIM, :] = v_t[
                    h * HEAD_DIM:(h + 1) * HEAD_DIM, lanes]
                v_ref[win, h * V_ROWS + HEAD_DIM:(h + 1) * V_ROWS, :] = ones_rows


def _swa_proj(x2, layer, g, w_q_t, w_kc, w_v_t):
    t = x2.shape[0]
    tm = ROW_TILE
    nk = SWA_KV_HEADS * HEAD_DIM
    nwin = tm // WINDOW
    row = lambda w: pl.BlockSpec((tm, w), lambda i: (i, 0))
    consts = [g, w_q_t, w_kc, w_v_t]
    return pl.pallas_call(
        _swa_proj_kernel,
        grid=(t // tm,),
        in_specs=[row(D_MODEL)] + [_layer_spec(c, layer) for c in consts],
        out_specs=[pl.BlockSpec((tm // SUB_TILE, MIX_WIDTH, SUB_TILE), lambda i: (i, 0, 0)),
                   row(nk),
                   pl.BlockSpec((nwin, SWA_KV_HEADS * V_ROWS, WINDOW), lambda i: (i, 0, 0)),
                   row(CROSS_WIDTH)],
        out_shape=[jax.ShapeDtypeStruct((t // SUB_TILE, MIX_WIDTH, SUB_TILE), BF16),
                   jax.ShapeDtypeStruct((t, nk), BF16),
                   jax.ShapeDtypeStruct((t // WINDOW, SWA_KV_HEADS * V_ROWS, WINDOW), BF16),
                   jax.ShapeDtypeStruct((t, CROSS_WIDTH), BF16)],
        compiler_params=_params("parallel"),
        name="swa_proj",
    )(x2, *consts)


def _swa_attn_kernel(slope_ref, sink_ref, q_ref, ko_ref, kp_ref, vo_ref, vp_ref,
                     pqo_ref, pko_ref, pkp_ref, o_ref, *, nwin):
    i = pl.program_id(1)
    w_ = WINDOW
    kcat = jnp.concatenate([kp_ref[...], ko_ref[...]], axis=0)
    pos_k = jnp.concatenate([pkp_ref[...], pko_ref[...]], axis=0)
    kj = lax.broadcasted_iota(jnp.int32, (2 * w_, w_), 0)
    qi = lax.broadcasted_iota(jnp.int32, (2 * w_, w_), 1)
    rel = w_ + qi - kj
    band = (rel >= 0) & (rel < w_)
    first_band = band & ((kj >= w_) | (i > 0))
    zeros = jnp.zeros((HEAD_DIM, w_), BF16)

    def window_keys(w):
        return slice(w * w_, (w + 2) * w_)

    def tile_lanes(w):
        tile, off = divmod(w * w_, SUB_TILE)
        return tile, slice(off, off + w_)

    def scores(w, kh):
        k_pair = kcat[window_keys(w), (kh // 2) * LANES:(kh // 2 + 1) * LANES]
        qs = []
        for g in range(SWA_GROUP):
            hq = kh * SWA_GROUP + g
            tile, lanes = tile_lanes(w)
            qh = q_ref[tile, hq * HEAD_DIM:(hq + 1) * HEAD_DIM, lanes]
            qs.append(jnp.concatenate([qh, zeros] if kh % 2 == 0 else [zeros, qh], axis=0))
        return _dot(k_pair, jnp.concatenate(qs, axis=1))

    dists = {}

    def masked_dist(w):
        if w not in dists:
            dists[w] = jnp.where(first_band if w == 0 else band,
                                 (pqo_ref[w] - pos_k[window_keys(w)]).astype(F32), MASK_DIST)
        return dists[w]

    def finish(w, kh, s3):
        dist = masked_dist(w)
        v_prev = vp_ref[0] if w == 0 else vo_ref[w - 1]
        rows = slice(kh * V_ROWS, (kh + 1) * V_ROWS)
        v_win = jnp.concatenate([v_prev[rows], vo_ref[w, rows, :]], axis=1)
        ps = []
        sink_terms = []
        for g in range(SWA_GROUP):
            hq = kh * SWA_GROUP + g
            sink = sink_ref[hq]
            s = s3[:, g * w_:(g + 1) * w_] - slope_ref[hq] * dist
            m = jnp.maximum(jnp.max(s, axis=0, keepdims=True), sink)
            ps.append(jnp.exp2(s - m).astype(BF16))
            sink_terms.append(jnp.exp2(sink - m))
        o3 = _dot(v_win, jnp.concatenate(ps, axis=1))
        tile, lanes = tile_lanes(w)
        for g in range(SWA_GROUP):
            hq = kh * SWA_GROUP + g
            og = o3[:, g * w_:(g + 1) * w_]
            o_ref[tile, hq * HEAD_DIM:(hq + 1) * HEAD_DIM, lanes] = (
                og[:HEAD_DIM] / (og[HEAD_DIM:HEAD_DIM + 1] + sink_terms[g])).astype(BF16)

    units = [(w, kh) for w in range(nwin) for kh in range(SWA_KV_HEADS)]
    pending = {}
    for idx in range(len(units) + SWA_LOOKAHEAD):
        if idx < len(units):
            pending[idx] = scores(*units[idx])
        if idx >= SWA_LOOKAHEAD:
            done = idx - SWA_LOOKAHEAD
            finish(*units[done], pending.pop(done))


def _swa_attn(slopes, sinks, q_t, k, v_t, pos_col, pos_row, batch, seq):
    rows = SWA_ROWS
    nwin = rows // WINDOW
    nsteps = seq // rows
    nblk = seq // WINDOW
    nk = SWA_KV_HEADS * HEAD_DIM
    vr = SWA_KV_HEADS * V_ROWS
    own = lambda b, i: (b * nsteps + i, 0)
    own3 = lambda b, i: (b * nsteps + i, 0, 0)
    prev = lambda b, i: (b * nblk + jnp.maximum(i * nwin - 1, 0), 0)
    prev3 = lambda b, i: (b * nblk + jnp.maximum(i * nwin - 1, 0), 0, 0)
    smem = pl.BlockSpec(memory_space=pltpu.SMEM)
    return pl.pallas_call(
        functools.partial(_swa_attn_kernel, nwin=nwin),
        grid=(batch, nsteps),
        in_specs=[smem, smem,
                  pl.BlockSpec((rows // SUB_TILE, MIX_WIDTH, SUB_TILE), own3),
                  pl.BlockSpec((rows, nk), own),
                  pl.BlockSpec((WINDOW, nk), prev),
                  pl.BlockSpec((nwin, vr, WINDOW), own3),
                  pl.BlockSpec((1, vr, WINDOW), prev3),
                  pl.BlockSpec((nwin, 1, WINDOW), own3),
                  pl.BlockSpec((rows, 1), own),
                  pl.BlockSpec((WINDOW, 1), prev)],
        out_specs=pl.BlockSpec((rows // SUB_TILE, MIX_WIDTH, SUB_TILE), own3),
        out_shape=jax.ShapeDtypeStruct((batch * seq // SUB_TILE, MIX_WIDTH, SUB_TILE), BF16),
        compiler_params=_params("parallel", "parallel"),
        name="swa_attn",
    )(slopes, sinks, q_t, k, k, v_t, v_t, pos_row, pos_col, pos_col)


def _out_mlp_kernel(*refs, final, layer):
    (x_ref, mix_ref, qc_ref, km_ref, vm_ref, wo_ref, g_ref, wup_hbm, wdn_hbm) = refs[:9]
    gf_ref = refs[9] if final else None
    o_ref, acc_ref, wup_ref, wdn_ref, stage_ref, sem = refs[-6:]

    cols_per_row_block = D_FF // D_MODEL
    n_up = (D_MODEL // W_CHUNK_ROWS) * cols_per_row_block
    n_chunks = n_up + D_FF // W_CHUNK_ROWS

    def chunk(idx):
        if idx < n_up:
            r, c = divmod(idx, cols_per_row_block)
            rows = slice(r * W_CHUNK_ROWS, (r + 1) * W_CHUNK_ROWS)
            cols = slice(c * D_MODEL, (c + 1) * D_MODEL)
            return (wup_hbm.at[layer, rows, cols], wup_ref.at[rows, cols])
        r = idx - n_up
        rows = slice(r * W_CHUNK_ROWS, (r + 1) * W_CHUNK_ROWS)
        return (wdn_hbm.at[layer, rows, :], wdn_ref.at[rows, :])

    def fetch(idx):
        slot = idx % 2
        return pltpu.make_async_copy(chunk(idx)[0], stage_ref.at[slot], sem.at[slot])

    @pl.when(pl.program_id(0) == 0)
    def _():
        fetch(0).start()
        for idx in range(n_chunks):
            if idx + 1 < n_chunks:
                fetch(idx + 1).start()
            fetch(idx).wait()
            chunk(idx)[1][...] = stage_ref[idx % 2].astype(BF16)

    lo = _low_half()
    n_pairs = CROSS_WIDTH // LANES
    for part in range(x_ref.shape[0] // SUB_TILE):
        rows = slice(part * SUB_TILE, (part + 1) * SUB_TILE)
        scores = []
        for pair in range(n_pairs):
            sl = slice(pair * LANES, (pair + 1) * LANES)
            qp = qc_ref[rows, sl]
            for half in range(2):
                sel = lo if half == 0 else jnp.logical_not(lo)
                scores.append(_dot_nt(jnp.where(sel, qp, jnp.zeros_like(qp)), km_ref[:, sl]))
        cross = []
        for pair in range(n_pairs):
            vp = vm_ref[:, pair * LANES:(pair + 1) * LANES]
            outs = []
            for half in range(2):
                s = scores[2 * pair + half]
                e = jnp.exp(s - jnp.max(s, axis=-1, keepdims=True))
                den = jnp.sum(e, axis=-1, keepdims=True)
                outs.append(_dot(e.astype(BF16), vp) / den)
            cross.append(jnp.where(lo, outs[0], outs[1]).astype(BF16))
        mix = mix_ref[part].astype(F32).T.astype(BF16)
        attn = jnp.concatenate([mix] + cross, axis=1)
        x1 = x_ref[rows, :] + _dot(attn, wo_ref[...])
        hn = _rms(x1, g_ref[...]).astype(BF16)
        acc_ref[...] = x1
        for c in range(D_FF // FF_CHUNK):
            cols = slice(c * FF_CHUNK, (c + 1) * FF_CHUNK)
            h = jnp.maximum(_dot(hn, wup_ref[:, cols]), 0.0)
            acc_ref[...] += _dot((h * h).astype(BF16), wdn_ref[cols, :])
        if final:
            o_ref[rows, :] = _rms(acc_ref[...], gf_ref[...])
        else:
            o_ref[rows, :] = acc_ref[...]


def _out_mlp(x2, mix, qc, memkv, layer, w_o, g, w_up, w_dn, g_final, seq):
    t = x2.shape[0]
    tm = ROW_TILE
    per_b = seq // tm
    final = g_final is not None
    row = lambda w: pl.BlockSpec((tm, w), lambda i: (i, 0))
    in_specs = [row(D_MODEL),
                pl.BlockSpec((tm // SUB_TILE, MIX_WIDTH, SUB_TILE), lambda i: (i, 0, 0)),
                row(CROSS_WIDTH),
                pl.BlockSpec((N_MEM, CROSS_WIDTH), lambda i: (i // per_b, 2 * layer)),
                pl.BlockSpec((N_MEM, CROSS_WIDTH), lambda i: (i // per_b, 2 * layer + 1)),
                _layer_spec(w_o, layer), _layer_spec(g, layer),
                pl.BlockSpec(memory_space=pl.ANY), pl.BlockSpec(memory_space=pl.ANY)]
    args = [x2, mix, qc, memkv, memkv, w_o, g, w_up, w_dn]
    if final:
        in_specs.append(_const_spec(g_final.shape))
        args.append(g_final)
    return pl.pallas_call(
        functools.partial(_out_mlp_kernel, final=final, layer=layer),
        grid=(t // tm,),
        in_specs=in_specs,
        out_specs=row(D_MODEL),
        out_shape=jax.ShapeDtypeStruct((t, D_MODEL), F32),
        scratch_shapes=[pltpu.VMEM((SUB_TILE, D_MODEL), F32),
                        pltpu.VMEM((D_MODEL, D_FF), BF16), pltpu.VMEM((D_FF, D_MODEL), BF16),
                        pltpu.VMEM((2, W_CHUNK_ROWS, D_MODEL), F32),
                        pltpu.SemaphoreType.DMA((2,))],
        compiler_params=pltpu.CompilerParams(dimension_semantics=("arbitrary",),
                                             vmem_limit_bytes=MLP_VMEM_LIMIT),
        name="out_mlp",
    )(*args)


def _prep_mla(w_in, w_uq, w_ukv):
    n = w_in.shape[0]
    n1 = MLA_Q_RANK + MLA_KV_RANK
    w_in = w_in.astype(BF16)
    w_a = jnp.concatenate([w_in[:, :, :n1], w_in[:, :, n1 + MLA_ROPE:]], axis=2)
    w_kr_t = w_in[:, :, n1:n1 + MLA_ROPE].transpose(0, 2, 1)
    w_uq_t = w_uq.astype(BF16).transpose(0, 2, 1)
    kv = w_ukv.astype(BF16).reshape(n, MLA_KV_RANK, MLA_HEADS, 2 * HEAD_DIM)
    w_uk = kv[..., :MLA_NOPE].reshape(n, MLA_KV_RANK, MLA_HEADS * MLA_NOPE)
    w_uv_t = kv[..., MLA_NOPE:].reshape(n, MLA_KV_RANK, MIX_WIDTH).transpose(0, 2, 1)
    return w_a, w_kr_t, w_uq_t, w_uk, w_uv_t


def _prep_swa(w_in):
    nq = SWA_Q_HEADS * HEAD_DIM
    nk = SWA_KV_HEADS * HEAD_DIM
    w_in = w_in.astype(BF16)
    w_q_t = w_in[:, :, :nq].transpose(0, 2, 1)
    w_kc = jnp.concatenate([w_in[:, :, nq:nq + nk], w_in[:, :, nq + 2 * nk:]], axis=2)
    w_v_t = w_in[:, :, nq + nk:nq + 2 * nk].transpose(0, 2, 1)
    return w_q_t, w_kc, w_v_t


def kernel(x, mem, positions, attn_norm_g, mlp_norm_g, mem_norm_g, final_norm_g,
           mla_w_in, mla_q_norm_g, mla_kv_norm_g, mla_w_uq, mla_w_ukv,
           swa_w_in, swa_sinks, w_mem_kv, w_o, mlp_w_up, mlp_w_down):
    batch, seq, d = x.shape
    depth = attn_norm_g.shape[0]
    t = batch * seq
    x2 = x.reshape(t, d)

    w_mem = w_mem_kv.astype(BF16)
    w_mem = jnp.concatenate([w_mem[:, :, :CROSS_WIDTH] * (HEAD_DIM ** -0.5),
                             w_mem[:, :, CROSS_WIDTH:]], axis=2)
    memkv = _memkv(mem.reshape(batch * N_MEM, d), mem_norm_g.reshape(1, d), w_mem)

    inv = ROPE_THETA ** (-(jnp.arange(ROPE_HALF, dtype=F32) * 2.0) / MLA_ROPE)
    cos_t, sin_t = _rope_tables(positions.astype(F32).reshape(1, t), inv.reshape(ROPE_HALF, 1))

    pos_col_i = positions.reshape(t, 1)
    pos_row_i = positions.reshape(t // WINDOW, 1, WINDOW)
    slopes = 2.0 ** (-8.0 * (jnp.arange(SWA_Q_HEADS, dtype=F32) + 1.0) / SWA_Q_HEADS) * LOG2E

    g_attn = attn_norm_g.reshape(depth, 1, d)
    g_mlp = mlp_norm_g.reshape(depth, 1, d)
    mla_w = _prep_mla(mla_w_in, mla_w_uq, mla_w_ukv)
    mla_qg = mla_q_norm_g.reshape(-1, 1, MLA_Q_RANK)
    mla_kvg = mla_kv_norm_g.reshape(-1, 1, MLA_KV_RANK)
    swa_w = _prep_swa(swa_w_in)
    w_o_b = w_o.astype(BF16)
    w_up = mlp_w_up.astype(F32)
    w_dn = mlp_w_down.astype(F32)
    sinks = swa_sinks.astype(F32) * LOG2E
    g_attn_mla, g_attn_swa = g_attn[0::2], g_attn[1::2]

    for i in range(depth):
        j = i // 2
        is_mla = i % 2 == 0
        if is_mla:
            w_a, w_kr_t, w_uq_t, w_uk, w_uv_t = mla_w
            q_t, k, v_t, qc = _mla_proj(x2, j, g_attn_mla, w_a, w_kr_t, mla_qg, mla_kvg,
                                        w_uq_t, w_uk, w_uv_t, cos_t, sin_t)
            mix = _mla_attn(q_t, k, v_t, batch, seq)
        else:
            q_t, k, v_t, qc = _swa_proj(x2, j, g_attn_swa, *swa_w)
            mix = _swa_attn(slopes, sinks[j], q_t, k, v_t, pos_col_i, pos_row_i, batch, seq)
        g_final = final_norm_g.reshape(1, d) if i == depth - 1 else None
        x2 = _out_mlp(x2, mix, qc, memkv, i, w_o_b, g_mlp, w_up, w_dn, g_final, seq)
    return x2.reshape(batch, seq, d)
```

```python
import functools
import math

import jax
import jax.numpy as jnp
from jax import lax
from jax.experimental import pallas as pl
from jax.experimental.pallas import tpu as pltpu

F32 = jnp.float32
BF16 = jnp.bfloat16

D_MODEL = 1024
HEAD_DIM = 64
N_MEM = 256
CROSS_WIDTH = 256
MIX_WIDTH = 768
EPS = 1e-6
NEG = -1e30
LOG2E = math.log2(math.e)
MASK_DIST = 1e32
MLA_HEADS = 12
MLA_Q_RANK = 384
MLA_KV_RANK = 256
MLA_NOPE = 64
MLA_ROPE = 32
MLA_QK = MLA_NOPE + MLA_ROPE
ROPE_THETA = 10000.0
SWA_Q_HEADS = 12
SWA_KV_HEADS = 4
SWA_GROUP = 3
WINDOW = 128
D_FF = 4 * D_MODEL

LANES = 128
QK_PAD = LANES
ROPE_HALF = MLA_ROPE // 2
BF16_ROWS = 16
V_ROWS = HEAD_DIM + BF16_ROWS

SUB_TILE = 512
ROW_TILE = 2 * SUB_TILE
ATTN_TILE = SUB_TILE
FULL_BLOCKS_PER_TRIP = 14
DIAG_BLOCKS_PER_TRIP = 8
SWA_ROWS = ROW_TILE
SWA_LOOKAHEAD = 2
FF_CHUNK = 1024
VMEM_LIMIT = 48 * 1024 * 1024


def _rms(x, g):
    ms = jnp.mean(x * x, axis=-1, keepdims=True)
    return x * lax.rsqrt(ms + EPS) * g


def _dot(a, b):
    return jnp.dot(a, b, preferred_element_type=F32)


def _dot_nt(a, b):
    return lax.dot_general(a, b, (((1,), (1,)), ((), ())), preferred_element_type=F32)


def _low_half():
    return lax.broadcasted_iota(jnp.int32, (1, LANES), 1) < HEAD_DIM


def _const_spec(shape):
    nd = len(shape)
    return pl.BlockSpec(shape, lambda *_: (0,) * nd, pipeline_mode=pl.Buffered(1))


def _layer_spec(stacked, layer):
    nd = stacked.ndim
    return pl.BlockSpec((None,) + stacked.shape[1:], lambda *_: (layer,) + (0,) * (nd - 1),
                        pipeline_mode=pl.Buffered(1))


def _mlp_weight_cast_specs(w_up, w_dn, layer, n_steps):
    d, ff = w_up.shape[1:]
    up_rows, dn_rows = d // n_steps, ff // n_steps
    in_specs = [pl.BlockSpec((None, up_rows, ff), lambda i: (layer, i, 0)),
                pl.BlockSpec((None, dn_rows, d), lambda i: (layer, i, 0))]
    out_specs = [pl.BlockSpec((up_rows, ff), lambda i: (i, 0)),
                 pl.BlockSpec((dn_rows, d), lambda i: (i, 0))]
    out_shape = [jax.ShapeDtypeStruct((d, ff), BF16), jax.ShapeDtypeStruct((ff, d), BF16)]
    return in_specs, out_specs, out_shape


def _params(*sem):
    return pltpu.CompilerParams(dimension_semantics=sem, vmem_limit_bytes=VMEM_LIMIT)


def _rope_table_kernel(pos_ref, inv_ref, cos_ref, sin_ref):
    ang = inv_ref[...] * pos_ref[...]
    cos_ref[...] = jnp.cos(ang)
    sin_ref[...] = jnp.sin(ang)


def _rope_tables(pos_row, inv_col):
    t = pos_row.shape[1]
    tn = 2048
    return pl.pallas_call(
        _rope_table_kernel,
        grid=(t // tn,),
        in_specs=[pl.BlockSpec((1, tn), lambda i: (0, i)),
                  pl.BlockSpec((ROPE_HALF, 1), lambda i: (0, 0))],
        out_specs=[pl.BlockSpec((ROPE_HALF, tn), lambda i: (0, i))] * 2,
        out_shape=[jax.ShapeDtypeStruct((ROPE_HALF, t), F32)] * 2,
        compiler_params=_params("parallel"),
        name="rope_tables",
    )(pos_row, inv_col)


def _memkv_kernel(mem_ref, g_ref, w_ref, o_ref):
    mn = _rms(mem_ref[...], g_ref[...]).astype(BF16)
    ncol = w_ref.shape[2]
    for layer in range(w_ref.shape[0]):
        o_ref[:, layer * ncol:(layer + 1) * ncol] = _dot(mn, w_ref[layer]).astype(BF16)


def _memkv(mem2, g, w_all):
    rows = mem2.shape[0]
    depth, _, ncol = w_all.shape
    return pl.pallas_call(
        _memkv_kernel,
        grid=(rows // N_MEM,),
        in_specs=[pl.BlockSpec((N_MEM, D_MODEL), lambda b: (b, 0)),
                  _const_spec((1, D_MODEL)), _const_spec(w_all.shape)],
        out_specs=pl.BlockSpec((N_MEM, depth * ncol), lambda b: (b, 0)),
        out_shape=jax.ShapeDtypeStruct((rows, depth * ncol), BF16),
        compiler_params=_params("parallel"),
        name="mem_kv",
    )(mem2, g, w_all)


def _mla_proj_kernel(x_ref, g_ref, wa_ref, wkr_ref, qg_ref, kvg_ref, wuq_ref, wuk_ref,
                     wuv_ref, cos_ref, sin_ref, wupf_ref, wdnf_ref,
                     q_ref, k_ref, v_ref, qc_ref, wup_ref, wdn_ref):
    wup_ref[...] = wupf_ref[...].astype(BF16)
    wdn_ref[...] = wdnf_ref[...].astype(BF16)
    half = SUB_TILE
    scale = MLA_QK ** -0.5 * LOG2E
    lo = _low_half()

    def compress(rows):
        hn = _rms(x_ref[rows, :], g_ref[...]).astype(BF16)
        return hn, _dot(hn, wa_ref[...])

    def expand(part, hn, proj):
        rows = slice(part * half, (part + 1) * half)
        c_q = proj[:, :MLA_Q_RANK]
        c_kv = proj[:, MLA_Q_RANK:MLA_Q_RANK + MLA_KV_RANK]
        qc_ref[rows, :] = proj[:, MLA_Q_RANK + MLA_KV_RANK:].astype(BF16)
        cqn = _rms(c_q, qg_ref[...]).astype(BF16)
        ckvn = _rms(c_kv, kvg_ref[...]).astype(BF16)
        cos = cos_ref[:, rows]
        sin = sin_ref[:, rows]

        def rope(x1, x2):
            return x1 * cos - x2 * sin, x1 * sin + x2 * cos

        kr_t = _dot_nt(wkr_ref[...], hn)
        r1, r2 = rope(kr_t[:ROPE_HALF], kr_t[ROPE_HALF:])
        k_rope = jnp.concatenate(
            [jnp.zeros((MLA_NOPE, half), F32), r1, r2,
             jnp.zeros((QK_PAD - MLA_QK, half), F32)], axis=0).T

        k_nope = _dot(ckvn, wuk_ref[...])
        for pair in range(MLA_HEADS // 2):
            both = k_nope[:, pair * LANES:(pair + 1) * LANES]
            for parity, tile in enumerate((both, pltpu.roll(both, HEAD_DIM, 1))):
                h = 2 * pair + parity
                k_ref[rows, h * QK_PAD:(h + 1) * QK_PAD] = jnp.where(lo, tile, k_rope).astype(BF16)

        v_t = _dot_nt(wuv_ref[...], ckvn).astype(BF16)
        ones_rows = (lax.broadcasted_iota(jnp.int32, (BF16_ROWS, half), 0) == 0).astype(BF16)
        for h in range(MLA_HEADS):
            v_ref[part, h * V_ROWS:h * V_ROWS + HEAD_DIM, :] = v_t[h * HEAD_DIM:(h + 1) * HEAD_DIM]
            v_ref[part, h * V_ROWS + HEAD_DIM:(h + 1) * V_ROWS, :] = ones_rows

        q_t = _dot_nt(wuq_ref[...], cqn)
        pad = jnp.zeros((QK_PAD - MLA_QK, half), BF16)
        for h in range(MLA_HEADS):
            src = h * MLA_QK
            dst = h * QK_PAD
            r1, r2 = rope(q_t[src + MLA_NOPE:src + MLA_NOPE + ROPE_HALF],
                          q_t[src + MLA_NOPE + ROPE_HALF:src + MLA_QK])
            q_ref[part, dst:dst + MLA_NOPE, :] = (q_t[src:src + MLA_NOPE] * scale).astype(BF16)
            q_ref[part, dst + MLA_NOPE:dst + MLA_NOPE + ROPE_HALF, :] = (r1 * scale).astype(BF16)
            q_ref[part, dst + MLA_NOPE + ROPE_HALF:dst + MLA_QK, :] = (r2 * scale).astype(BF16)
            q_ref[part, dst + MLA_QK:dst + QK_PAD, :] = pad

    n_sub = x_ref.shape[0] // half
    compressed = [compress(slice(p * half, (p + 1) * half)) for p in range(n_sub)]
    for p in range(n_sub):
        expand(p, *compressed[p])


def _mla_proj(x2, layer, g, w_a, w_kr, qg, kvg, w_uq, w_uk, w_uv, cos_t, sin_t,
              w_up, w_dn, mlp_layer):
    t = x2.shape[0]
    tm = ROW_TILE
    qw = MLA_HEADS * QK_PAD
    row = lambda w: pl.BlockSpec((tm, w), lambda i: (i, 0))
    col = lambda r: pl.BlockSpec((r, tm), lambda i: (0, i))
    n_sub = tm // SUB_TILE
    tile = lambda r: pl.BlockSpec((n_sub, r, SUB_TILE), lambda i: (i, 0, 0))
    consts = [g, w_a, w_kr, qg, kvg, w_uq, w_uk, w_uv]
    cast_in, cast_out, cast_shape = _mlp_weight_cast_specs(w_up, w_dn, mlp_layer, t // tm)
    return pl.pallas_call(
        _mla_proj_kernel,
        grid=(t // tm,),
        in_specs=[row(D_MODEL)] + [_layer_spec(c, layer) for c in consts]
                 + [col(ROPE_HALF), col(ROPE_HALF)] + cast_in,
        out_specs=[tile(qw), row(qw), tile(MLA_HEADS * V_ROWS), row(CROSS_WIDTH)] + cast_out,
        out_shape=[jax.ShapeDtypeStruct((t // SUB_TILE, qw, SUB_TILE), BF16),
                   jax.ShapeDtypeStruct((t, qw), BF16),
                   jax.ShapeDtypeStruct((t // SUB_TILE, MLA_HEADS * V_ROWS, SUB_TILE), BF16),
                   jax.ShapeDtypeStruct((t, CROSS_WIDTH), BF16)] + cast_shape,
        compiler_params=_params("parallel"),
        name="mla_proj",
    )(x2, *consts, cos_t, sin_t, w_up, w_dn)


def _mla_attn_kernel(q_ref, k_ref, v_ref, o_ref, s_ref, mb_ref, m_ref, acc_ref, bias_ref,
                     *, nt):
    t = q_ref.shape[2]
    half = t // 2
    acc_ref[...] = jnp.zeros(acc_ref.shape, F32)
    key = lax.broadcasted_iota(jnp.int32, (half, half), 0)
    qry = lax.broadcasted_iota(jnp.int32, (half, half), 1)
    bias_ref[...] = jnp.where(key <= qry, 0.0, NEG)

    def head_rows(h):
        return slice(h * QK_PAD, (h + 1) * QK_PAD)

    def scores(i, j, slot, h):
        start = pl.multiple_of(j * t, t)
        kb = k_ref[pl.ds(start, t), head_rows(h)]
        s = _dot(kb, q_ref[i, head_rows(h), :])
        s_ref[slot, h] = s
        mb_ref[slot, h] = jnp.max(s, axis=0, keepdims=True)

    def consume(i, j, slot, h):
        s = s_ref[slot, h]
        m_prev = jnp.where(j == 0, NEG, m_ref[i, h])
        m_new = jnp.maximum(m_prev, mb_ref[slot, h])
        a = jnp.exp2(m_prev - m_new)
        p = jnp.exp2(s - m_new).astype(BF16)
        rows = slice(h * V_ROWS, (h + 1) * V_ROWS)
        m_ref[i, h] = m_new
        acc_ref[i, rows, :] = a * acc_ref[i, rows, :] + _dot(v_ref[j, rows, :], p)

    def scores_diag(i, slot, h):
        start = pl.multiple_of(i * t, t)
        q = q_ref[i, head_rows(h), :]
        s_top = _dot(k_ref[pl.ds(start, half), head_rows(h)], q)
        s_bot = _dot(k_ref[pl.ds(start + half, half), head_rows(h)], q[:, half:])
        tri = bias_ref[...]
        s_left = s_top[:, :half] + tri
        s_bot = s_bot + tri
        s_ref[slot, h, :half, :half] = s_left
        s_ref[slot, h, :half, half:] = s_top[:, half:]
        s_ref[slot, h, half:, half:] = s_bot
        mb_ref[slot, h, :, :half] = jnp.max(s_left, axis=0, keepdims=True)
        mb_ref[slot, h, :, half:] = jnp.maximum(
            jnp.max(s_top[:, half:], axis=0, keepdims=True), jnp.max(s_bot, axis=0, keepdims=True))

    def consume_diag(i, slot, h):
        m_prev = jnp.where(i == 0, NEG, m_ref[i, h])
        m_new = jnp.maximum(m_prev, mb_ref[slot, h])
        a = jnp.exp2(m_prev - m_new)
        p_top = jnp.exp2(s_ref[slot, h, :half, :] - m_new).astype(BF16)
        p_bot = jnp.exp2(s_ref[slot, h, half:, half:] - m_new[:, half:]).astype(BF16)
        rows = slice(h * V_ROWS, (h + 1) * V_ROWS)
        acc = a * acc_ref[i, rows, :] + _dot(v_ref[i, rows, :half], p_top)
        acc_r = acc[:, half:] + _dot(v_ref[i, rows, half:], p_bot)
        out = slice(h * HEAD_DIM, (h + 1) * HEAD_DIM)
        o_ref[i, out, :half] = (acc[:HEAD_DIM, :half]
                                / acc[HEAD_DIM:HEAD_DIM + 1, :half]).astype(BF16)
        o_ref[i, out, half:] = (acc_r[:HEAD_DIM] / acc_r[HEAD_DIM:HEAD_DIM + 1]).astype(BF16)

    def next_full(i, j):
        wrap = j + 1 == i
        i_n = jnp.minimum(jnp.where(wrap, i + 1, i), nt - 1)
        return i_n, jnp.where(wrap, 0, j + 1)

    def full_trip(_, cur):
        for b in range(FULL_BLOCKS_PER_TRIP):
            nxt = next_full(*cur)
            for h in range(2):
                scores(*nxt, 1 - b % 2, h)
                consume(*cur, b % 2, h)
            cur = nxt
        return cur

    def diag_trip(_, i):
        for b in range(DIAG_BLOCKS_PER_TRIP):
            i_n = jnp.minimum(i + 1, nt - 1)
            for h in range(2):
                scores_diag(i_n, 1 - b % 2, h)
                consume_diag(i, b % 2, h)
            i = i + 1
        return i

    for h in range(2):
        scores(1, 0, 0, h)
    lax.fori_loop(0, nt * (nt - 1) // 2 // FULL_BLOCKS_PER_TRIP, full_trip,
                  (jnp.int32(1), jnp.int32(0)))
    for h in range(2):
        scores_diag(0, 0, h)
    lax.fori_loop(0, nt // DIAG_BLOCKS_PER_TRIP, diag_trip, jnp.int32(0))


def _mla_attn(q_t, k, v_t, batch, seq):
    t = ATTN_TILE
    nt = seq // t
    assert FULL_BLOCKS_PER_TRIP % 2 == 0 and DIAG_BLOCKS_PER_TRIP % 2 == 0
    assert (nt * (nt - 1) // 2) % FULL_BLOCKS_PER_TRIP == 0 and nt % DIAG_BLOCKS_PER_TRIP == 0
    pairs = MLA_HEADS // 2
    return pl.pallas_call(
        functools.partial(_mla_attn_kernel, nt=nt),
        grid=(batch, pairs),
        in_specs=[pl.BlockSpec((nt, 2 * QK_PAD, t), lambda b, p: (b, p, 0)),
                  pl.BlockSpec((seq, 2 * QK_PAD), lambda b, p: (b, p)),
                  pl.BlockSpec((nt, 2 * V_ROWS, t), lambda b, p: (b, p, 0))],
        out_specs=pl.BlockSpec((nt, LANES, t), lambda b, p: (b, p, 0)),
        out_shape=jax.ShapeDtypeStruct((batch * nt, MIX_WIDTH, t), BF16),
        scratch_shapes=[pltpu.VMEM((2, 2, t, t), F32), pltpu.VMEM((2, 2, 1, t), F32),
                        pltpu.VMEM((nt, 2, 1, t), F32), pltpu.VMEM((nt, 2 * V_ROWS, t), F32),
                        pltpu.VMEM((t // 2, t // 2), F32)],
        compiler_params=_params("parallel", "parallel"),
        name="mla_attn",
    )(q_t, k, v_t)


def _swa_proj_kernel(x_ref, g_ref, wq_ref, wkc_ref, wv_ref, wupf_ref, wdnf_ref,
                     q_ref, k_ref, v_ref, qc_ref, wup_ref, wdn_ref):
    wup_ref[...] = wupf_ref[...].astype(BF16)
    wdn_ref[...] = wdnf_ref[...].astype(BF16)
    half = SUB_TILE
    nk = SWA_KV_HEADS * HEAD_DIM
    ones_rows = (lax.broadcasted_iota(jnp.int32, (BF16_ROWS, WINDOW), 0) == 0).astype(BF16)
    norms = [_rms(x_ref[p * half:(p + 1) * half, :], g_ref[...]).astype(BF16)
             for p in range(x_ref.shape[0] // half)]
    for part, hn in enumerate(norms):
        rows = slice(part * half, (part + 1) * half)
        q_ref[part] = (_dot_nt(wq_ref[...], hn) * (HEAD_DIM ** -0.5 * LOG2E)).astype(BF16)
        kc = _dot(hn, wkc_ref[...])
        k_ref[rows, :] = kc[:, :nk].astype(BF16)
        qc_ref[rows, :] = kc[:, nk:].astype(BF16)
        v_t = _dot_nt(wv_ref[...], hn).astype(BF16)
        for w in range(half // WINDOW):
            lanes = slice(w * WINDOW, (w + 1) * WINDOW)
            win = part * (half // WINDOW) + w
            for h in range(SWA_KV_HEADS):
                v_ref[win, h * V_ROWS:h * V_ROWS + HEAD_DIM, :] = v_t[
                    h * HEAD_DIM:(h + 1) * HEAD_DIM, lanes]
                v_ref[win, h * V_ROWS + HEAD_DIM:(h + 1) * V_ROWS, :] = ones_rows


def _swa_proj(x2, layer, g, w_q_t, w_kc, w_v_t, w_up, w_dn, mlp_layer):
    t = x2.shape[0]
    tm = ROW_TILE
    nk = SWA_KV_HEADS * HEAD_DIM
    nwin = tm // WINDOW
    row = lambda w: pl.BlockSpec((tm, w), lambda i: (i, 0))
    consts = [g, w_q_t, w_kc, w_v_t]
    cast_in, cast_out, cast_shape = _mlp_weight_cast_specs(w_up, w_dn, mlp_layer, t // tm)
    return pl.pallas_call(
        _swa_proj_kernel,
        grid=(t // tm,),
        in_specs=[row(D_MODEL)] + [_layer_spec(c, layer) for c in consts] + cast_in,
        out_specs=[pl.BlockSpec((tm // SUB_TILE, MIX_WIDTH, SUB_TILE), lambda i: (i, 0, 0)),
                   row(nk),
                   pl.BlockSpec((nwin, SWA_KV_HEADS * V_ROWS, WINDOW), lambda i: (i, 0, 0)),
                   row(CROSS_WIDTH)] + cast_out,
        out_shape=[jax.ShapeDtypeStruct((t // SUB_TILE, MIX_WIDTH, SUB_TILE), BF16),
                   jax.ShapeDtypeStruct((t, nk), BF16),
                   jax.ShapeDtypeStruct((t // WINDOW, SWA_KV_HEADS * V_ROWS, WINDOW), BF16),
                   jax.ShapeDtypeStruct((t, CROSS_WIDTH), BF16)] + cast_shape,
        compiler_params=_params("parallel"),
        name="swa_proj",
    )(x2, *consts, w_up, w_dn)


def _swa_attn_kernel(slope_ref, sink_ref, q_ref, ko_ref, kp_ref, vo_ref, vp_ref,
                     pqo_ref, pko_ref, pkp_ref, o_ref, *, nwin):
    i = pl.program_id(1)
    w_ = WINDOW
    kcat = jnp.concatenate([kp_ref[...], ko_ref[...]], axis=0)
    pos_k = jnp.concatenate([pkp_ref[...], pko_ref[...]], axis=0)
    kj = lax.broadcasted_iota(jnp.int32, (2 * w_, w_), 0)
    qi = lax.broadcasted_iota(jnp.int32, (2 * w_, w_), 1)
    rel = w_ + qi - kj
    band = (rel >= 0) & (rel < w_)
    first_band = band & ((kj >= w_) | (i > 0))
    zeros = jnp.zeros((HEAD_DIM, w_), BF16)

    def window_keys(w):
        return slice(w * w_, (w + 2) * w_)

    def tile_lanes(w):
        tile, off = divmod(w * w_, SUB_TILE)
        return tile, slice(off, off + w_)

    def scores(w, kh):
        k_pair = kcat[window_keys(w), (kh // 2) * LANES:(kh // 2 + 1) * LANES]
        qs = []
        for g in range(SWA_GROUP):
            hq = kh * SWA_GROUP + g
            tile, lanes = tile_lanes(w)
            qh = q_ref[tile, hq * HEAD_DIM:(hq + 1) * HEAD_DIM, lanes]
            qs.append(jnp.concatenate([qh, zeros] if kh % 2 == 0 else [zeros, qh], axis=0))
        return _dot(k_pair, jnp.concatenate(qs, axis=1))

    dists = {}

    def masked_dist(w):
        if w not in dists:
            dists[w] = jnp.where(first_band if w == 0 else band,
                                 (pqo_ref[w] - pos_k[window_keys(w)]).astype(F32), MASK_DIST)
        return dists[w]

    def finish(w, kh, s3):
        dist = masked_dist(w)
        v_prev = vp_ref[0] if w == 0 else vo_ref[w - 1]
        rows = slice(kh * V_ROWS, (kh + 1) * V_ROWS)
        v_win = jnp.concatenate([v_prev[rows], vo_ref[w, rows, :]], axis=1)
        ps = []
        sink_terms = []
        for g in range(SWA_GROUP):
            hq = kh * SWA_GROUP + g
            sink = sink_ref[hq]
            s = s3[:, g * w_:(g + 1) * w_] - slope_ref[hq] * dist
            m = jnp.maximum(jnp.max(s, axis=0, keepdims=True), sink)
            ps.append(jnp.exp2(s - m).astype(BF16))
            sink_terms.append(jnp.exp2(sink - m))
        o3 = _dot(v_win, jnp.concatenate(ps, axis=1))
        tile, lanes = tile_lanes(w)
        for g in range(SWA_GROUP):
            hq = kh * SWA_GROUP + g
            og = o3[:, g * w_:(g + 1) * w_]
            o_ref[tile, hq * HEAD_DIM:(hq + 1) * HEAD_DIM, lanes] = (
                og[:HEAD_DIM] / (og[HEAD_DIM:HEAD_DIM + 1] + sink_terms[g])).astype(BF16)

    units = [(w, kh) for w in range(nwin) for kh in range(SWA_KV_HEADS)]
    pending = {}
    for idx in range(len(units) + SWA_LOOKAHEAD):
        if idx < len(units):
            pending[idx] = scores(*units[idx])
        if idx >= SWA_LOOKAHEAD:
            done = idx - SWA_LOOKAHEAD
            finish(*units[done], pending.pop(done))


def _swa_attn(slopes, sinks, q_t, k, v_t, pos_col, pos_row, batch, seq):
    rows = SWA_ROWS
    nwin = rows // WINDOW
    nsteps = seq // rows
    nblk = seq // WINDOW
    nk = SWA_KV_HEADS * HEAD_DIM
    vr = SWA_KV_HEADS * V_ROWS
    own = lambda b, i: (b * nsteps + i, 0)
    own3 = lambda b, i: (b * nsteps + i, 0, 0)
    prev = lambda b, i: (b * nblk + jnp.maximum(i * nwin - 1, 0), 0)
    prev3 = lambda b, i: (b * nblk + jnp.maximum(i * nwin - 1, 0), 0, 0)
    smem = pl.BlockSpec(memory_space=pltpu.SMEM)
    return pl.pallas_call(
        functools.partial(_swa_attn_kernel, nwin=nwin),
        grid=(batch, nsteps),
        in_specs=[smem, smem,
                  pl.BlockSpec((rows // SUB_TILE, MIX_WIDTH, SUB_TILE), own3),
                  pl.BlockSpec((rows, nk), own),
                  pl.BlockSpec((WINDOW, nk), prev),
                  pl.BlockSpec((nwin, vr, WINDOW), own3),
                  pl.BlockSpec((1, vr, WINDOW), prev3),
                  pl.BlockSpec((nwin, 1, WINDOW), own3),
                  pl.BlockSpec((rows, 1), own),
                  pl.BlockSpec((WINDOW, 1), prev)],
        out_specs=pl.BlockSpec((rows // SUB_TILE, MIX_WIDTH, SUB_TILE), own3),
        out_shape=jax.ShapeDtypeStruct((batch * seq // SUB_TILE, MIX_WIDTH, SUB_TILE), BF16),
        compiler_params=_params("parallel", "parallel"),
        name="swa_attn",
    )(slopes, sinks, q_t, k, k, v_t, v_t, pos_row, pos_col, pos_col)


def _out_mlp_kernel(*refs, final):
    if final:
        (x_ref, mix_ref, qc_ref, km_ref, vm_ref, wo_ref, g_ref, wup_ref, wdn_ref,
         gf_ref, o_ref, acc_ref) = refs
    else:
        (x_ref, mix_ref, qc_ref, km_ref, vm_ref, wo_ref, g_ref, wup_ref, wdn_ref,
         o_ref, acc_ref) = refs
    lo = _low_half()
    n_pairs = CROSS_WIDTH // LANES
    for part in range(x_ref.shape[0] // SUB_TILE):
        rows = slice(part * SUB_TILE, (part + 1) * SUB_TILE)
        scores = []
        for pair in range(n_pairs):
            sl = slice(pair * LANES, (pair + 1) * LANES)
            qp = qc_ref[rows, sl]
            for half in range(2):
                sel = lo if half == 0 else jnp.logical_not(lo)
                scores.append(_dot_nt(jnp.where(sel, qp, jnp.zeros_like(qp)), km_ref[:, sl]))
        cross = []
        for pair in range(n_pairs):
            vp = vm_ref[:, pair * LANES:(pair + 1) * LANES]
            outs = []
            for half in range(2):
                s = scores[2 * pair + half]
                e = jnp.exp(s - jnp.max(s, axis=-1, keepdims=True))
                den = jnp.sum(e, axis=-1, keepdims=True)
                outs.append(_dot(e.astype(BF16), vp) / den)
            cross.append(jnp.where(lo, outs[0], outs[1]).astype(BF16))
        mix = mix_ref[part].astype(F32).T.astype(BF16)
        attn = jnp.concatenate([mix] + cross, axis=1)
        x1 = x_ref[rows, :] + _dot(attn, wo_ref[...])
        hn = _rms(x1, g_ref[...]).astype(BF16)
        acc_ref[...] = x1
        for c in range(D_FF // FF_CHUNK):
            cols = slice(c * FF_CHUNK, (c + 1) * FF_CHUNK)
            h = jnp.maximum(_dot(hn, wup_ref[:, cols]), 0.0)
            acc_ref[...] += _dot((h * h).astype(BF16), wdn_ref[cols, :])
        if final:
            o_ref[rows, :] = _rms(acc_ref[...], gf_ref[...])
        else:
            o_ref[rows, :] = acc_ref[...]


def _out_mlp(x2, mix, qc, memkv, layer, w_o, g, w_up, w_dn, g_final, seq):
    t = x2.shape[0]
    tm = ROW_TILE
    per_b = seq // tm
    final = g_final is not None
    row = lambda w: pl.BlockSpec((tm, w), lambda i: (i, 0))
    in_specs = [row(D_MODEL),
                pl.BlockSpec((tm // SUB_TILE, MIX_WIDTH, SUB_TILE), lambda i: (i, 0, 0)),
                row(CROSS_WIDTH),
                pl.BlockSpec((N_MEM, CROSS_WIDTH), lambda i: (i // per_b, 2 * layer)),
                pl.BlockSpec((N_MEM, CROSS_WIDTH), lambda i: (i // per_b, 2 * layer + 1)),
                _layer_spec(w_o, layer), _layer_spec(g, layer),
                _const_spec(w_up.shape), _const_spec(w_dn.shape)]
    args = [x2, mix, qc, memkv, memkv, w_o, g, w_up, w_dn]
    if final:
        in_specs.append(_const_spec(g_final.shape))
        args.append(g_final)
    return pl.pallas_call(
        functools.partial(_out_mlp_kernel, final=final),
        grid=(t // tm,),
        in_specs=in_specs,
        out_specs=row(D_MODEL),
        out_shape=jax.ShapeDtypeStruct((t, D_MODEL), F32),
        scratch_shapes=[pltpu.VMEM((SUB_TILE, D_MODEL), F32)],
        compiler_params=_params("parallel"),
        name="out_mlp",
    )(*args)


def _prep_mla(w_in, w_uq, w_ukv):
    n = w_in.shape[0]
    n1 = MLA_Q_RANK + MLA_KV_RANK
    w_in = w_in.astype(BF16)
    w_a = jnp.concatenate([w_in[:, :, :n1], w_in[:, :, n1 + MLA_ROPE:]], axis=2)
    w_kr_t = w_in[:, :, n1:n1 + MLA_ROPE].transpose(0, 2, 1)
    w_uq_t = w_uq.astype(BF16).transpose(0, 2, 1)
    kv = w_ukv.astype(BF16).reshape(n, MLA_KV_RANK, MLA_HEADS, 2 * HEAD_DIM)
    w_uk = kv[..., :MLA_NOPE].reshape(n, MLA_KV_RANK, MLA_HEADS * MLA_NOPE)
    w_uv_t = kv[..., MLA_NOPE:].reshape(n, MLA_KV_RANK, MIX_WIDTH).transpose(0, 2, 1)
    return w_a, w_kr_t, w_uq_t, w_uk, w_uv_t


def _prep_swa(w_in):
    nq = SWA_Q_HEADS * HEAD_DIM
    nk = SWA_KV_HEADS * HEAD_DIM
    w_in = w_in.astype(BF16)
    w_q_t = w_in[:, :, :nq].transpose(0, 2, 1)
    w_kc = jnp.concatenate([w_in[:, :, nq:nq + nk], w_in[:, :, nq + 2 * nk:]], axis=2)
    w_v_t = w_in[:, :, nq + nk:nq + 2 * nk].transpose(0, 2, 1)
    return w_q_t, w_kc, w_v_t


def kernel(x, mem, positions, attn_norm_g, mlp_norm_g, mem_norm_g, final_norm_g,
           mla_w_in, mla_q_norm_g, mla_kv_norm_g, mla_w_uq, mla_w_ukv,
           swa_w_in, swa_sinks, w_mem_kv, w_o, mlp_w_up, mlp_w_down):
    batch, seq, d = x.shape
    depth = attn_norm_g.shape[0]
    t = batch * seq
    x2 = x.reshape(t, d)

    w_mem = w_mem_kv.astype(BF16)
    w_mem = jnp.concatenate([w_mem[:, :, :CROSS_WIDTH] * (HEAD_DIM ** -0.5),
                             w_mem[:, :, CROSS_WIDTH:]], axis=2)
    memkv = _memkv(mem.reshape(batch * N_MEM, d), mem_norm_g.reshape(1, d), w_mem)

    inv = ROPE_THETA ** (-(jnp.arange(ROPE_HALF, dtype=F32) * 2.0) / MLA_ROPE)
    cos_t, sin_t = _rope_tables(positions.astype(F32).reshape(1, t), inv.reshape(ROPE_HALF, 1))

    pos_col_i = positions.reshape(t, 1)
    pos_row_i = positions.reshape(t // WINDOW, 1, WINDOW)
    slopes = 2.0 ** (-8.0 * (jnp.arange(SWA_Q_HEADS, dtype=F32) + 1.0) / SWA_Q_HEADS) * LOG2E

    g_attn = attn_norm_g.reshape(depth, 1, d)
    g_mlp = mlp_norm_g.reshape(depth, 1, d)
    mla_w = _prep_mla(mla_w_in, mla_w_uq, mla_w_ukv)
    mla_qg = mla_q_norm_g.reshape(-1, 1, MLA_Q_RANK)
    mla_kvg = mla_kv_norm_g.reshape(-1, 1, MLA_KV_RANK)
    swa_w = _prep_swa(swa_w_in)
    w_o_b = w_o.astype(BF16)
    w_up_f = mlp_w_up.astype(F32)
    w_dn_f = mlp_w_down.astype(F32)
    sinks = swa_sinks.astype(F32) * LOG2E
    g_attn_mla, g_attn_swa = g_attn[0::2], g_attn[1::2]

    for i in range(depth):
        j = i // 2
        is_mla = i % 2 == 0
        if is_mla:
            w_a, w_kr_t, w_uq_t, w_uk, w_uv_t = mla_w
            q_t, k, v_t, qc, w_up, w_dn = _mla_proj(
                x2, j, g_attn_mla, w_a, w_kr_t, mla_qg, mla_kvg, w_uq_t, w_uk, w_uv_t,
                cos_t, sin_t, w_up_f, w_dn_f, i)
            mix = _mla_attn(q_t, k, v_t, batch, seq)
        else:
            q_t, k, v_t, qc, w_up, w_dn = _swa_proj(x2, j, g_attn_swa, *swa_w, w_up_f, w_dn_f, i)
            mix = _swa_attn(slopes, sinks[j], q_t, k, v_t, pos_col_i, pos_row_i, batch, seq)
        g_final = final_norm_g.reshape(1, d) if i == depth - 1 else None
        x2 = _out_mlp(x2, mix, qc, memkv, i, w_o_b, g_mlp, w_up, w_dn, g_final, seq)
    return x2.reshape(batch, seq, d)
```

```python
import functools
import math

import jax
import jax.numpy as jnp
from jax import lax
from jax.experimental import pallas as pl
from jax.experimental.pallas import tpu as pltpu

F32 = jnp.float32
BF16 = jnp.bfloat16

D_MODEL = 1024
HEAD_DIM = 64
N_MEM = 256
CROSS_WIDTH = 256
MIX_WIDTH = 768
EPS = 1e-6
NEG = -1e30
LOG2E = math.log2(math.e)
MASK_DIST = 1e32
MLA_HEADS = 12
MLA_Q_RANK = 384
MLA_KV_RANK = 256
MLA_NOPE = 64
MLA_ROPE = 32
MLA_QK = MLA_NOPE + MLA_ROPE
ROPE_THETA = 10000.0
SWA_Q_HEADS = 12
SWA_KV_HEADS = 4
SWA_GROUP = 3
WINDOW = 128
D_FF = 4 * D_MODEL

LANES = 128
QK_PAD = LANES
ROPE_HALF = MLA_ROPE // 2
BF16_ROWS = 16
V_ROWS = HEAD_DIM + BF16_ROWS

SUB_TILE = 512
ROW_TILE = 2 * SUB_TILE
ATTN_TILE = SUB_TILE
FULL_BLOCKS_PER_TRIP = 14
DIAG_BLOCKS_PER_TRIP = 8
SWA_ROWS = ROW_TILE
SWA_LOOKAHEAD = 2
FF_CHUNK = 1024
VMEM_LIMIT = 48 * 1024 * 1024


def _rms(x, g):
    ms = jnp.mean(x * x, axis=-1, keepdims=True)
    return x * lax.rsqrt(ms + EPS) * g


def _dot(a, b):
    return jnp.dot(a, b, preferred_element_type=F32)


def _dot_nt(a, b):
    return lax.dot_general(a, b, (((1,), (1,)), ((), ())), preferred_element_type=F32)


def _low_half():
    return lax.broadcasted_iota(jnp.int32, (1, LANES), 1) < HEAD_DIM


def _const_spec(shape):
    nd = len(shape)
    return pl.BlockSpec(shape, lambda *_: (0,) * nd, pipeline_mode=pl.Buffered(1))


def _layer_spec(stacked, layer):
    nd = stacked.ndim
    return pl.BlockSpec((None,) + stacked.shape[1:], lambda *_: (layer,) + (0,) * (nd - 1),
                        pipeline_mode=pl.Buffered(1))


def _mlp_weight_cast_specs(w_up, w_dn, layer, n_steps):
    d, ff = w_up.shape[1:]
    up_rows, dn_rows = d // n_steps, ff // n_steps
    in_specs = [pl.BlockSpec((None, up_rows, ff), lambda i: (layer, i, 0)),
                pl.BlockSpec((None, dn_rows, d), lambda i: (layer, i, 0)),
                pl.BlockSpec((None, up_rows, d), lambda i: (layer, i, 0))]
    out_specs = [pl.BlockSpec((up_rows, ff), lambda i: (i, 0)),
                 pl.BlockSpec((dn_rows, d), lambda i: (i, 0)),
                 pl.BlockSpec((up_rows, d), lambda i: (i, 0))]
    out_shape = [jax.ShapeDtypeStruct((d, ff), BF16), jax.ShapeDtypeStruct((ff, d), BF16),
                 jax.ShapeDtypeStruct((d, d), BF16)]
    return in_specs, out_specs, out_shape


def _params(*sem):
    return pltpu.CompilerParams(dimension_semantics=sem, vmem_limit_bytes=VMEM_LIMIT)


def _rope_table_kernel(pos_ref, inv_ref, cos_ref, sin_ref):
    ang = inv_ref[...] * pos_ref[...]
    cos_ref[...] = jnp.cos(ang)
    sin_ref[...] = jnp.sin(ang)


def _rope_tables(pos_row, inv_col):
    t = pos_row.shape[1]
    tn = 2048
    return pl.pallas_call(
        _rope_table_kernel,
        grid=(t // tn,),
        in_specs=[pl.BlockSpec((1, tn), lambda i: (0, i)),
                  pl.BlockSpec((ROPE_HALF, 1), lambda i: (0, 0))],
        out_specs=[pl.BlockSpec((ROPE_HALF, tn), lambda i: (0, i))] * 2,
        out_shape=[jax.ShapeDtypeStruct((ROPE_HALF, t), F32)] * 2,
        compiler_params=_params("parallel"),
        name="rope_tables",
    )(pos_row, inv_col)


def _memkv_kernel(mem_ref, g_ref, w_ref, o_ref):
    mn = _rms(mem_ref[...], g_ref[...]).astype(BF16)
    ncol = w_ref.shape[2]
    for layer in range(w_ref.shape[0]):
        o_ref[:, layer * ncol:(layer + 1) * ncol] = _dot(mn, w_ref[layer]).astype(BF16)


def _memkv(mem2, g, w_all):
    rows = mem2.shape[0]
    depth, _, ncol = w_all.shape
    return pl.pallas_call(
        _memkv_kernel,
        grid=(rows // N_MEM,),
        in_specs=[pl.BlockSpec((N_MEM, D_MODEL), lambda b: (b, 0)),
                  _const_spec((1, D_MODEL)), _const_spec(w_all.shape)],
        out_specs=pl.BlockSpec((N_MEM, depth * ncol), lambda b: (b, 0)),
        out_shape=jax.ShapeDtypeStruct((rows, depth * ncol), BF16),
        compiler_params=_params("parallel"),
        name="mem_kv",
    )(mem2, g, w_all)


def _mla_proj_kernel(x_ref, g_ref, wa_ref, wkr_ref, qg_ref, kvg_ref, wuq_ref, wuk_ref,
                     wuv_ref, cos_ref, sin_ref, wupf_ref, wdnf_ref, wof_ref,
                     q_ref, k_ref, v_ref, qc_ref, wup_ref, wdn_ref, wo_ref):
    wup_ref[...] = wupf_ref[...].astype(BF16)
    wdn_ref[...] = wdnf_ref[...].astype(BF16)
    wo_ref[...] = wof_ref[...].astype(BF16)
    half = SUB_TILE
    scale = MLA_QK ** -0.5 * LOG2E
    lo = _low_half()

    def compress(rows):
        hn = _rms(x_ref[rows, :], g_ref[...]).astype(BF16)
        return hn, _dot(hn, wa_ref[...])

    def expand(part, hn, proj):
        rows = slice(part * half, (part + 1) * half)
        c_q = proj[:, :MLA_Q_RANK]
        c_kv = proj[:, MLA_Q_RANK:MLA_Q_RANK + MLA_KV_RANK]
        qc_ref[rows, :] = proj[:, MLA_Q_RANK + MLA_KV_RANK:].astype(BF16)
        cqn = _rms(c_q, qg_ref[...]).astype(BF16)
        ckvn = _rms(c_kv, kvg_ref[...]).astype(BF16)
        cos = cos_ref[:, rows]
        sin = sin_ref[:, rows]

        def rope(x1, x2):
            return x1 * cos - x2 * sin, x1 * sin + x2 * cos

        kr_t = _dot_nt(wkr_ref[...], hn)
        r1, r2 = rope(kr_t[:ROPE_HALF], kr_t[ROPE_HALF:])
        k_rope = jnp.concatenate(
            [jnp.zeros((MLA_NOPE, half), F32), r1, r2,
             jnp.zeros((QK_PAD - MLA_QK, half), F32)], axis=0).T

        k_nope = _dot(ckvn, wuk_ref[...])
        for pair in range(MLA_HEADS // 2):
            both = k_nope[:, pair * LANES:(pair + 1) * LANES]
            for parity, tile in enumerate((both, pltpu.roll(both, HEAD_DIM, 1))):
                h = 2 * pair + parity
                k_ref[rows, h * QK_PAD:(h + 1) * QK_PAD] = jnp.where(lo, tile, k_rope).astype(BF16)

        v_t = _dot_nt(wuv_ref[...], ckvn).astype(BF16)
        ones_rows = (lax.broadcasted_iota(jnp.int32, (BF16_ROWS, half), 0) == 0).astype(BF16)
        for h in range(MLA_HEADS):
            v_ref[part, h * V_ROWS:h * V_ROWS + HEAD_DIM, :] = v_t[h * HEAD_DIM:(h + 1) * HEAD_DIM]
            v_ref[part, h * V_ROWS + HEAD_DIM:(h + 1) * V_ROWS, :] = ones_rows

        q_t = _dot_nt(wuq_ref[...], cqn)
        pad = jnp.zeros((QK_PAD - MLA_QK, half), BF16)
        for h in range(MLA_HEADS):
            src = h * MLA_QK
            dst = h * QK_PAD
            r1, r2 = rope(q_t[src + MLA_NOPE:src + MLA_NOPE + ROPE_HALF],
                          q_t[src + MLA_NOPE + ROPE_HALF:src + MLA_QK])
            q_ref[part, dst:dst + MLA_NOPE, :] = (q_t[src:src + MLA_NOPE] * scale).astype(BF16)
            q_ref[part, dst + MLA_NOPE:dst + MLA_NOPE + ROPE_HALF, :] = (r1 * scale).astype(BF16)
            q_ref[part, dst + MLA_NOPE + ROPE_HALF:dst + MLA_QK, :] = (r2 * scale).astype(BF16)
            q_ref[part, dst + MLA_QK:dst + QK_PAD, :] = pad

    n_sub = x_ref.shape[0] // half
    compressed = [compress(slice(p * half, (p + 1) * half)) for p in range(n_sub)]
    for p in range(n_sub):
        expand(p, *compressed[p])


def _mla_proj(x2, layer, g, w_a, w_kr, qg, kvg, w_uq, w_uk, w_uv, cos_t, sin_t,
              w_up, w_dn, w_o, mlp_layer):
    t = x2.shape[0]
    tm = ROW_TILE
    qw = MLA_HEADS * QK_PAD
    row = lambda w: pl.BlockSpec((tm, w), lambda i: (i, 0))
    col = lambda r: pl.BlockSpec((r, tm), lambda i: (0, i))
    n_sub = tm // SUB_TILE
    tile = lambda r: pl.BlockSpec((n_sub, r, SUB_TILE), lambda i: (i, 0, 0))
    consts = [g, w_a, w_kr, qg, kvg, w_uq, w_uk, w_uv]
    cast_in, cast_out, cast_shape = _mlp_weight_cast_specs(w_up, w_dn, mlp_layer, t // tm)
    return pl.pallas_call(
        _mla_proj_kernel,
        grid=(t // tm,),
        in_specs=[row(D_MODEL)] + [_layer_spec(c, layer) for c in consts]
                 + [col(ROPE_HALF), col(ROPE_HALF)] + cast_in,
        out_specs=[tile(qw), row(qw), tile(MLA_HEADS * V_ROWS), row(CROSS_WIDTH)] + cast_out,
        out_shape=[jax.ShapeDtypeStruct((t // SUB_TILE, qw, SUB_TILE), BF16),
                   jax.ShapeDtypeStruct((t, qw), BF16),
                   jax.ShapeDtypeStruct((t // SUB_TILE, MLA_HEADS * V_ROWS, SUB_TILE), BF16),
                   jax.ShapeDtypeStruct((t, CROSS_WIDTH), BF16)] + cast_shape,
        compiler_params=_params("parallel"),
        name="mla_proj",
    )(x2, *consts, cos_t, sin_t, w_up, w_dn, w_o)


def _mla_attn_kernel(q_ref, k_ref, v_ref, o_ref, s_ref, mb_ref, m_ref, acc_ref, bias_ref,
                     *, nt):
    t = q_ref.shape[2]
    half = t // 2
    acc_ref[...] = jnp.zeros(acc_ref.shape, F32)
    key = lax.broadcasted_iota(jnp.int32, (half, half), 0)
    qry = lax.broadcasted_iota(jnp.int32, (half, half), 1)
    bias_ref[...] = jnp.where(key <= qry, 0.0, NEG)

    def head_rows(h):
        return slice(h * QK_PAD, (h + 1) * QK_PAD)

    def scores(i, j, slot, h):
        start = pl.multiple_of(j * t, t)
        kb = k_ref[pl.ds(start, t), head_rows(h)]
        s = _dot(kb, q_ref[i, head_rows(h), :])
        s_ref[slot, h] = s
        mb_ref[slot, h] = jnp.max(s, axis=0, keepdims=True)

    def consume(i, j, slot, h):
        s = s_ref[slot, h]
        m_prev = jnp.where(j == 0, NEG, m_ref[i, h])
        m_new = jnp.maximum(m_prev, mb_ref[slot, h])
        a = jnp.exp2(m_prev - m_new)
        p = jnp.exp2(s - m_new).astype(BF16)
        rows = slice(h * V_ROWS, (h + 1) * V_ROWS)
        m_ref[i, h] = m_new
        acc_ref[i, rows, :] = a * acc_ref[i, rows, :] + _dot(v_ref[j, rows, :], p)

    def scores_diag(i, slot, h):
        start = pl.multiple_of(i * t, t)
        q = q_ref[i, head_rows(h), :]
        s_top = _dot(k_ref[pl.ds(start, half), head_rows(h)], q)
        s_bot = _dot(k_ref[pl.ds(start + half, half), head_rows(h)], q[:, half:])
        tri = bias_ref[...]
        s_left = s_top[:, :half] + tri
        s_bot = s_bot + tri
        s_ref[slot, h, :half, :half] = s_left
        s_ref[slot, h, :half, half:] = s_top[:, half:]
        s_ref[slot, h, half:, half:] = s_bot
        mb_ref[slot, h, :, :half] = jnp.max(s_left, axis=0, keepdims=True)
        mb_ref[slot, h, :, half:] = jnp.maximum(
            jnp.max(s_top[:, half:], axis=0, keepdims=True), jnp.max(s_bot, axis=0, keepdims=True))

    def consume_diag(i, slot, h):
        m_prev = jnp.where(i == 0, NEG, m_ref[i, h])
        m_new = jnp.maximum(m_prev, mb_ref[slot, h])
        a = jnp.exp2(m_prev - m_new)
        p_top = jnp.exp2(s_ref[slot, h, :half, :] - m_new).astype(BF16)
        p_bot = jnp.exp2(s_ref[slot, h, half:, half:] - m_new[:, half:]).astype(BF16)
        rows = slice(h * V_ROWS, (h + 1) * V_ROWS)
        acc = a * acc_ref[i, rows, :] + _dot(v_ref[i, rows, :half], p_top)
        acc_r = acc[:, half:] + _dot(v_ref[i, rows, half:], p_bot)
        out = slice(h * HEAD_DIM, (h + 1) * HEAD_DIM)
        o_ref[i, out, :half] = (acc[:HEAD_DIM, :half]
                                / acc[HEAD_DIM:HEAD_DIM + 1, :half]).astype(BF16)
        o_ref[i, out, half:] = (acc_r[:HEAD_DIM] / acc_r[HEAD_DIM:HEAD_DIM + 1]).astype(BF16)

    def next_full(i, j):
        wrap = j + 1 == i
        i_n = jnp.minimum(jnp.where(wrap, i + 1, i), nt - 1)
        return i_n, jnp.where(wrap, 0, j + 1)

    def full_trip(_, cur):
        for b in range(FULL_BLOCKS_PER_TRIP):
            nxt = next_full(*cur)
            for h in range(2):
                scores(*nxt, 1 - b % 2, h)
                consume(*cur, b % 2, h)
            cur = nxt
        return cur

    def diag_trip(_, i):
        for b in range(DIAG_BLOCKS_PER_TRIP):
            i_n = jnp.minimum(i + 1, nt - 1)
            for h in range(2):
                scores_diag(i_n, 1 - b % 2, h)
                consume_diag(i, b % 2, h)
            i = i + 1
        return i

    for h in range(2):
        scores(1, 0, 0, h)
    lax.fori_loop(0, nt * (nt - 1) // 2 // FULL_BLOCKS_PER_TRIP, full_trip,
                  (jnp.int32(1), jnp.int32(0)))
    for h in range(2):
        scores_diag(0, 0, h)
    lax.fori_loop(0, nt // DIAG_BLOCKS_PER_TRIP, diag_trip, jnp.int32(0))


def _mla_attn(q_t, k, v_t, batch, seq):
    t = ATTN_TILE
    nt = seq // t
    assert FULL_BLOCKS_PER_TRIP % 2 == 0 and DIAG_BLOCKS_PER_TRIP % 2 == 0
    assert (nt * (nt - 1) // 2) % FULL_BLOCKS_PER_TRIP == 0 and nt % DIAG_BLOCKS_PER_TRIP == 0
    pairs = MLA_HEADS // 2
    return pl.pallas_call(
        functools.partial(_mla_attn_kernel, nt=nt),
        grid=(batch, pairs),
        in_specs=[pl.BlockSpec((nt, 2 * QK_PAD, t), lambda b, p: (b, p, 0)),
                  pl.BlockSpec((seq, 2 * QK_PAD), lambda b, p: (b, p)),
                  pl.BlockSpec((nt, 2 * V_ROWS, t), lambda b, p: (b, p, 0))],
        out_specs=pl.BlockSpec((nt, LANES, t), lambda b, p: (b, p, 0)),
        out_shape=jax.ShapeDtypeStruct((batch * nt, MIX_WIDTH, t), BF16),
        scratch_shapes=[pltpu.VMEM((2, 2, t, t), F32), pltpu.VMEM((2, 2, 1, t), F32),
                        pltpu.VMEM((nt, 2, 1, t), F32), pltpu.VMEM((nt, 2 * V_ROWS, t), F32),
                        pltpu.VMEM((t // 2, t // 2), F32)],
        compiler_params=_params("parallel", "parallel"),
        name="mla_attn",
    )(q_t, k, v_t)


def _swa_proj_kernel(x_ref, g_ref, wq_ref, wkc_ref, wv_ref, wupf_ref, wdnf_ref, wof_ref,
                     q_ref, k_ref, v_ref, qc_ref, wup_ref, wdn_ref, wo_ref):
    wup_ref[...] = wupf_ref[...].astype(BF16)
    wdn_ref[...] = wdnf_ref[...].astype(BF16)
    wo_ref[...] = wof_ref[...].astype(BF16)
    half = SUB_TILE
    nk = SWA_KV_HEADS * HEAD_DIM
    ones_rows = (lax.broadcasted_iota(jnp.int32, (BF16_ROWS, WINDOW), 0) == 0).astype(BF16)
    norms = [_rms(x_ref[p * half:(p + 1) * half, :], g_ref[...]).astype(BF16)
             for p in range(x_ref.shape[0] // half)]
    for part, hn in enumerate(norms):
        rows = slice(part * half, (part + 1) * half)
        q_ref[part] = (_dot_nt(wq_ref[...], hn) * (HEAD_DIM ** -0.5 * LOG2E)).astype(BF16)
        kc = _dot(hn, wkc_ref[...])
        k_ref[rows, :] = kc[:, :nk].astype(BF16)
        qc_ref[rows, :] = kc[:, nk:].astype(BF16)
        v_t = _dot_nt(wv_ref[...], hn).astype(BF16)
        for w in range(half // WINDOW):
            lanes = slice(w * WINDOW, (w + 1) * WINDOW)
            win = part * (half // WINDOW) + w
            for h in range(SWA_KV_HEADS):
                v_ref[win, h * V_ROWS:h * V_ROWS + HEAD_DIM, :] = v_t[
                    h * HEAD_DIM:(h + 1) * HEAD_DIM, lanes]
                v_ref[win, h * V_ROWS + HEAD_DIM:(h + 1) * V_ROWS, :] = ones_rows


def _swa_proj(x2, layer, g, w_q_t, w_kc, w_v_t, w_up, w_dn, w_o, mlp_layer):
    t = x2.shape[0]
    tm = ROW_TILE
    nk = SWA_KV_HEADS * HEAD_DIM
    nwin = tm // WINDOW
    row = lambda w: pl.BlockSpec((tm, w), lambda i: (i, 0))
    consts = [g, w_q_t, w_kc, w_v_t]
    cast_in, cast_out, cast_shape = _mlp_weight_cast_specs(w_up, w_dn, mlp_layer, t // tm)
    return pl.pallas_call(
        _swa_proj_kernel,
        grid=(t // tm,),
        in_specs=[row(D_MODEL)] + [_layer_spec(c, layer) for c in consts] + cast_in,
        out_specs=[pl.BlockSpec((tm // SUB_TILE, MIX_WIDTH, SUB_TILE), lambda i: (i, 0, 0)),
                   row(nk),
                   pl.BlockSpec((nwin, SWA_KV_HEADS * V_ROWS, WINDOW), lambda i: (i, 0, 0)),
                   row(CROSS_WIDTH)] + cast_out,
        out_shape=[jax.ShapeDtypeStruct((t // SUB_TILE, MIX_WIDTH, SUB_TILE), BF16),
                   jax.ShapeDtypeStruct((t, nk), BF16),
                   jax.ShapeDtypeStruct((t // WINDOW, SWA_KV_HEADS * V_ROWS, WINDOW), BF16),
                   jax.ShapeDtypeStruct((t, CROSS_WIDTH), BF16)] + cast_shape,
        compiler_params=_params("parallel"),
        name="swa_proj",
    )(x2, *consts, w_up, w_dn, w_o)


def _swa_attn_kernel(slope_ref, sink_ref, q_ref, ko_ref, kp_ref, vo_ref, vp_ref,
                     pqo_ref, pko_ref, pkp_ref, o_ref, *, nwin):
    i = pl.program_id(1)
    w_ = WINDOW
    kcat = jnp.concatenate([kp_ref[...], ko_ref[...]], axis=0)
    pos_k = jnp.concatenate([pkp_ref[...], pko_ref[...]], axis=0)
    kj = lax.broadcasted_iota(jnp.int32, (2 * w_, w_), 0)
    qi = lax.broadcasted_iota(jnp.int32, (2 * w_, w_), 1)
    rel = w_ + qi - kj
    band = (rel >= 0) & (rel < w_)
    first_band = band & ((kj >= w_) | (i > 0))
    zeros = jnp.zeros((HEAD_DIM, w_), BF16)

    def window_keys(w):
        return slice(w * w_, (w + 2) * w_)

    def tile_lanes(w):
        tile, off = divmod(w * w_, SUB_TILE)
        return tile, slice(off, off + w_)

    def scores(w, kh):
        k_pair = kcat[window_keys(w), (kh // 2) * LANES:(kh // 2 + 1) * LANES]
        qs = []
        for g in range(SWA_GROUP):
            hq = kh * SWA_GROUP + g
            tile, lanes = tile_lanes(w)
            qh = q_ref[tile, hq * HEAD_DIM:(hq + 1) * HEAD_DIM, lanes]
            qs.append(jnp.concatenate([qh, zeros] if kh % 2 == 0 else [zeros, qh], axis=0))
        return _dot(k_pair, jnp.concatenate(qs, axis=1))

    dists = {}

    def masked_dist(w):
        if w not in dists:
            dists[w] = jnp.where(first_band if w == 0 else band,
                                 (pqo_ref[w] - pos_k[window_keys(w)]).astype(F32), MASK_DIST)
        return dists[w]

    def finish(w, kh, s3):
        dist = masked_dist(w)
        v_prev = vp_ref[0] if w == 0 else vo_ref[w - 1]
        rows = slice(kh * V_ROWS, (kh + 1) * V_ROWS)
        v_win = jnp.concatenate([v_prev[rows], vo_ref[w, rows, :]], axis=1)
        ps = []
        sink_terms = []
        for g in range(SWA_GROUP):
            hq = kh * SWA_GROUP + g
            sink = sink_ref[hq]
            s = s3[:, g * w_:(g + 1) * w_] - slope_ref[hq] * dist
            m = jnp.maximum(jnp.max(s, axis=0, keepdims=True), sink)
            ps.append(jnp.exp2(s - m).astype(BF16))
            sink_terms.append(jnp.exp2(sink - m))
        o3 = _dot(v_win, jnp.concatenate(ps, axis=1))
        tile, lanes = tile_lanes(w)
        for g in range(SWA_GROUP):
            hq = kh * SWA_GROUP + g
            og = o3[:, g * w_:(g + 1) * w_]
            o_ref[tile, hq * HEAD_DIM:(hq + 1) * HEAD_DIM, lanes] = (
                og[:HEAD_DIM] / (og[HEAD_DIM:HEAD_DIM + 1] + sink_terms[g])).astype(BF16)

    units = [(w, kh) for w in range(nwin) for kh in range(SWA_KV_HEADS)]
    pending = {}
    for idx in range(len(units) + SWA_LOOKAHEAD):
        if idx < len(units):
            pending[idx] = scores(*units[idx])
        if idx >= SWA_LOOKAHEAD:
            done = idx - SWA_LOOKAHEAD
            finish(*units[done], pending.pop(done))


def _swa_attn(slopes, sinks, q_t, k, v_t, pos_col, pos_row, batch, seq):
    rows = SWA_ROWS
    nwin = rows // WINDOW
    nsteps = seq // rows
    nblk = seq // WINDOW
    nk = SWA_KV_HEADS * HEAD_DIM
    vr = SWA_KV_HEADS * V_ROWS
    own = lambda b, i: (b * nsteps + i, 0)
    own3 = lambda b, i: (b * nsteps + i, 0, 0)
    prev = lambda b, i: (b * nblk + jnp.maximum(i * nwin - 1, 0), 0)
    prev3 = lambda b, i: (b * nblk + jnp.maximum(i * nwin - 1, 0), 0, 0)
    smem = pl.BlockSpec(memory_space=pltpu.SMEM)
    return pl.pallas_call(
        functools.partial(_swa_attn_kernel, nwin=nwin),
        grid=(batch, nsteps),
        in_specs=[smem, smem,
                  pl.BlockSpec((rows // SUB_TILE, MIX_WIDTH, SUB_TILE), own3),
                  pl.BlockSpec((rows, nk), own),
                  pl.BlockSpec((WINDOW, nk), prev),
                  pl.BlockSpec((nwin, vr, WINDOW), own3),
                  pl.BlockSpec((1, vr, WINDOW), prev3),
                  pl.BlockSpec((nwin, 1, WINDOW), own3),
                  pl.BlockSpec((rows, 1), own),
                  pl.BlockSpec((WINDOW, 1), prev)],
        out_specs=pl.BlockSpec((rows // SUB_TILE, MIX_WIDTH, SUB_TILE), own3),
        out_shape=jax.ShapeDtypeStruct((batch * seq // SUB_TILE, MIX_WIDTH, SUB_TILE), BF16),
        compiler_params=_params("parallel", "parallel"),
        name="swa_attn",
    )(slopes, sinks, q_t, k, k, v_t, v_t, pos_row, pos_col, pos_col)


def _out_mlp_kernel(*refs, final):
    if final:
        (x_ref, mix_ref, qc_ref, km_ref, vm_ref, wo_ref, g_ref, wup_ref, wdn_ref,
         gf_ref, o_ref, acc_ref) = refs
    else:
        (x_ref, mix_ref, qc_ref, km_ref, vm_ref, wo_ref, g_ref, wup_ref, wdn_ref,
         o_ref, acc_ref) = refs
    lo = _low_half()
    n_pairs = CROSS_WIDTH // LANES
    for part in range(x_ref.shape[0] // SUB_TILE):
        rows = slice(part * SUB_TILE, (part + 1) * SUB_TILE)
        scores = []
        for pair in range(n_pairs):
            sl = slice(pair * LANES, (pair + 1) * LANES)
            qp = qc_ref[rows, sl]
            for half in range(2):
                sel = lo if half == 0 else jnp.logical_not(lo)
                scores.append(_dot_nt(jnp.where(sel, qp, jnp.zeros_like(qp)), km_ref[:, sl]))
        cross = []
        for pair in range(n_pairs):
            vp = vm_ref[:, pair * LANES:(pair + 1) * LANES]
            outs = []
            for half in range(2):
                s = scores[2 * pair + half]
                e = jnp.exp(s - jnp.max(s, axis=-1, keepdims=True))
                den = jnp.sum(e, axis=-1, keepdims=True)
                outs.append(_dot(e.astype(BF16), vp) / den)
            cross.append(jnp.where(lo, outs[0], outs[1]).astype(BF16))
        mix = mix_ref[part].astype(F32).T.astype(BF16)
        attn = jnp.concatenate([mix] + cross, axis=1)
        x1 = x_ref[rows, :] + _dot(attn, wo_ref[...])
        hn = _rms(x1, g_ref[...]).astype(BF16)
        acc_ref[...] = x1
        for c in range(D_FF // FF_CHUNK):
            cols = slice(c * FF_CHUNK, (c + 1) * FF_CHUNK)
            h = jnp.maximum(_dot(hn, wup_ref[:, cols]), 0.0)
            acc_ref[...] += _dot((h * h).astype(BF16), wdn_ref[cols, :])
        if final:
            o_ref[rows, :] = _rms(acc_ref[...], gf_ref[...])
        else:
            o_ref[rows, :] = acc_ref[...]


def _out_mlp(x2, mix, qc, memkv, layer, w_o, g, w_up, w_dn, g_final, seq):
    t = x2.shape[0]
    tm = ROW_TILE
    per_b = seq // tm
    final = g_final is not None
    row = lambda w: pl.BlockSpec((tm, w), lambda i: (i, 0))
    in_specs = [row(D_MODEL),
                pl.BlockSpec((tm // SUB_TILE, MIX_WIDTH, SUB_TILE), lambda i: (i, 0, 0)),
                row(CROSS_WIDTH),
                pl.BlockSpec((N_MEM, CROSS_WIDTH), lambda i: (i // per_b, 2 * layer)),
                pl.BlockSpec((N_MEM, CROSS_WIDTH), lambda i: (i // per_b, 2 * layer + 1)),
                _const_spec(w_o.shape), _layer_spec(g, layer),
                _const_spec(w_up.shape), _const_spec(w_dn.shape)]
    args = [x2, mix, qc, memkv, memkv, w_o, g, w_up, w_dn]
    if final:
        in_specs.append(_const_spec(g_final.shape))
        args.append(g_final)
    return pl.pallas_call(
        functools.partial(_out_mlp_kernel, final=final),
        grid=(t // tm,),
        in_specs=in_specs,
        out_specs=row(D_MODEL),
        out_shape=jax.ShapeDtypeStruct((t, D_MODEL), F32),
        scratch_shapes=[pltpu.VMEM((SUB_TILE, D_MODEL), F32)],
        compiler_params=_params("parallel"),
        name="out_mlp",
    )(*args)


def _prep_mla(w_in, w_uq, w_ukv):
    n = w_in.shape[0]
    n1 = MLA_Q_RANK + MLA_KV_RANK
    w_in = w_in.astype(BF16)
    w_a = jnp.concatenate([w_in[:, :, :n1], w_in[:, :, n1 + MLA_ROPE:]], axis=2)
    w_kr_t = w_in[:, :, n1:n1 + MLA_ROPE].transpose(0, 2, 1)
    w_uq_t = w_uq.astype(BF16).transpose(0, 2, 1)
    kv = w_ukv.astype(BF16).reshape(n, MLA_KV_RANK, MLA_HEADS, 2 * HEAD_DIM)
    w_uk = kv[..., :MLA_NOPE].reshape(n, MLA_KV_RANK, MLA_HEADS * MLA_NOPE)
    w_uv_t = kv[..., MLA_NOPE:].reshape(n, MLA_KV_RANK, MIX_WIDTH).transpose(0, 2, 1)
    return w_a, w_kr_t, w_uq_t, w_uk, w_uv_t


def _prep_swa(w_in):
    nq = SWA_Q_HEADS * HEAD_DIM
    nk = SWA_KV_HEADS * HEAD_DIM
    w_in = w_in.astype(BF16)
    w_q_t = w_in[:, :, :nq].transpose(0, 2, 1)
    w_kc = jnp.concatenate([w_in[:, :, nq:nq + nk], w_in[:, :, nq + 2 * nk:]], axis=2)
    w_v_t = w_in[:, :, nq + nk:nq + 2 * nk].transpose(0, 2, 1)
    return w_q_t, w_kc, w_v_t


def kernel(x, mem, positions, attn_norm_g, mlp_norm_g, mem_norm_g, final_norm_g,
           mla_w_in, mla_q_norm_g, mla_kv_norm_g, mla_w_uq, mla_w_ukv,
           swa_w_in, swa_sinks, w_mem_kv, w_o, mlp_w_up, mlp_w_down):
    batch, seq, d = x.shape
    depth = attn_norm_g.shape[0]
    t = batch * seq
    x2 = x.reshape(t, d)

    w_mem = w_mem_kv.astype(BF16)
    w_mem = jnp.concatenate([w_mem[:, :, :CROSS_WIDTH] * (HEAD_DIM ** -0.5),
                             w_mem[:, :, CROSS_WIDTH:]], axis=2)
    memkv = _memkv(mem.reshape(batch * N_MEM, d), mem_norm_g.reshape(1, d), w_mem)

    inv = ROPE_THETA ** (-(jnp.arange(ROPE_HALF, dtype=F32) * 2.0) / MLA_ROPE)
    cos_t, sin_t = _rope_tables(positions.astype(F32).reshape(1, t), inv.reshape(ROPE_HALF, 1))

    pos_col_i = positions.reshape(t, 1)
    pos_row_i = positions.reshape(t // WINDOW, 1, WINDOW)
    slopes = 2.0 ** (-8.0 * (jnp.arange(SWA_Q_HEADS, dtype=F32) + 1.0) / SWA_Q_HEADS) * LOG2E

    g_attn = attn_norm_g.reshape(depth, 1, d)
    g_mlp = mlp_norm_g.reshape(depth, 1, d)
    mla_w = _prep_mla(mla_w_in, mla_w_uq, mla_w_ukv)
    mla_qg = mla_q_norm_g.reshape(-1, 1, MLA_Q_RANK)
    mla_kvg = mla_kv_norm_g.reshape(-1, 1, MLA_KV_RANK)
    swa_w = _prep_swa(swa_w_in)
    w_up_f = mlp_w_up.astype(F32)
    w_dn_f = mlp_w_down.astype(F32)
    w_o_f = w_o.astype(F32)
    sinks = swa_sinks.astype(F32) * LOG2E
    g_attn_mla, g_attn_swa = g_attn[0::2], g_attn[1::2]

    for i in range(depth):
        j = i // 2
        is_mla = i % 2 == 0
        if is_mla:
            w_a, w_kr_t, w_uq_t, w_uk, w_uv_t = mla_w
            q_t, k, v_t, qc, w_up, w_dn, w_o_b = _mla_proj(
                x2, j, g_attn_mla, w_a, w_kr_t, mla_qg, mla_kvg, w_uq_t, w_uk, w_uv_t,
                cos_t, sin_t, w_up_f, w_dn_f, w_o_f, i)
            mix = _mla_attn(q_t, k, v_t, batch, seq)
        else:
            q_t, k, v_t, qc, w_up, w_dn, w_o_b = _swa_proj(x2, j, g_attn_swa, *swa_w,
                                                         w_up_f, w_dn_f, w_o_f, i)
            mix = _swa_attn(slopes, sinks[j], q_t, k, v_t, pos_col_i, pos_row_i, batch, seq)
        g_final = final_norm_g.reshape(1, d) if i == depth - 1 else None
        x2 = _out_mlp(x2, mix, qc, memkv, i, w_o_b, g_mlp, w_up, w_dn, g_final, seq)
    return x2.reshape(batch, seq, d)
```

```python
import functools
import math

import jax
import jax.numpy as jnp
from jax import lax
from jax.experimental import pallas as pl
from jax.experimental.pallas import tpu as pltpu

F32 = jnp.float32
BF16 = jnp.bfloat16

D_MODEL = 1024
HEAD_DIM = 64
N_MEM = 256
CROSS_WIDTH = 256
MIX_WIDTH = 768
EPS = 1e-6
NEG = -1e30
LOG2E = math.log2(math.e)
MASK_DIST = 1e32
MLA_HEADS = 12
MLA_Q_RANK = 384
MLA_KV_RANK = 256
MLA_NOPE = 64
MLA_ROPE = 32
MLA_QK = MLA_NOPE + MLA_ROPE
ROPE_THETA = 10000.0
SWA_Q_HEADS = 12
SWA_KV_HEADS = 4
SWA_GROUP = 3
WINDOW = 128
D_FF = 4 * D_MODEL

LANES = 128
QK_PAD = LANES
ROPE_HALF = MLA_ROPE // 2
BF16_ROWS = 16
V_ROWS = HEAD_DIM + BF16_ROWS

SUB_TILE = 512
ROW_TILE = 2 * SUB_TILE
ATTN_TILE = SUB_TILE
FULL_BLOCKS_PER_TRIP = 14
DIAG_BLOCKS_PER_TRIP = 8
SWA_ROWS = ROW_TILE
SWA_LOOKAHEAD = 3
FF_CHUNK = 1024
VMEM_LIMIT = 48 * 1024 * 1024


def _rms(x, g):
    ms = jnp.mean(x * x, axis=-1, keepdims=True)
    return x * lax.rsqrt(ms + EPS) * g


def _dot(a, b):
    return jnp.dot(a, b, preferred_element_type=F32)


def _dot_nt(a, b):
    return lax.dot_general(a, b, (((1,), (1,)), ((), ())), preferred_element_type=F32)


def _low_half():
    return lax.broadcasted_iota(jnp.int32, (1, LANES), 1) < HEAD_DIM


def _const_spec(shape):
    nd = len(shape)
    return pl.BlockSpec(shape, lambda *_: (0,) * nd, pipeline_mode=pl.Buffered(1))


def _layer_spec(stacked, layer):
    nd = stacked.ndim
    return pl.BlockSpec((None,) + stacked.shape[1:], lambda *_: (layer,) + (0,) * (nd - 1),
                        pipeline_mode=pl.Buffered(1))


def _mlp_weight_cast_specs(w_up, w_dn, layer, n_steps):
    d, ff = w_up.shape[1:]
    up_rows, dn_rows = d // n_steps, ff // n_steps
    in_specs = [pl.BlockSpec((None, up_rows, ff), lambda i: (layer, i, 0)),
                pl.BlockSpec((None, dn_rows, d), lambda i: (layer, i, 0)),
                pl.BlockSpec((None, up_rows, d), lambda i: (layer, i, 0))]
    out_specs = [pl.BlockSpec((up_rows, ff), lambda i: (i, 0)),
                 pl.BlockSpec((dn_rows, d), lambda i: (i, 0)),
                 pl.BlockSpec((up_rows, d), lambda i: (i, 0))]
    out_shape = [jax.ShapeDtypeStruct((d, ff), BF16), jax.ShapeDtypeStruct((ff, d), BF16),
                 jax.ShapeDtypeStruct((d, d), BF16)]
    return in_specs, out_specs, out_shape


def _params(*sem):
    return pltpu.CompilerParams(dimension_semantics=sem, vmem_limit_bytes=VMEM_LIMIT)


def _rope_table_kernel(pos_ref, inv_ref, cos_ref, sin_ref):
    ang = inv_ref[...] * pos_ref[...]
    cos_ref[...] = jnp.cos(ang)
    sin_ref[...] = jnp.sin(ang)


def _rope_tables(pos_row, inv_col):
    t = pos_row.shape[1]
    tn = 2048
    return pl.pallas_call(
        _rope_table_kernel,
        grid=(t // tn,),
        in_specs=[pl.BlockSpec((1, tn), lambda i: (0, i)),
                  pl.BlockSpec((ROPE_HALF, 1), lambda i: (0, 0))],
        out_specs=[pl.BlockSpec((ROPE_HALF, tn), lambda i: (0, i))] * 2,
        out_shape=[jax.ShapeDtypeStruct((ROPE_HALF, t), F32)] * 2,
        compiler_params=_params("parallel"),
        name="rope_tables",
    )(pos_row, inv_col)


def _memkv_kernel(mem_ref, g_ref, w_ref, o_ref):
    mn = _rms(mem_ref[...], g_ref[...]).astype(BF16)
    ncol = w_ref.shape[2]
    for layer in range(w_ref.shape[0]):
        o_ref[:, layer * ncol:(layer + 1) * ncol] = _dot(mn, w_ref[layer]).astype(BF16)


def _memkv(mem2, g, w_all):
    rows = mem2.shape[0]
    depth, _, ncol = w_all.shape
    return pl.pallas_call(
        _memkv_kernel,
        grid=(rows // N_MEM,),
        in_specs=[pl.BlockSpec((N_MEM, D_MODEL), lambda b: (b, 0)),
                  _const_spec((1, D_MODEL)), _const_spec(w_all.shape)],
        out_specs=pl.BlockSpec((N_MEM, depth * ncol), lambda b: (b, 0)),
        out_shape=jax.ShapeDtypeStruct((rows, depth * ncol), BF16),
        compiler_params=_params("parallel"),
        name="mem_kv",
    )(mem2, g, w_all)


def _mla_proj_kernel(x_ref, g_ref, wa_ref, wkr_ref, qg_ref, kvg_ref, wuq_ref, wuk_ref,
                     wuv_ref, cos_ref, sin_ref, wupf_ref, wdnf_ref, wof_ref,
                     q_ref, k_ref, v_ref, qc_ref, wup_ref, wdn_ref, wo_ref):
    wup_ref[...] = wupf_ref[...].astype(BF16)
    wdn_ref[...] = wdnf_ref[...].astype(BF16)
    wo_ref[...] = wof_ref[...].astype(BF16)
    half = SUB_TILE
    scale = MLA_QK ** -0.5 * LOG2E
    lo = _low_half()

    def compress(rows):
        hn = _rms(x_ref[rows, :], g_ref[...]).astype(BF16)
        return hn, _dot(hn, wa_ref[...])

    def expand(part, hn, proj):
        rows = slice(part * half, (part + 1) * half)
        c_q = proj[:, :MLA_Q_RANK]
        c_kv = proj[:, MLA_Q_RANK:MLA_Q_RANK + MLA_KV_RANK]
        qc_ref[rows, :] = proj[:, MLA_Q_RANK + MLA_KV_RANK:].astype(BF16)
        cqn = _rms(c_q, qg_ref[...]).astype(BF16)
        ckvn = _rms(c_kv, kvg_ref[...]).astype(BF16)
        cos = cos_ref[:, rows]
        sin = sin_ref[:, rows]

        def rope(x1, x2):
            return x1 * cos - x2 * sin, x1 * sin + x2 * cos

        kr_t = _dot_nt(wkr_ref[...], hn)
        r1, r2 = rope(kr_t[:ROPE_HALF], kr_t[ROPE_HALF:])
        k_rope = jnp.concatenate(
            [jnp.zeros((MLA_NOPE, half), F32), r1, r2,
             jnp.zeros((QK_PAD - MLA_QK, half), F32)], axis=0).T

        k_nope = _dot(ckvn, wuk_ref[...])
        for pair in range(MLA_HEADS // 2):
            both = k_nope[:, pair * LANES:(pair + 1) * LANES]
            for parity, tile in enumerate((both, pltpu.roll(both, HEAD_DIM, 1))):
                h = 2 * pair + parity
                k_ref[rows, h * QK_PAD:(h + 1) * QK_PAD] = jnp.where(lo, tile, k_rope).astype(BF16)

        v_t = _dot_nt(wuv_ref[...], ckvn).astype(BF16)
        ones_rows = (lax.broadcasted_iota(jnp.int32, (BF16_ROWS, half), 0) == 0).astype(BF16)
        for h in range(MLA_HEADS):
            v_ref[part, h * V_ROWS:h * V_ROWS + HEAD_DIM, :] = v_t[h * HEAD_DIM:(h + 1) * HEAD_DIM]
            v_ref[part, h * V_ROWS + HEAD_DIM:(h + 1) * V_ROWS, :] = ones_rows

        q_t = _dot_nt(wuq_ref[...], cqn)
        pad = jnp.zeros((QK_PAD - MLA_QK, half), BF16)
        for h in range(MLA_HEADS):
            src = h * MLA_QK
            dst = h * QK_PAD
            r1, r2 = rope(q_t[src + MLA_NOPE:src + MLA_NOPE + ROPE_HALF],
                          q_t[src + MLA_NOPE + ROPE_HALF:src + MLA_QK])
            q_ref[part, dst:dst + MLA_NOPE, :] = (q_t[src:src + MLA_NOPE] * scale).astype(BF16)
            q_ref[part, dst + MLA_NOPE:dst + MLA_NOPE + ROPE_HALF, :] = (r1 * scale).astype(BF16)
            q_ref[part, dst + MLA_NOPE + ROPE_HALF:dst + MLA_QK, :] = (r2 * scale).astype(BF16)
            q_ref[part, dst + MLA_QK:dst + QK_PAD, :] = pad

    n_sub = x_ref.shape[0] // half
    compressed = [compress(slice(p * half, (p + 1) * half)) for p in range(n_sub)]
    for p in range(n_sub):
        expand(p, *compressed[p])


def _mla_proj(x2, layer, g, w_a, w_kr, qg, kvg, w_uq, w_uk, w_uv, cos_t, sin_t,
              w_up, w_dn, w_o, mlp_layer):
    t = x2.shape[0]
    tm = ROW_TILE
    qw = MLA_HEADS * QK_PAD
    row = lambda w: pl.BlockSpec((tm, w), lambda i: (i, 0))
    col = lambda r: pl.BlockSpec((r, tm), lambda i: (0, i))
    n_sub = tm // SUB_TILE
    tile = lambda r: pl.BlockSpec((n_sub, r, SUB_TILE), lambda i: (i, 0, 0))
    consts = [g, w_a, w_kr, qg, kvg, w_uq, w_uk, w_uv]
    cast_in, cast_out, cast_shape = _mlp_weight_cast_specs(w_up, w_dn, mlp_layer, t // tm)
    return pl.pallas_call(
        _mla_proj_kernel,
        grid=(t // tm,),
        in_specs=[row(D_MODEL)] + [_layer_spec(c, layer) for c in consts]
                 + [col(ROPE_HALF), col(ROPE_HALF)] + cast_in,
        out_specs=[tile(qw), row(qw), tile(MLA_HEADS * V_ROWS), row(CROSS_WIDTH)] + cast_out,
        out_shape=[jax.ShapeDtypeStruct((t // SUB_TILE, qw, SUB_TILE), BF16),
                   jax.ShapeDtypeStruct((t, qw), BF16),
                   jax.ShapeDtypeStruct((t // SUB_TILE, MLA_HEADS * V_ROWS, SUB_TILE), BF16),
                   jax.ShapeDtypeStruct((t, CROSS_WIDTH), BF16)] + cast_shape,
        compiler_params=_params("parallel"),
        name="mla_proj",
    )(x2, *consts, cos_t, sin_t, w_up, w_dn, w_o)


def _mla_attn_kernel(q_ref, k_ref, v_ref, o_ref, s_ref, mb_ref, m_ref, acc_ref, bias_ref,
                     *, nt):
    t = q_ref.shape[2]
    half = t // 2
    acc_ref[...] = jnp.zeros(acc_ref.shape, F32)
    key = lax.broadcasted_iota(jnp.int32, (half, half), 0)
    qry = lax.broadcasted_iota(jnp.int32, (half, half), 1)
    bias_ref[...] = jnp.where(key <= qry, 0.0, NEG)

    def head_rows(h):
        return slice(h * QK_PAD, (h + 1) * QK_PAD)

    def scores(i, j, slot, h):
        start = pl.multiple_of(j * t, t)
        kb = k_ref[pl.ds(start, t), head_rows(h)]
        s = _dot(kb, q_ref[i, head_rows(h), :])
        s_ref[slot, h] = s
        mb_ref[slot, h] = jnp.max(s, axis=0, keepdims=True)

    def consume(i, j, slot, h):
        s = s_ref[slot, h]
        m_prev = jnp.where(j == 0, NEG, m_ref[i, h])
        m_new = jnp.maximum(m_prev, mb_ref[slot, h])
        a = jnp.exp2(m_prev - m_new)
        p = jnp.exp2(s - m_new).astype(BF16)
        rows = slice(h * V_ROWS, (h + 1) * V_ROWS)
        m_ref[i, h] = m_new
        acc_ref[i, rows, :] = a * acc_ref[i, rows, :] + _dot(v_ref[j, rows, :], p)

    def scores_diag(i, slot, h):
        start = pl.multiple_of(i * t, t)
        q = q_ref[i, head_rows(h), :]
        s_top = _dot(k_ref[pl.ds(start, half), head_rows(h)], q)
        s_bot = _dot(k_ref[pl.ds(start + half, half), head_rows(h)], q[:, half:])
        tri = bias_ref[...]
        s_left = s_top[:, :half] + tri
        s_bot = s_bot + tri
        s_ref[slot, h, :half, :half] = s_left
        s_ref[slot, h, :half, half:] = s_top[:, half:]
        s_ref[slot, h, half:, half:] = s_bot
        mb_ref[slot, h, :, :half] = jnp.max(s_left, axis=0, keepdims=True)
        mb_ref[slot, h, :, half:] = jnp.maximum(
            jnp.max(s_top[:, half:], axis=0, keepdims=True), jnp.max(s_bot, axis=0, keepdims=True))

    def consume_diag(i, slot, h):
        m_prev = jnp.where(i == 0, NEG, m_ref[i, h])
        m_new = jnp.maximum(m_prev, mb_ref[slot, h])
        a = jnp.exp2(m_prev - m_new)
        p_top = jnp.exp2(s_ref[slot, h, :half, :] - m_new).astype(BF16)
        p_bot = jnp.exp2(s_ref[slot, h, half:, half:] - m_new[:, half:]).astype(BF16)
        rows = slice(h * V_ROWS, (h + 1) * V_ROWS)
        acc = a * acc_ref[i, rows, :] + _dot(v_ref[i, rows, :half], p_top)
        acc_r = acc[:, half:] + _dot(v_ref[i, rows, half:], p_bot)
        out = slice(h * HEAD_DIM, (h + 1) * HEAD_DIM)
        o_ref[i, out, :half] = (acc[:HEAD_DIM, :half]
                                / acc[HEAD_DIM:HEAD_DIM + 1, :half]).astype(BF16)
        o_ref[i, out, half:] = (acc_r[:HEAD_DIM] / acc_r[HEAD_DIM:HEAD_DIM + 1]).astype(BF16)

    def next_full(i, j):
        wrap = j + 1 == i
        i_n = jnp.minimum(jnp.where(wrap, i + 1, i), nt - 1)
        return i_n, jnp.where(wrap, 0, j + 1)

    def full_trip(_, cur):
        for b in range(FULL_BLOCKS_PER_TRIP):
            nxt = next_full(*cur)
            for h in range(2):
                scores(*nxt, 1 - b % 2, h)
                consume(*cur, b % 2, h)
            cur = nxt
        return cur

    def diag_trip(_, i):
        for b in range(DIAG_BLOCKS_PER_TRIP):
            i_n = jnp.minimum(i + 1, nt - 1)
            for h in range(2):
                scores_diag(i_n, 1 - b % 2, h)
                consume_diag(i, b % 2, h)
            i = i + 1
        return i

    for h in range(2):
        scores(1, 0, 0, h)
    lax.fori_loop(0, nt * (nt - 1) // 2 // FULL_BLOCKS_PER_TRIP, full_trip,
                  (jnp.int32(1), jnp.int32(0)))
    for h in range(2):
        scores_diag(0, 0, h)
    lax.fori_loop(0, nt // DIAG_BLOCKS_PER_TRIP, diag_trip, jnp.int32(0))


def _mla_attn(q_t, k, v_t, batch, seq):
    t = ATTN_TILE
    nt = seq // t
    assert FULL_BLOCKS_PER_TRIP % 2 == 0 and DIAG_BLOCKS_PER_TRIP % 2 == 0
    assert (nt * (nt - 1) // 2) % FULL_BLOCKS_PER_TRIP == 0 and nt % DIAG_BLOCKS_PER_TRIP == 0
    pairs = MLA_HEADS // 2
    return pl.pallas_call(
        functools.partial(_mla_attn_kernel, nt=nt),
        grid=(batch, pairs),
        in_specs=[pl.BlockSpec((nt, 2 * QK_PAD, t), lambda b, p: (b, p, 0)),
                  pl.BlockSpec((seq, 2 * QK_PAD), lambda b, p: (b, p)),
                  pl.BlockSpec((nt, 2 * V_ROWS, t), lambda b, p: (b, p, 0))],
        out_specs=pl.BlockSpec((nt, LANES, t), lambda b, p: (b, p, 0)),
        out_shape=jax.ShapeDtypeStruct((batch * nt, MIX_WIDTH, t), BF16),
        scratch_shapes=[pltpu.VMEM((2, 2, t, t), F32), pltpu.VMEM((2, 2, 1, t), F32),
                        pltpu.VMEM((nt, 2, 1, t), F32), pltpu.VMEM((nt, 2 * V_ROWS, t), F32),
                        pltpu.VMEM((t // 2, t // 2), F32)],
        compiler_params=_params("parallel", "parallel"),
        name="mla_attn",
    )(q_t, k, v_t)


def _swa_proj_kernel(x_ref, g_ref, wq_ref, wkc_ref, wv_ref, wupf_ref, wdnf_ref, wof_ref,
                     q_ref, k_ref, v_ref, qc_ref, wup_ref, wdn_ref, wo_ref):
    wup_ref[...] = wupf_ref[...].astype(BF16)
    wdn_ref[...] = wdnf_ref[...].astype(BF16)
    wo_ref[...] = wof_ref[...].astype(BF16)
    half = SUB_TILE
    nk = SWA_KV_HEADS * HEAD_DIM
    ones_rows = (lax.broadcasted_iota(jnp.int32, (BF16_ROWS, WINDOW), 0) == 0).astype(BF16)
    norms = [_rms(x_ref[p * half:(p + 1) * half, :], g_ref[...]).astype(BF16)
             for p in range(x_ref.shape[0] // half)]
    for part, hn in enumerate(norms):
        rows = slice(part * half, (part + 1) * half)
        q_ref[part] = (_dot_nt(wq_ref[...], hn) * (HEAD_DIM ** -0.5 * LOG2E)).astype(BF16)
        kc = _dot(hn, wkc_ref[...])
        k_ref[rows, :] = kc[:, :nk].astype(BF16)
        qc_ref[rows, :] = kc[:, nk:].astype(BF16)
        v_t = _dot_nt(wv_ref[...], hn).astype(BF16)
        for w in range(half // WINDOW):
            lanes = slice(w * WINDOW, (w + 1) * WINDOW)
            win = part * (half // WINDOW) + w
            for h in range(SWA_KV_HEADS):
                v_ref[win, h * V_ROWS:h * V_ROWS + HEAD_DIM, :] = v_t[
                    h * HEAD_DIM:(h + 1) * HEAD_DIM, lanes]
                v_ref[win, h * V_ROWS + HEAD_DIM:(h + 1) * V_ROWS, :] = ones_rows


def _swa_proj(x2, layer, g, w_q_t, w_kc, w_v_t, w_up, w_dn, w_o, mlp_layer):
    t = x2.shape[0]
    tm = ROW_TILE
    nk = SWA_KV_HEADS * HEAD_DIM
    nwin = tm // WINDOW
    row = lambda w: pl.BlockSpec((tm, w), lambda i: (i, 0))
    consts = [g, w_q_t, w_kc, w_v_t]
    cast_in, cast_out, cast_shape = _mlp_weight_cast_specs(w_up, w_dn, mlp_layer, t // tm)
    return pl.pallas_call(
        _swa_proj_kernel,
        grid=(t // tm,),
        in_specs=[row(D_MODEL)] + [_layer_spec(c, layer) for c in consts] + cast_in,
        out_specs=[pl.BlockSpec((tm // SUB_TILE, MIX_WIDTH, SUB_TILE), lambda i: (i, 0, 0)),
                   row(nk),
                   pl.BlockSpec((nwin, SWA_KV_HEADS * V_ROWS, WINDOW), lambda i: (i, 0, 0)),
                   row(CROSS_WIDTH)] + cast_out,
        out_shape=[jax.ShapeDtypeStruct((t // SUB_TILE, MIX_WIDTH, SUB_TILE), BF16),
                   jax.ShapeDtypeStruct((t, nk), BF16),
                   jax.ShapeDtypeStruct((t // WINDOW, SWA_KV_HEADS * V_ROWS, WINDOW), BF16),
                   jax.ShapeDtypeStruct((t, CROSS_WIDTH), BF16)] + cast_shape,
        compiler_params=_params("parallel"),
        name="swa_proj",
    )(x2, *consts, w_up, w_dn, w_o)


def _swa_attn_kernel(slope_ref, sink_ref, q_ref, ko_ref, kp_ref, vo_ref, vp_ref,
                     pqo_ref, pko_ref, pkp_ref, o_ref, *, nwin):
    i = pl.program_id(1)
    w_ = WINDOW
    kcat = jnp.concatenate([kp_ref[...], ko_ref[...]], axis=0)
    pos_k = jnp.concatenate([pkp_ref[...], pko_ref[...]], axis=0)
    kj = lax.broadcasted_iota(jnp.int32, (2 * w_, w_), 0)
    qi = lax.broadcasted_iota(jnp.int32, (2 * w_, w_), 1)
    rel = w_ + qi - kj
    band = (rel >= 0) & (rel < w_)
    first_band = band & ((kj >= w_) | (i > 0))
    zeros = jnp.zeros((HEAD_DIM, w_), BF16)

    def window_keys(w):
        return slice(w * w_, (w + 2) * w_)

    def tile_lanes(w):
        tile, off = divmod(w * w_, SUB_TILE)
        return tile, slice(off, off + w_)

    def scores(w, kh):
        k_pair = kcat[window_keys(w), (kh // 2) * LANES:(kh // 2 + 1) * LANES]
        qs = []
        for g in range(SWA_GROUP):
            hq = kh * SWA_GROUP + g
            tile, lanes = tile_lanes(w)
            qh = q_ref[tile, hq * HEAD_DIM:(hq + 1) * HEAD_DIM, lanes]
            qs.append(jnp.concatenate([qh, zeros] if kh % 2 == 0 else [zeros, qh], axis=0))
        return _dot(k_pair, jnp.concatenate(qs, axis=1))

    dists = {}

    def masked_dist(w):
        if w not in dists:
            dists[w] = jnp.where(first_band if w == 0 else band,
                                 (pqo_ref[w] - pos_k[window_keys(w)]).astype(F32), MASK_DIST)
        return dists[w]

    def finish(w, kh, s3):
        dist = masked_dist(w)
        v_prev = vp_ref[0] if w == 0 else vo_ref[w - 1]
        rows = slice(kh * V_ROWS, (kh + 1) * V_ROWS)
        v_win = jnp.concatenate([v_prev[rows], vo_ref[w, rows, :]], axis=1)
        ps = []
        sink_terms = []
        for g in range(SWA_GROUP):
            hq = kh * SWA_GROUP + g
            sink = sink_ref[hq]
            s = s3[:, g * w_:(g + 1) * w_] - slope_ref[hq] * dist
            m = jnp.maximum(jnp.max(s, axis=0, keepdims=True), sink)
            ps.append(jnp.exp2(s - m).astype(BF16))
            sink_terms.append(jnp.exp2(sink - m))
        o3 = _dot(v_win, jnp.concatenate(ps, axis=1))
        tile, lanes = tile_lanes(w)
        for g in range(SWA_GROUP):
            hq = kh * SWA_GROUP + g
            og = o3[:, g * w_:(g + 1) * w_]
            o_ref[tile, hq * HEAD_DIM:(hq + 1) * HEAD_DIM, lanes] = (
                og[:HEAD_DIM] / (og[HEAD_DIM:HEAD_DIM + 1] + sink_terms[g])).astype(BF16)

    units = [(w, kh) for w in range(nwin) for kh in range(SWA_KV_HEADS)]
    pending = {}
    for idx in range(len(units) + SWA_LOOKAHEAD):
        if idx < len(units):
            pending[idx] = scores(*units[idx])
        if idx >= SWA_LOOKAHEAD:
            done = idx - SWA_LOOKAHEAD
            finish(*units[done], pending.pop(done))


def _swa_attn(slopes, sinks, q_t, k, v_t, pos_col, pos_row, batch, seq):
    rows = SWA_ROWS
    nwin = rows // WINDOW
    nsteps = seq // rows
    nblk = seq // WINDOW
    nk = SWA_KV_HEADS * HEAD_DIM
    vr = SWA_KV_HEADS * V_ROWS
    own = lambda b, i: (b * nsteps + i, 0)
    own3 = lambda b, i: (b * nsteps + i, 0, 0)
    prev = lambda b, i: (b * nblk + jnp.maximum(i * nwin - 1, 0), 0)
    prev3 = lambda b, i: (b * nblk + jnp.maximum(i * nwin - 1, 0), 0, 0)
    smem = pl.BlockSpec(memory_space=pltpu.SMEM)
    return pl.pallas_call(
        functools.partial(_swa_attn_kernel, nwin=nwin),
        grid=(batch, nsteps),
        in_specs=[smem, smem,
                  pl.BlockSpec((rows // SUB_TILE, MIX_WIDTH, SUB_TILE), own3),
                  pl.BlockSpec((rows, nk), own),
                  pl.BlockSpec((WINDOW, nk), prev),
                  pl.BlockSpec((nwin, vr, WINDOW), own3),
                  pl.BlockSpec((1, vr, WINDOW), prev3),
                  pl.BlockSpec((nwin, 1, WINDOW), own3),
                  pl.BlockSpec((rows, 1), own),
                  pl.BlockSpec((WINDOW, 1), prev)],
        out_specs=pl.BlockSpec((rows // SUB_TILE, MIX_WIDTH, SUB_TILE), own3),
        out_shape=jax.ShapeDtypeStruct((batch * seq // SUB_TILE, MIX_WIDTH, SUB_TILE), BF16),
        compiler_params=_params("parallel", "parallel"),
        name="swa_attn",
    )(slopes, sinks, q_t, k, k, v_t, v_t, pos_row, pos_col, pos_col)


def _out_mlp_kernel(*refs, final):
    if final:
        (x_ref, mix_ref, qc_ref, km_ref, vm_ref, wo_ref, g_ref, wup_ref, wdn_ref,
         gf_ref, o_ref, acc_ref) = refs
    else:
        (x_ref, mix_ref, qc_ref, km_ref, vm_ref, wo_ref, g_ref, wup_ref, wdn_ref,
         o_ref, acc_ref) = refs
    lo = _low_half()
    n_pairs = CROSS_WIDTH // LANES
    for part in range(x_ref.shape[0] // SUB_TILE):
        rows = slice(part * SUB_TILE, (part + 1) * SUB_TILE)
        scores = []
        for pair in range(n_pairs):
            sl = slice(pair * LANES, (pair + 1) * LANES)
            qp = qc_ref[rows, sl]
            for half in range(2):
                sel = lo if half == 0 else jnp.logical_not(lo)
                scores.append(_dot_nt(jnp.where(sel, qp, jnp.zeros_like(qp)), km_ref[:, sl]))
        cross = []
        for pair in range(n_pairs):
            vp = vm_ref[:, pair * LANES:(pair + 1) * LANES]
            outs = []
            for half in range(2):
                s = scores[2 * pair + half]
                e = jnp.exp(s - jnp.max(s, axis=-1, keepdims=True))
                den = jnp.sum(e, axis=-1, keepdims=True)
                outs.append(_dot(e.astype(BF16), vp) / den)
            cross.append(jnp.where(lo, outs[0], outs[1]).astype(BF16))
        mix = mix_ref[part].astype(F32).T.astype(BF16)
        attn = jnp.concatenate([mix] + cross, axis=1)
        x1 = x_ref[rows, :] + _dot(attn, wo_ref[...])
        hn = _rms(x1, g_ref[...]).astype(BF16)
        acc_ref[...] = x1
        for c in range(D_FF // FF_CHUNK):
            cols = slice(c * FF_CHUNK, (c + 1) * FF_CHUNK)
            h = jnp.maximum(_dot(hn, wup_ref[:, cols]), 0.0)
            acc_ref[...] += _dot((h * h).astype(BF16), wdn_ref[cols, :])
        if final:
            o_ref[rows, :] = _rms(acc_ref[...], gf_ref[...])
        else:
            o_ref[rows, :] = acc_ref[...]


def _out_mlp(x2, mix, qc, memkv, layer, w_o, g, w_up, w_dn, g_final, seq):
    t = x2.shape[0]
    tm = ROW_TILE
    per_b = seq // tm
    final = g_final is not None
    row = lambda w: pl.BlockSpec((tm, w), lambda i: (i, 0))
    in_specs = [row(D_MODEL),
                pl.BlockSpec((tm // SUB_TILE, MIX_WIDTH, SUB_TILE), lambda i: (i, 0, 0)),
                row(CROSS_WIDTH),
                pl.BlockSpec((N_MEM, CROSS_WIDTH), lambda i: (i // per_b, 2 * layer)),
                pl.BlockSpec((N_MEM, CROSS_WIDTH), lambda i: (i // per_b, 2 * layer + 1)),
                _const_spec(w_o.shape), _layer_spec(g, layer),
                _const_spec(w_up.shape), _const_spec(w_dn.shape)]
    args = [x2, mix, qc, memkv, memkv, w_o, g, w_up, w_dn]
    if final:
        in_specs.append(_const_spec(g_final.shape))
        args.append(g_final)
    return pl.pallas_call(
        functools.partial(_out_mlp_kernel, final=final),
        grid=(t // tm,),
        in_specs=in_specs,
        out_specs=row(D_MODEL),
        out_shape=jax.ShapeDtypeStruct((t, D_MODEL), F32),
        scratch_shapes=[pltpu.VMEM((SUB_TILE, D_MODEL), F32)],
        compiler_params=_params("parallel"),
        name="out_mlp",
    )(*args)


def _prep_mla(w_in, w_uq, w_ukv):
    n = w_in.shape[0]
    n1 = MLA_Q_RANK + MLA_KV_RANK
    w_in = w_in.astype(BF16)
    w_a = jnp.concatenate([w_in[:, :, :n1], w_in[:, :, n1 + MLA_ROPE:]], axis=2)
    w_kr_t = w_in[:, :, n1:n1 + MLA_ROPE].transpose(0, 2, 1)
    w_uq_t = w_uq.astype(BF16).transpose(0, 2, 1)
    kv = w_ukv.astype(BF16).reshape(n, MLA_KV_RANK, MLA_HEADS, 2 * HEAD_DIM)
    w_uk = kv[..., :MLA_NOPE].reshape(n, MLA_KV_RANK, MLA_HEADS * MLA_NOPE)
    w_uv_t = kv[..., MLA_NOPE:].reshape(n, MLA_KV_RANK, MIX_WIDTH).transpose(0, 2, 1)
    return w_a, w_kr_t, w_uq_t, w_uk, w_uv_t


def _prep_swa(w_in):
    nq = SWA_Q_HEADS * HEAD_DIM
    nk = SWA_KV_HEADS * HEAD_DIM
    w_in = w_in.astype(BF16)
    w_q_t = w_in[:, :, :nq].transpose(0, 2, 1)
    w_kc = jnp.concatenate([w_in[:, :, nq:nq + nk], w_in[:, :, nq + 2 * nk:]], axis=2)
    w_v_t = w_in[:, :, nq + nk:nq + 2 * nk].transpose(0, 2, 1)
    return w_q_t, w_kc, w_v_t


def kernel(x, mem, positions, attn_norm_g, mlp_norm_g, mem_norm_g, final_norm_g,
           mla_w_in, mla_q_norm_g, mla_kv_norm_g, mla_w_uq, mla_w_ukv,
           swa_w_in, swa_sinks, w_mem_kv, w_o, mlp_w_up, mlp_w_down):
    batch, seq, d = x.shape
    depth = attn_norm_g.shape[0]
    t = batch * seq
    x2 = x.reshape(t, d)

    w_mem = w_mem_kv.astype(BF16)
    w_mem = jnp.concatenate([w_mem[:, :, :CROSS_WIDTH] * (HEAD_DIM ** -0.5),
                             w_mem[:, :, CROSS_WIDTH:]], axis=2)
    memkv = _memkv(mem.reshape(batch * N_MEM, d), mem_norm_g.reshape(1, d), w_mem)

    inv = ROPE_THETA ** (-(jnp.arange(ROPE_HALF, dtype=F32) * 2.0) / MLA_ROPE)
    cos_t, sin_t = _rope_tables(positions.astype(F32).reshape(1, t), inv.reshape(ROPE_HALF, 1))

    pos_col_i = positions.reshape(t, 1)
    pos_row_i = positions.reshape(t // WINDOW, 1, WINDOW)
    slopes = 2.0 ** (-8.0 * (jnp.arange(SWA_Q_HEADS, dtype=F32) + 1.0) / SWA_Q_HEADS) * LOG2E

    g_attn = attn_norm_g.reshape(depth, 1, d)
    g_mlp = mlp_norm_g.reshape(depth, 1, d)
    mla_w = _prep_mla(mla_w_in, mla_w_uq, mla_w_ukv)
    mla_qg = mla_q_norm_g.reshape(-1, 1, MLA_Q_RANK)
    mla_kvg = mla_kv_norm_g.reshape(-1, 1, MLA_KV_RANK)
    swa_w = _prep_swa(swa_w_in)
    w_up_f = mlp_w_up.astype(F32)
    w_dn_f = mlp_w_down.astype(F32)
    w_o_f = w_o.astype(F32)
    sinks = swa_sinks.astype(F32) * LOG2E
    g_attn_mla, g_attn_swa = g_attn[0::2], g_attn[1::2]

    for i in range(depth):
        j = i // 2
        is_mla = i % 2 == 0
        if is_mla:
            w_a, w_kr_t, w_uq_t, w_uk, w_uv_t = mla_w
            q_t, k, v_t, qc, w_up, w_dn, w_o_b = _mla_proj(
                x2, j, g_attn_mla, w_a, w_kr_t, mla_qg, mla_kvg, w_uq_t, w_uk, w_uv_t,
                cos_t, sin_t, w_up_f, w_dn_f, w_o_f, i)
            mix = _mla_attn(q_t, k, v_t, batch, seq)
        else:
            q_t, k, v_t, qc, w_up, w_dn, w_o_b = _swa_proj(x2, j, g_attn_swa, *swa_w,
                                                         w_up_f, w_dn_f, w_o_f, i)
            mix = _swa_attn(slopes, sinks[j], q_t, k, v_t, pos_col_i, pos_row_i, batch, seq)
        g_final = final_norm_g.reshape(1, d) if i == depth - 1 else None
        x2 = _out_mlp(x2, mix, qc, memkv, i, w_o_b, g_mlp, w_up, w_dn, g_final, seq)
    return x2.reshape(batch, seq, d)
```
